```python
import math
import jax
import jax.numpy as jnp
from jax import lax
import numpy as np

D_MODEL = 2048
BATCH = 2
SEQ = 16384
DEPTH = 2
DEC_BATCH = 8
DEC_SEQ = 32
PAST_LEN = 4096

CHUNK = 64
N_MEM = 256
D_MIX = D_MODEL
RWKV_WIDTH = D_MIX // 2
RWKV_HEAD = 64
RWKV_HEADS = RWKV_WIDTH // RWKV_HEAD
RANK_W = 64
RANK_A = 64
RANK_G = 128
RWKV_COLS = 3 * RWKV_WIDTH + RANK_W + RANK_A + RANK_G
S5_WIDTH = D_MIX - RWKV_WIDTH
S5_GROUP = 16
S5_GROUPS = S5_WIDTH // S5_GROUP
S5_STATE = 64
N_IN = RWKV_COLS + S5_WIDTH
XATTN_HEADS = 4
XATTN_HEAD = D_MODEL // XATTN_HEADS
D_FF = ((8 * D_MODEL // 3 + 255) // 256) * 256
ALPHA = (2.0 * DEPTH) ** 0.25
BETA = (8.0 * DEPTH) ** -0.25
LN_EPS = 1e-5
GN_EPS = 64e-5
DT_MIN = 1e-3
DT_MAX = 1e-1

kernel_name = 'hymba_rwkv7_s5_deepnorm_stream_step'

F32 = jnp.float32


def _layer_norm(x, g, b, eps=LN_EPS):
    xf = x.astype(F32)
    mu = jnp.mean(xf, axis=-1, keepdims=True)
    var = jnp.mean(jnp.square(xf - mu), axis=-1, keepdims=True)
    return ((xf - mu) * lax.rsqrt(var + eps) * g.astype(F32) + b.astype(F32)).astype(x.dtype)


def _swiglu(x, w_gate, w_up, w_down):
    return (jax.nn.silu(x @ w_gate) * (x @ w_up)) @ w_down


def _rwkv7_step(S, inp):
    r_t, w_t, k_t, v_t, kk_t, b_t = inp
    sa = jnp.einsum('bhvk,bhk->bhv', S, kk_t)
    S = S * w_t[:, :, None, :] - sa[..., None] * b_t[:, :, None, :] + v_t[..., None] * k_t[:, :, None, :]
    return S, jnp.einsum('bhvk,bhk->bhv', S, r_t)


def _rwkv7_mix(p, shift_prev, s0, lp):
    bsz, L, _ = p.shape
    prev = jnp.concatenate([shift_prev.astype(p.dtype), p[:, :-1]], axis=1)
    ps = (p + lp['shift_mu'] * (prev - p)).astype(F32)
    R = RWKV_WIDTH
    r, k, v, lw, la, lg = jnp.split(ps, [R, 2 * R, 3 * R, 3 * R + RANK_W, 3 * R + RANK_W + RANK_A], axis=-1)
    log_w = -math.exp(-0.5) * jax.nn.sigmoid(lp['w0'] + jnp.tanh(lw) @ lp['w_up'])
    a = jax.nn.sigmoid(lp['a0'] + la @ lp['a_up'])
    g = jax.nn.sigmoid(lg) @ lp['g_up']
    hs = lambda t: t.astype(F32).reshape(bsz, L, RWKV_HEADS, RWKV_HEAD)
    kk = hs(k * lp['k_k'])
    kk = kk * lax.rsqrt(jnp.maximum(jnp.sum(kk * kk, axis=-1, keepdims=True), 1e-24))
    k = k * (1.0 + (a - 1.0) * lp['k_a'])
    r_h, k_h, v_h, a_h, w_h = hs(r), hs(k), hs(v), hs(a), hs(jnp.exp(log_w))
    tm = lambda t: jnp.swapaxes(t, 0, 1)
    s_last, ys = lax.scan(_rwkv7_step, s0.astype(F32),
                          (tm(r_h), tm(w_h), tm(k_h), tm(v_h), tm(kk), tm(kk * a_h)))
    y = tm(ys)
    mu = jnp.mean(y, axis=-1, keepdims=True)
    var = jnp.mean(jnp.square(y - mu), axis=-1, keepdims=True)
    yn = (y - mu) * lax.rsqrt(var + GN_EPS) * lp['gn_w'].astype(F32).reshape(RWKV_HEADS, RWKV_HEAD) \
        + lp['gn_b'].astype(F32).reshape(RWKV_HEADS, RWKV_HEAD)
    bonus = jnp.sum(r_h * k_h * lp['r_k'].astype(F32), axis=-1, keepdims=True) * v_h
    out = (yn + bonus).reshape(bsz, L, R) * g
    return out.astype(p.dtype), p[:, -1:], s_last


def _lin_combine(e1, e2):
    a1, b1 = e1
    a2, b2 = e2
    return a1 * a2, a2 * b1 + b2


def _s5_mix(u, h_re, h_im, lp):
    bsz, L, _ = u.shape
    ug = u.astype(F32).reshape(bsz, L, S5_GROUPS, S5_GROUP)
    lam = lax.complex(lp['a_re'].astype(F32), lp['a_im'].astype(F32))
    dt = jnp.exp(lp['log_dt'].astype(F32))[:, None]
    lam_bar = jnp.exp(lam * dt)
    b_bar = ((lam_bar - 1.0) / lam)[..., None] * lax.complex(lp['b_re'].astype(F32), lp['b_im'].astype(F32))
    c_mat = lax.complex(lp['c_re'].astype(F32), lp['c_im'].astype(F32))
    blk = CHUNK if L % CHUNK == 0 else L
    nb = L // blk
    u_blocks = jnp.swapaxes(ug.reshape(bsz, nb, blk, S5_GROUPS, S5_GROUP), 0, 1)

    def block_step(h, ub):
        bu = jnp.einsum('gpc,btgc->btgp', b_bar, ub.astype(jnp.complex64))
        a = jnp.broadcast_to(lam_bar, bu.shape)
        a_cum, h_loc = lax.associative_scan(_lin_combine, (a, bu), axis=1)
        h_all = h_loc + a_cum * h[:, None]
        y = jnp.real(jnp.einsum('gcp,btgp->btgc', c_mat, h_all))
        return h_all[:, -1], y

    h0 = lax.complex(h_re.astype(F32), h_im.astype(F32))
    h_last, ys = lax.scan(block_step, h0, u_blocks)
    y = jnp.swapaxes(ys, 0, 1).reshape(bsz, L, S5_GROUPS, S5_GROUP) + lp['d'].astype(F32) * ug
    y = jax.nn.gelu(y.reshape(bsz, L, S5_WIDTH))
    out = y * jax.nn.sigmoid(y @ lp['glu_w'] + lp['glu_b'])
    return out.astype(u.dtype), jnp.real(h_last), jnp.imag(h_last)


def _cross_attn(x, mk, mv, wq, wo):
    bsz, L, _ = x.shape
    q = (x @ wq).reshape(bsz, L, XATTN_HEADS, XATTN_HEAD) * (XATTN_HEAD ** -0.5)
    s = jnp.einsum('blhd,bmhd->bhlm', q, mk.astype(q.dtype)).astype(F32)
    pr = jax.nn.softmax(s, axis=-1).astype(x.dtype)
    o = jnp.einsum('bhlm,bmhd->blhd', pr, mv.astype(x.dtype)).reshape(bsz, L, D_MODEL)
    return o @ wo


def _layer(x, mk, mv, shift_prev, s_rwkv, h_re, h_im, lp):
    x = _layer_norm(ALPHA * x + 0.5 * _swiglu(x, lp['f1g'], lp['f1u'], lp['f1d']), lp['ln_g'][0], lp['ln_b'][0])
    p = x @ lp['w_in']
    rw, new_shift, s_new = _rwkv7_mix(p[..., :RWKV_COLS], shift_prev, s_rwkv, lp)
    s5o, hr, hi = _s5_mix(p[..., RWKV_COLS:], h_re, h_im, lp)
    mix = jnp.concatenate([rw, s5o.astype(rw.dtype)], axis=-1) @ lp['w_out']
    x = _layer_norm(ALPHA * x + mix, lp['ln_g'][1], lp['ln_b'][1])
    x = _layer_norm(ALPHA * x + _cross_attn(x, mk, mv, lp['xq'], lp['xo']), lp['ln_g'][2], lp['ln_b'][2])
    x = _layer_norm(ALPHA * x + 0.5 * _swiglu(x, lp['f2g'], lp['f2u'], lp['f2d']), lp['ln_g'][3], lp['ln_b'][3])
    return x, new_shift, s_new, hr, hi


def setup_inputs(seed: int = 0) -> dict:
    key = jax.random.key(seed)
    ks = iter(jax.random.split(key, 64))

    def nrm(shape, scale):
        return jax.random.normal(next(ks), shape, F32) * scale

    def uni(shape, lo, hi):
        return jax.random.uniform(next(ks), shape, F32, lo, hi)

    Ld = DEPTH
    return {
        'x_prompt': nrm((BATCH, SEQ, D_MODEL), 1.0),
        'x_sample': nrm((DEC_BATCH, DEC_SEQ, D_MODEL), 1.0),
        'mem_prompt': nrm((BATCH, N_MEM, D_MODEL), 1.0),
        'cache_mem_k': nrm((Ld, DEC_BATCH, N_MEM, XATTN_HEADS, XATTN_HEAD), 1.0),
        'cache_mem_v': nrm((Ld, DEC_BATCH, N_MEM, XATTN_HEADS, XATTN_HEAD), 1.0),
        'state_rwkv': nrm((Ld, DEC_BATCH, RWKV_HEADS, RWKV_HEAD, RWKV_HEAD), 0.3),
        'cache_shift': nrm((Ld, DEC_BATCH, 1, RWKV_COLS), 1.0),
        'state_s5_re': nrm((Ld, DEC_BATCH, S5_GROUPS, S5_STATE), 0.1),
        'state_s5_im': nrm((Ld, DEC_BATCH, S5_GROUPS, S5_STATE), 0.1),
        'ln_g': 1.0 + nrm((Ld, 4, D_MODEL), 0.01),
        'ln_b': nrm((Ld, 4, D_MODEL), 0.01),
        'ffn1_gate': nrm((Ld, D_MODEL, D_FF), D_MODEL ** -0.5),
        'ffn1_up': nrm((Ld, D_MODEL, D_FF), D_MODEL ** -0.5),
        'ffn1_down': nrm((Ld, D_FF, D_MODEL), BETA * D_FF ** -0.5),
        'w_in': nrm((Ld, D_MODEL, N_IN), D_MODEL ** -0.5),
        'shift_mu': uni((Ld, RWKV_COLS), 0.0, 1.0),
        'rwkv_w0': nrm((Ld, RWKV_WIDTH), 0.5),
        'rwkv_w_up': nrm((Ld, RANK_W, RWKV_WIDTH), RANK_W ** -0.5),
        'rwkv_a0': nrm((Ld, RWKV_WIDTH), 0.5),
        'rwkv_a_up': nrm((Ld, RANK_A, RWKV_WIDTH), RANK_A ** -0.5),
        'rwkv_g_up': nrm((Ld, RANK_G, RWKV_WIDTH), RANK_G ** -0.5),
        'rwkv_k_k': 0.85 + nrm((Ld, RWKV_WIDTH), 0.05),
        'rwkv_k_a': 1.0 + nrm((Ld, RWKV_WIDTH), 0.05),
        'rwkv_r_k': nrm((Ld, RWKV_HEADS, RWKV_HEAD), 0.1),
        'rwkv_gn_w': 1.0 + nrm((Ld, RWKV_WIDTH), 0.01),
        'rwkv_gn_b': nrm((Ld, RWKV_WIDTH), 0.01),
        's5_a_re': -0.5 + nrm((Ld, S5_GROUPS, S5_STATE), 0.01),
        's5_a_im': jnp.broadcast_to(jnp.pi * jnp.arange(S5_STATE, dtype=F32), (Ld, S5_GROUPS, S5_STATE)) + nrm((Ld, S5_GROUPS, S5_STATE), 0.01),
        's5_log_dt': uni((Ld, S5_GROUPS), math.log(DT_MIN), math.log(DT_MAX)),
        's5_b_re': nrm((Ld, S5_GROUPS, S5_STATE, S5_GROUP), (2.0 * S5_GROUP) ** -0.5),
        's5_b_im': nrm((Ld, S5_GROUPS, S5_STATE, S5_GROUP), (2.0 * S5_GROUP) ** -0.5),
        's5_c_re': nrm((Ld, S5_GROUPS, S5_GROUP, S5_STATE), (2.0 * S5_STATE) ** -0.5),
        's5_c_im': nrm((Ld, S5_GROUPS, S5_GROUP, S5_STATE), (2.0 * S5_STATE) ** -0.5),
        's5_d': nrm((Ld, S5_GROUPS, S5_GROUP), 0.5),
        's5_glu_w': nrm((Ld, S5_WIDTH, S5_WIDTH), S5_WIDTH ** -0.5),
        's5_glu_b': nrm((Ld, S5_WIDTH), 0.01),
        'w_mix_out': nrm((Ld, D_MIX, D_MODEL), BETA * D_MIX ** -0.5),
        'xattn_q': nrm((Ld, D_MODEL, D_MODEL), D_MODEL ** -0.5),
        'xattn_k': nrm((Ld, D_MODEL, D_MODEL), D_MODEL ** -0.5),
        'xattn_v': nrm((Ld, D_MODEL, D_MODEL), D_MODEL ** -0.5),
        'xattn_o': nrm((Ld, D_MODEL, D_MODEL), BETA * D_MODEL ** -0.5),
        'ffn2_gate': nrm((Ld, D_MODEL, D_FF), D_MODEL ** -0.5),
        'ffn2_up': nrm((Ld, D_MODEL, D_FF), D_MODEL ** -0.5),
        'ffn2_down': nrm((Ld, D_FF, D_MODEL), BETA * D_FF ** -0.5),
    }


def reference(x_prompt, x_sample, mem_prompt, cache_mem_k, cache_mem_v, state_rwkv, cache_shift,
              state_s5_re, state_s5_im, ln_g, ln_b, ffn1_gate, ffn1_up, ffn1_down, w_in, shift_mu,
              rwkv_w0, rwkv_w_up, rwkv_a0, rwkv_a_up, rwkv_g_up, rwkv_k_k, rwkv_k_a, rwkv_r_k,
              rwkv_gn_w, rwkv_gn_b, s5_a_re, s5_a_im, s5_log_dt, s5_b_re, s5_b_im, s5_c_re, s5_c_im,
              s5_d, s5_glu_w, s5_glu_b, w_mix_out, xattn_q, xattn_k, xattn_v, xattn_o,
              ffn2_gate, ffn2_up, ffn2_down):
    bp = x_prompt.shape[0]
    xp, xs = x_prompt, x_sample
    mk_p_l, mv_p_l, rw_p_l, sh_p_l, hr_p_l, hi_p_l = [], [], [], [], [], []
    rw_s_l, sh_s_l, hr_s_l, hi_s_l = [], [], [], []
    for l in range(DEPTH):
        lp = dict(ln_g=ln_g[l], ln_b=ln_b[l], f1g=ffn1_gate[l], f1u=ffn1_up[l], f1d=ffn1_down[l],
                  w_in=w_in[l], shift_mu=shift_mu[l], w0=rwkv_w0[l], w_up=rwkv_w_up[l], a0=rwkv_a0[l],
                  a_up=rwkv_a_up[l], g_up=rwkv_g_up[l], k_k=rwkv_k_k[l], k_a=rwkv_k_a[l], r_k=rwkv_r_k[l],
                  gn_w=rwkv_gn_w[l], gn_b=rwkv_gn_b[l], a_re=s5_a_re[l], a_im=s5_a_im[l], log_dt=s5_log_dt[l],
                  b_re=s5_b_re[l], b_im=s5_b_im[l], c_re=s5_c_re[l], c_im=s5_c_im[l], d=s5_d[l],
                  glu_w=s5_glu_w[l], glu_b=s5_glu_b[l], w_out=w_mix_out[l], xq=xattn_q[l], xo=xattn_o[l],
                  f2g=ffn2_gate[l], f2u=ffn2_up[l], f2d=ffn2_down[l])
        mk_p = (mem_prompt @ xattn_k[l]).reshape(bp, N_MEM, XATTN_HEADS, XATTN_HEAD)
        mv_p = (mem_prompt @ xattn_v[l]).reshape(bp, N_MEM, XATTN_HEADS, XATTN_HEAD)
        xp, sh_p, rw_p, hr_p, hi_p = _layer(
            xp, mk_p, mv_p,
            jnp.zeros((bp, 1, RWKV_COLS), xp.dtype),
            jnp.zeros((bp, RWKV_HEADS, RWKV_HEAD, RWKV_HEAD), F32),
            jnp.zeros((bp, S5_GROUPS, S5_STATE), F32),
            jnp.zeros((bp, S5_GROUPS, S5_STATE), F32), lp)
        xs, sh_s, rw_s, hr_s, hi_s = _layer(
            xs, cache_mem_k[l], cache_mem_v[l], cache_shift[l], state_rwkv[l],
            state_s5_re[l], state_s5_im[l], lp)
        mk_p_l.append(mk_p); mv_p_l.append(mv_p); rw_p_l.append(rw_p); sh_p_l.append(sh_p)
        hr_p_l.append(hr_p); hi_p_l.append(hi_p)
        rw_s_l.append(rw_s); sh_s_l.append(sh_s); hr_s_l.append(hr_s); hi_s_l.append(hi_s)
    return (xp, xs,
            jnp.stack(mk_p_l), jnp.stack(mv_p_l), jnp.stack(rw_p_l), jnp.stack(sh_p_l),
            jnp.stack(hr_p_l), jnp.stack(hi_p_l),
            jnp.stack(rw_s_l), jnp.stack(sh_s_l), jnp.stack(hr_s_l), jnp.stack(hi_s_l))
```

```python
import functools
import math

import jax
import jax.numpy as jnp
from jax import lax
from jax.experimental import pallas as pl
from jax.experimental.pallas import tpu as pltpu

F32 = jnp.float32
BF16 = jnp.bfloat16

LN_EPS = 1e-5
GN_EPS = 64e-5
RWKV_HEAD = 64
RANK_W = 64
RANK_A = 64
RANK_G = 128
S5_GROUP = 16
S5_STATE = 64
S5_CHUNK = 16
RWKV_CHUNK = 64
XATTN_HEADS = 4
LANES = 128
VMEM_LIMIT = 56 * 1024 * 1024

NT = (((1,), (1,)), ((), ()))
TN = (((0,), (0,)), ((), ()))


def _mm(a, b, dims=None):
    a = a.astype(BF16)
    b = b.astype(BF16)
    if dims is None:
        return jnp.dot(a, b, preferred_element_type=F32)
    return lax.dot_general(a, b, dims, preferred_element_type=F32)


def _mm_f32(a, b, dims=None):
    if dims is None:
        return jnp.dot(a, b, preferred_element_type=F32, precision=lax.Precision.HIGHEST)
    return lax.dot_general(a, b, dims, preferred_element_type=F32, precision=lax.Precision.HIGHEST)


def _ln(z, g, b):
    mu = jnp.mean(z, axis=-1, keepdims=True)
    d = z - mu
    var = jnp.mean(d * d, axis=-1, keepdims=True)
    return d * lax.rsqrt(var + LN_EPS) * g + b


def _sigmoid(x):
    return 1.0 / (1.0 + jnp.exp(-x))


def _params(*sem):
    return pltpu.CompilerParams(dimension_semantics=sem, vmem_limit_bytes=VMEM_LIMIT)


def _token_tile(n):
    for t in (512, 256, 128, 64, 32, 16, 8):
        if n % t == 0:
            return t
    raise ValueError(f"token count {n} is not a multiple of 8")


def _ffn_ln_kernel(x_ref, wg_ref, wu_ref, wd_ref, g_ref, b_ref, o_ref, xb_ref, acc_ref, *, alpha):
    j = pl.program_id(1)

    @pl.when(j == 0)
    def _():
        xb_ref[...] = x_ref[...].astype(BF16)
        acc_ref[...] = jnp.zeros_like(acc_ref)

    xb = xb_ref[...]
    gate = jnp.dot(xb, wg_ref[...], preferred_element_type=F32)
    up = jnp.dot(xb, wu_ref[...], preferred_element_type=F32)
    h = gate * _sigmoid(gate) * up
    acc_ref[...] += jnp.dot(h.astype(BF16), wd_ref[...], preferred_element_type=F32)

    @pl.when(j == pl.num_programs(1) - 1)
    def _():
        o_ref[...] = _ln(alpha * x_ref[...] + 0.5 * acc_ref[...], g_ref[...], b_ref[...])


def _ffn_ln(x, wg, wu, wd, g, b, alpha):
    n, d = x.shape
    dff = wg.shape[1]
    tm = _token_tile(n)
    tf = 512 if dff % 512 == 0 else dff
    return pl.pallas_call(
        functools.partial(_ffn_ln_kernel, alpha=alpha),
        grid=(n // tm, dff // tf),
        in_specs=[
            pl.BlockSpec((tm, d), lambda i, j: (i, 0)),
            pl.BlockSpec((d, tf), lambda i, j: (0, j)),
            pl.BlockSpec((d, tf), lambda i, j: (0, j)),
            pl.BlockSpec((tf, d), lambda i, j: (j, 0)),
            pl.BlockSpec((1, d), lambda i, j: (0, 0)),
            pl.BlockSpec((1, d), lambda i, j: (0, 0)),
        ],
        out_specs=pl.BlockSpec((tm, d), lambda i, j: (i, 0)),
        out_shape=jax.ShapeDtypeStruct((n, d), F32),
        scratch_shapes=[pltpu.VMEM((tm, d), BF16), pltpu.VMEM((tm, d), F32)],
        compiler_params=_params("parallel", "arbitrary"),
        name="ffn_ln",
    )(x, wg, wu, wd, g, b)


def _proj_kernel(x_ref, w_ref, o_ref, xb_ref):
    @pl.when(pl.program_id(1) == 0)
    def _():
        xb_ref[...] = x_ref[...].astype(BF16)

    o_ref[...] = jnp.dot(xb_ref[...], w_ref[...], preferred_element_type=F32)


def _proj(x, w):
    n, d = x.shape
    nout = w.shape[1]
    tm = _token_tile(n)
    tn = 256 if nout % 256 == 0 else nout
    return pl.pallas_call(
        _proj_kernel,
        grid=(n // tm, nout // tn),
        in_specs=[
            pl.BlockSpec((tm, d), lambda i, j: (i, 0)),
            pl.BlockSpec((d, tn), lambda i, j: (0, j)),
        ],
        out_specs=pl.BlockSpec((tm, tn), lambda i, j: (i, j)),
        out_shape=jax.ShapeDtypeStruct((n, nout), F32),
        scratch_shapes=[pltpu.VMEM((tm, d), BF16)],
        compiler_params=_params("parallel", "arbitrary"),
        name="proj",
    )(x, w)


def _unit_lower_inverse(m, t):
    row = lax.broadcasted_iota(jnp.int32, (t, t), 0)
    col = lax.broadcasted_iota(jnp.int32, (t, t), 1)
    inv = jnp.where(row == col, 1.0, 0.0).astype(F32)
    k = 1
    while k < t:
        shift = (2 * k).bit_length() - 1
        same = (row >> shift) == (col >> shift)
        join = same & ((row & k) != 0) & ((col & k) == 0)
        c = jnp.where(join, m, 0.0)
        if k == 1:
            inv = inv - c
        else:
            inv = inv - _mm(inv, _mm(c, inv))
        k *= 2
    return inv


def _rwkv_kernel(p_ref, shift_ref, s0_ref, mu_ref, w0_ref, wup_ref, a0_ref, aup_ref, gup_ref,
                 kk_ref, ka_ref, rk_ref, gnw_ref, gnb_ref, seg_ref, segt_ref,
                 o_ref, sout_ref, prev_ref, state_ref, *, t, heads):
    c = pl.program_id(1)
    r_w = heads * RWKV_HEAD

    @pl.when(c == 0)
    def _():
        prev_ref[...] = shift_ref[0]
        state_ref[...] = s0_ref[0]

    p = p_ref[0]
    row = lax.broadcasted_iota(jnp.int32, p.shape, 0)
    prev = jnp.where(row == 0, prev_ref[...], pltpu.roll(p, 1, axis=0))
    prev_ref[...] = p[t - 1:t, :]
    ps = p + mu_ref[...] * (prev - p)

    r = ps[:, 0:r_w]
    k = ps[:, r_w:2 * r_w]
    v = ps[:, 2 * r_w:3 * r_w]
    wa_in = ps[:, 3 * r_w:3 * r_w + RANK_W + RANK_A]
    lg = ps[:, 3 * r_w + RANK_W + RANK_A:]

    log_w = -math.exp(-0.5) * _sigmoid(w0_ref[...] + _mm(jnp.tanh(wa_in), wup_ref[...]))
    a = _sigmoid(a0_ref[...] + _mm(wa_in, aup_ref[...]))
    g = _mm(_sigmoid(lg), gup_ref[...])

    seg = seg_ref[...]
    segt = segt_ref[...]

    def head_sum(x):
        return _mm_f32(_mm_f32(x, seg), segt)

    kk = k * kk_ref[...]
    kk = kk * lax.rsqrt(jnp.maximum(head_sum(kk * kk), 1e-24))
    k2 = k * (1.0 + (a - 1.0) * ka_ref[...])
    b = kk * a

    tri_r = lax.broadcasted_iota(jnp.int32, (t, t), 0)
    tri_c = lax.broadcasted_iota(jnp.int32, (t, t), 1)
    incl = tri_r >= tri_c
    strict = tri_r > tri_c
    cum = _mm_f32(jnp.where(incl, 1.0, 0.0).astype(F32), log_w)
    cum_last = cum[t - 1:t, :]
    e_cum = jnp.exp(cum)
    e_neg = jnp.exp(-cum)
    e_tail = jnp.exp(cum_last - cum)
    kq = kk * jnp.exp(cum - log_w)
    rq = r * e_cum
    bd = b * e_neg
    kd = k2 * e_neg
    bt = b * e_tail
    kt = k2 * e_tail
    p_last = e_cum[t - 1:t, :]

    ys = []
    for h in range(heads):
        sl = slice(h * RWKV_HEAD, (h + 1) * RWKV_HEAD)
        lhs = jnp.concatenate([kq[:, sl], rq[:, sl]], axis=0)
        rhs = jnp.concatenate([bd[:, sl], kd[:, sl]], axis=0)
        m = _mm(lhs, rhs, NT)
        s_h = state_ref[h]
        qs = _mm(lhs, s_h, NT)
        v_h = v[:, sl]
        m_b = jnp.where(strict, m[:t, :t], 0.0)
        m_k = jnp.where(strict, m[:t, t:], 0.0)
        l_b = jnp.where(incl, m[t:, :t], 0.0)
        l_k = jnp.where(incl, m[t:, t:], 0.0)
        u = _mm(_unit_lower_inverse(m_b, t), -(qs[:t] + _mm(m_k, v_h)))
        uv = jnp.concatenate([u, v_h], axis=0)
        ys.append(qs[t:] + _mm(jnp.concatenate([l_b, l_k], axis=1), uv))
        tail = jnp.concatenate([bt[:, sl], kt[:, sl]], axis=0)
        state_ref[h] = s_h * p_last[:, sl] + _mm(uv, tail, TN)
    y = jnp.concatenate(ys, axis=1)

    inv_hd = 1.0 / RWKV_HEAD
    mu_y = head_sum(y) * inv_hd
    dy = y - mu_y
    var = head_sum(dy * dy) * inv_hd
    yn = dy * lax.rsqrt(var + GN_EPS) * gnw_ref[...] + gnb_ref[...]
    bonus = head_sum(r * k2 * rk_ref[...]) * v
    o_ref[0] = (yn + bonus) * g

    @pl.when(c == pl.num_programs(1) - 1)
    def _():
        sout_ref[0] = state_ref[...]


def _rwkv(p3, shift_prev, s0, lw):
    bsz, seq, _ = p3.shape
    heads = s0.shape[1]
    r_w = heads * RWKV_HEAD
    cols = 3 * r_w + RANK_W + RANK_A + RANK_G
    t = min(RWKV_CHUNK, seq)
    assert seq % t == 0 and t & (t - 1) == 0
    row = lambda x: x.reshape(1, -1).astype(F32)
    zeros = jnp.zeros((RANK_W, r_w), F32)
    wup = jnp.concatenate([lw['w_up'], zeros], axis=0).astype(BF16)
    aup = jnp.concatenate([zeros, lw['a_up']], axis=0).astype(BF16)
    seg = (jnp.arange(r_w)[:, None] // RWKV_HEAD == jnp.arange(heads)[None, :]).astype(F32)
    vec = lambda n: pl.BlockSpec((1, n), lambda b, c: (0, 0))
    full = lambda a, bb: pl.BlockSpec((a, bb), lambda b, c: (0, 0))
    return pl.pallas_call(
        functools.partial(_rwkv_kernel, t=t, heads=heads),
        grid=(bsz, seq // t),
        in_specs=[
            pl.BlockSpec((1, t, cols), lambda b, c: (b, c, 0)),
            pl.BlockSpec((1, 1, cols), lambda b, c: (b, 0, 0)),
            pl.BlockSpec((1, heads, RWKV_HEAD, RWKV_HEAD), lambda b, c: (b, 0, 0, 0)),
            vec(cols), vec(r_w), full(RANK_W + RANK_A, r_w), vec(r_w), full(RANK_W + RANK_A, r_w),
            full(RANK_G, r_w), vec(r_w), vec(r_w), vec(r_w), vec(r_w), vec(r_w),
            full(r_w, heads), full(heads, r_w),
        ],
        out_specs=[
            pl.BlockSpec((1, t, r_w), lambda b, c: (b, c, 0)),
            pl.BlockSpec((1, heads, RWKV_HEAD, RWKV_HEAD), lambda b, c: (b, 0, 0, 0)),
        ],
        out_shape=[
            jax.ShapeDtypeStruct((bsz, seq, r_w), F32),
            jax.ShapeDtypeStruct((bsz, heads, RWKV_HEAD, RWKV_HEAD), F32),
        ],
        scratch_shapes=[pltpu.VMEM((1, cols), F32), pltpu.VMEM((heads, RWKV_HEAD, RWKV_HEAD), F32)],
        compiler_params=_params("parallel", "arbitrary"),
        name="rwkv",
    )(p3, shift_prev.astype(F32), s0.astype(F32), row(lw['shift_mu']), row(lw['w0']), wup, row(lw['a0']), aup,
      lw['g_up'].astype(BF16), row(lw['k_k']), row(lw['k_a']), row(lw['r_k']), row(lw['gn_w']),
      row(lw['gn_b']), seg, seg.T)


def _s5_prep_kernel(are_ref, aim_ref, ldt_ref, btre_ref, btim_ref, cre_ref, cim_ref, d_ref,
                    kst_ref, wre_ref, wim_ref, vre_ref, vimn_ref, lre_ref, lim_ref, *, tc):
    a_re = are_ref[0]
    a_im = aim_ref[0]
    dt = jnp.exp(ldt_ref[0])
    mag = jnp.exp(a_re * dt)
    l_re = mag * jnp.cos(a_im * dt)
    l_im = mag * jnp.sin(a_im * dt)
    den = a_re * a_re + a_im * a_im
    x_re = l_re - 1.0
    co_re = (x_re * a_re + l_im * a_im) / den
    co_im = (l_im * a_re - x_re * a_im) / den
    pw = [(jnp.ones_like(l_re), jnp.zeros_like(l_re))]
    for _ in range(tc):
        q_re, q_im = pw[-1]
        pw.append((q_re * l_re - q_im * l_im, q_re * l_im + q_im * l_re))

    c_re = cre_ref[0]
    c_im = cim_ref[0]
    bt_re = btre_ref[0]
    bt_im = btim_ref[0]
    cc_re = c_re * co_re - c_im * co_im
    cc_im = c_re * co_im + c_im * co_re
    cl_re = jnp.concatenate([cc_re * q[0] - cc_im * q[1] for q in pw[:tc]], axis=0)
    cl_im = jnp.concatenate([cc_re * q[1] + cc_im * q[0] for q in pw[:tc]], axis=0)
    kst = _mm_f32(cl_re, bt_re, NT) - _mm_f32(cl_im, bt_im, NT)
    rr = lax.broadcasted_iota(jnp.int32, kst.shape, 0)
    cc = lax.broadcasted_iota(jnp.int32, kst.shape, 1)
    kst_ref[0] = kst + jnp.where(rr == cc, d_ref[0], 0.0)

    w_re, w_im = [], []
    for s in range(tc):
        q_re, q_im = pw[tc - 1 - s]
        f_re = q_re * co_re - q_im * co_im
        f_im = q_re * co_im + q_im * co_re
        w_re.append(bt_re * f_re - bt_im * f_im)
        w_im.append(bt_re * f_im + bt_im * f_re)
    wre_ref[0] = jnp.concatenate(w_re, axis=0)
    wim_ref[0] = jnp.concatenate(w_im, axis=0)
    vre_ref[0] = jnp.concatenate([c_re * q[0] - c_im * q[1] for q in pw[1:]], axis=0)
    vimn_ref[0] = jnp.concatenate([-(c_re * q[1] + c_im * q[0]) for q in pw[1:]], axis=0)
    lre_ref[0] = pw[tc][0]
    lim_ref[0] = pw[tc][1]


def _s5_prep(lw):
    groups, state = lw['a_re'].shape
    ch = S5_GROUP
    tc = S5_CHUNK
    g3 = lambda x, a, b: x.astype(F32).reshape(groups, a, b)
    spec = lambda a, b: pl.BlockSpec((1, a, b), lambda g: (g, 0, 0))
    outs = pl.pallas_call(
        functools.partial(_s5_prep_kernel, tc=tc),
        grid=(groups,),
        in_specs=[spec(1, state), spec(1, state), spec(1, 1), spec(ch, state), spec(ch, state),
                  spec(ch, state), spec(ch, state), spec(1, ch)],
        out_specs=[spec(tc * ch, ch), spec(tc * ch, state), spec(tc * ch, state), spec(tc * ch, state),
                   spec(tc * ch, state), spec(1, state), spec(1, state)],
        out_shape=[jax.ShapeDtypeStruct((groups, tc * ch, ch), F32)]
        + [jax.ShapeDtypeStruct((groups, tc * ch, state), F32)] * 4
        + [jax.ShapeDtypeStruct((groups, 1, state), F32)] * 2,
        compiler_params=_params("parallel"),
        name="s5_prep",
    )(g3(lw['a_re'], 1, state), g3(lw['a_im'], 1, state), g3(lw['log_dt'], 1, 1),
      jnp.swapaxes(lw['b_re'], 1, 2).astype(F32), jnp.swapaxes(lw['b_im'], 1, 2).astype(F32),
      lw['c_re'].astype(F32), lw['c_im'].astype(F32), g3(lw['d'], 1, ch))
    kst, w_re, w_im, v_re, v_imn, l_re, l_im = outs
    k4 = kst.reshape(groups, tc, ch, ch)
    tt = jnp.arange(tc)
    tau = tt[:, None] - tt[None, :]
    blocks = jnp.where((tau >= 0)[None, :, :, None, None], k4[:, jnp.clip(tau, 0, tc - 1)], 0.0)
    toep = jnp.transpose(blocks, (0, 1, 3, 2, 4)).reshape(groups, tc * ch, tc * ch)
    return dict(
        toep=toep.astype(BF16),
        w=jnp.concatenate([w_re, w_im, w_im, w_re], axis=-1).astype(BF16),
        vt=jnp.concatenate([v_re, v_imn], axis=-1).astype(BF16),
        a16=jnp.concatenate([l_re, l_re], axis=-1),
        b16=jnp.concatenate([-l_im, l_im], axis=-1),
    )


def _s5_kernel(u_ref, toep_ref, w_ref, vt_ref, a_ref, b_ref, h0_ref, h0s_ref,
               y_ref, hout_ref, e_ref, hs_ref, *, bsz, n_chunks):
    u = u_ref[0]
    e_ref[...] = _mm(u, w_ref[0])
    a = a_ref[0]
    b = b_ref[0]
    width = a.shape[-1]
    steps = min(8, n_chunks)
    rows = steps * bsz

    def body(i, carry):
        x, s = carry
        start = pl.multiple_of(i * rows, rows)
        blk = e_ref[pl.ds(start, rows), :]
        starts = []
        for j in range(steps):
            starts.append(x)
            e = blk[j * bsz:(j + 1) * bsz]
            x, s = a * x + b * s + e[:, :width], a * s - b * x + e[:, width:]
        hs_ref[pl.ds(start, rows), :] = jnp.concatenate(starts, axis=0)
        return x, s

    x, _ = lax.fori_loop(0, n_chunks // steps, body, (h0_ref[0], h0s_ref[0]))
    hout_ref[0] = x
    y_ref[0] = _mm(u, toep_ref[0], NT) + _mm(hs_ref[...], vt_ref[0], NT)


def _s5(u, h_re, h_im, ops):
    bsz, seq, width = u.shape
    groups, state = h_re.shape[1:]
    ch, tc = S5_GROUP, S5_CHUNK
    assert seq % tc == 0 and width == groups * ch
    n_chunks = seq // tc
    assert n_chunks % min(8, n_chunks) == 0
    rows = n_chunks * bsz
    ug = u.reshape(bsz, n_chunks, tc, groups, ch).transpose(3, 1, 0, 2, 4).reshape(groups, rows, tc * ch)
    h0 = jnp.concatenate([h_re, h_im], axis=-1).astype(F32).transpose(1, 0, 2)
    h0s = jnp.concatenate([h_im, h_re], axis=-1).astype(F32).transpose(1, 0, 2)
    spec = lambda a, b: pl.BlockSpec((1, a, b), lambda g: (g, 0, 0))
    y, h_out = pl.pallas_call(
        functools.partial(_s5_kernel, bsz=bsz, n_chunks=n_chunks),
        grid=(groups,),
        in_specs=[spec(rows, tc * ch), spec(tc * ch, tc * ch), spec(tc * ch, 4 * state), spec(tc * ch, 2 * state),
                  spec(1, 2 * state), spec(1, 2 * state), spec(bsz, 2 * state), spec(bsz, 2 * state)],
        out_specs=[spec(rows, tc * ch), spec(bsz, 2 * state)],
        out_shape=[jax.ShapeDtypeStruct((groups, rows, tc * ch), F32),
                   jax.ShapeDtypeStruct((groups, bsz, 2 * state), F32)],
        scratch_shapes=[pltpu.VMEM((rows, 4 * state), F32), pltpu.VMEM((rows, 2 * state), F32)],
        compiler_params=_params("parallel"),
        name="s5",
    )(ug.astype(BF16), ops['toep'], ops['w'], ops['vt'], ops['a16'], ops['b16'], h0, h0s)
    y = y.reshape(groups, n_chunks, bsz, tc, ch).transpose(2, 1, 3, 0, 4).reshape(bsz, seq, width)
    h_out = h_out.transpose(1, 0, 2)
    return y, h_out[..., :state], h_out[..., state:]


def _gelu_tanh(x):
    return 0.5 * x * (1.0 + jnp.tanh(math.sqrt(2.0 / math.pi) * (x + 0.044715 * (x * x * x))))


def _mix_kernel(x_ref, rw_ref, y_ref, gw_ref, gb_ref, wo1_ref, wo2_ref, g_ref, b_ref, o_ref, *, alpha):
    y = _gelu_tanh(y_ref[...])
    s5o = y * _sigmoid(_mm(y, gw_ref[...]) + gb_ref[...])
    mix = _mm(rw_ref[...], wo1_ref[...]) + _mm(s5o, wo2_ref[...])
    o_ref[...] = _ln(alpha * x_ref[...] + mix, g_ref[...], b_ref[...])


def _mix(x, rw, y, glu_w, glu_b, w_out, g, b, alpha):
    n, d = x.shape
    r_w = rw.shape[1]
    s_w = y.shape[1]
    tm = min(_token_tile(n), 256)
    tile = lambda w: pl.BlockSpec((tm, w), lambda i: (i, 0))
    full = lambda a, bb: pl.BlockSpec((a, bb), lambda i: (0, 0))
    return pl.pallas_call(
        functools.partial(_mix_kernel, alpha=alpha),
        grid=(n // tm,),
        in_specs=[tile(d), tile(r_w), tile(s_w), full(s_w, s_w), full(1, s_w), full(r_w, d), full(s_w, d),
                  full(1, d), full(1, d)],
        out_specs=tile(d),
        out_shape=jax.ShapeDtypeStruct((n, d), F32),
        compiler_params=_params("parallel"),
        name="mix",
    )(x, rw, y, glu_w.astype(BF16), glu_b.reshape(1, -1).astype(F32), w_out[:r_w].astype(BF16),
      w_out[r_w:].astype(BF16), g, b)


def _attn_kernel(x_ref, mk_ref, mv_ref, wq_ref, wo_ref, g_ref, b_ref, o_ref, *, alpha, heads):
    x = x_ref[...]
    d = x.shape[-1]
    hd = d // heads
    q = _mm(x, wq_ref[...]) * (hd ** -0.5)
    mk = mk_ref[0]
    mv = mv_ref[0]
    outs = []
    for h in range(heads):
        sl = slice(h * hd, (h + 1) * hd)
        s = _mm(q[:, sl], mk[:, sl], NT)
        e = jnp.exp(s - jnp.max(s, axis=-1, keepdims=True))
        pr = e / jnp.sum(e, axis=-1, keepdims=True)
        outs.append(_mm(pr, mv[:, sl]))
    o = jnp.concatenate(outs, axis=-1)
    o_ref[...] = _ln(alpha * x + _mm(o, wo_ref[...]), g_ref[...], b_ref[...])


def _attn(x, mk, mv, wq, wo, g, b, alpha, seq):
    n, d = x.shape
    n_mem = mk.shape[1]
    tm = min(_token_tile(n), _token_tile(seq), 256)
    per_batch = seq // tm
    tile = pl.BlockSpec((tm, d), lambda i: (i, 0))
    mem = pl.BlockSpec((1, n_mem, d), lambda i: (i // per_batch, 0, 0))
    full = lambda a, bb: pl.BlockSpec((a, bb), lambda i: (0, 0))
    return pl.pallas_call(
        functools.partial(_attn_kernel, alpha=alpha, heads=XATTN_HEADS),
        grid=(n // tm,),
        in_specs=[tile, mem, mem, full(d, d), full(d, d), full(1, d), full(1, d)],
        out_specs=tile,
        out_shape=jax.ShapeDtypeStruct((n, d), F32),
        compiler_params=_params("parallel"),
        name="attn",
    )(x, mk.astype(BF16), mv.astype(BF16), wq, wo, g, b)


def _layer(x, mk, mv, shift_prev, s_rwkv, h_re, h_im, lw, s5_ops, alpha):
    bsz, seq, d = x.shape
    n = bsz * seq
    ln = lambda i: (lw['ln_g'][i].reshape(1, d).astype(F32), lw['ln_b'][i].reshape(1, d).astype(F32))
    cols = shift_prev.shape[-1]
    x1 = _ffn_ln(x.reshape(n, d), lw['f1g'], lw['f1u'], lw['f1d'], *ln(0), alpha)
    p = _proj(x1, lw['w_in']).reshape(bsz, seq, -1)
    rw, s_new = _rwkv(p, shift_prev, s_rwkv, lw)
    y, hr, hi = _s5(p[..., cols:], h_re, h_im, s5_ops)
    x2 = _mix(x1, rw.reshape(n, -1), y.reshape(n, -1), lw['glu_w'], lw['glu_b'], lw['w_out'], *ln(1), alpha)
    x3 = _attn(x2, mk.reshape(bsz, -1, d), mv.reshape(bsz, -1, d), lw['xq'], lw['xo'], *ln(2), alpha, seq)
    x4 = _ffn_ln(x3, lw['f2g'], lw['f2u'], lw['f2d'], *ln(3), alpha)
    return x4.reshape(bsz, seq, d), p[:, -1:, :cols], s_new, hr, hi


def kernel(x_prompt, x_sample, mem_prompt, cache_mem_k, cache_mem_v, state_rwkv, cache_shift, state_s5_re, state_s5_im, ln_g, ln_b, ffn1_gate, ffn1_up, ffn1_down, w_in, shift_mu, rwkv_w0, rwkv_w_up, rwkv_a0, rwkv_a_up, rwkv_g_up, rwkv_k_k, rwkv_k_a, rwkv_r_k, rwkv_gn_w, rwkv_gn_b, s5_a_re, s5_a_im, s5_log_dt, s5_b_re, s5_b_im, s5_c_re, s5_c_im, s5_d, s5_glu_w, s5_glu_b, w_mix_out, xattn_q, xattn_k, xattn_v, xattn_o, ffn2_gate, ffn2_up, ffn2_down):
    depth = ln_g.shape[0]
    bp, _, d = x_prompt.shape
    n_mem = mem_prompt.shape[1]
    heads, hd = state_rwkv.shape[2], state_rwkv.shape[3]
    cols = cache_shift.shape[-1]
    groups, state = state_s5_re.shape[2:]
    alpha = (2.0 * depth) ** 0.25
    bf = lambda w: w.astype(BF16)
    xp, xs = x_prompt, x_sample
    outs = [[] for _ in range(10)]
    for l in range(depth):
        lw = dict(ln_g=ln_g[l], ln_b=ln_b[l], f1g=bf(ffn1_gate[l]), f1u=bf(ffn1_up[l]), f1d=bf(ffn1_down[l]),
                  w_in=bf(w_in[l]), shift_mu=shift_mu[l], w0=rwkv_w0[l], w_up=rwkv_w_up[l], a0=rwkv_a0[l],
                  a_up=rwkv_a_up[l], g_up=rwkv_g_up[l], k_k=rwkv_k_k[l], k_a=rwkv_k_a[l], r_k=rwkv_r_k[l],
                  gn_w=rwkv_gn_w[l], gn_b=rwkv_gn_b[l], a_re=s5_a_re[l], a_im=s5_a_im[l], log_dt=s5_log_dt[l],
                  b_re=s5_b_re[l], b_im=s5_b_im[l], c_re=s5_c_re[l], c_im=s5_c_im[l], d=s5_d[l],
                  glu_w=s5_glu_w[l], glu_b=s5_glu_b[l], w_out=w_mix_out[l], xq=bf(xattn_q[l]), xo=bf(xattn_o[l]),
                  f2g=bf(ffn2_gate[l]), f2u=bf(ffn2_up[l]), f2d=bf(ffn2_down[l]))
        s5_ops = _s5_prep(lw)
        mem2 = mem_prompt.reshape(bp * n_mem, d)
        mk_p = _proj(mem2, bf(xattn_k[l])).reshape(bp, n_mem, XATTN_HEADS, d // XATTN_HEADS)
        mv_p = _proj(mem2, bf(xattn_v[l])).reshape(bp, n_mem, XATTN_HEADS, d // XATTN_HEADS)
        xp, sh_p, rw_p, hr_p, hi_p = _layer(
            xp, mk_p, mv_p, jnp.zeros((bp, 1, cols), F32), jnp.zeros((bp, heads, hd, hd), F32),
            jnp.zeros((bp, groups, state), F32), jnp.zeros((bp, groups, state), F32), lw, s5_ops, alpha)
        xs, sh_s, rw_s, hr_s, hi_s = _layer(
            xs, cache_mem_k[l], cache_mem_v[l], cache_shift[l], state_rwkv[l],
            state_s5_re[l], state_s5_im[l], lw, s5_ops, alpha)
        for acc, val in zip(outs, (mk_p, mv_p, rw_p, sh_p, hr_p, hi_p, rw_s, sh_s, hr_s, hi_s)):
            acc.append(val)
    return (xp, xs) + tuple(jnp.stack(o) for o in outs)
```

```python
import functools
import math

import jax
import jax.numpy as jnp
from jax import lax
from jax.experimental import pallas as pl
from jax.experimental.pallas import tpu as pltpu

F32 = jnp.float32
BF16 = jnp.bfloat16

LN_EPS = 1e-5
GN_EPS = 64e-5
RWKV_HEAD = 64
RANK_W = 64
RANK_A = 64
RANK_G = 128
S5_GROUP = 16
S5_STATE = 64
S5_CHUNK = 16
RWKV_CHUNK = 64
XATTN_HEADS = 4
LANES = 128
VMEM_LIMIT = 56 * 1024 * 1024

NT = (((1,), (1,)), ((), ()))
TN = (((0,), (0,)), ((), ()))


def _mm(a, b, dims=None):
    a = a.astype(BF16)
    b = b.astype(BF16)
    if dims is None:
        return jnp.dot(a, b, preferred_element_type=F32)
    return lax.dot_general(a, b, dims, preferred_element_type=F32)


def _mm_f32(a, b, dims=None):
    if dims is None:
        return jnp.dot(a, b, preferred_element_type=F32, precision=lax.Precision.HIGHEST)
    return lax.dot_general(a, b, dims, preferred_element_type=F32, precision=lax.Precision.HIGHEST)


def _ln(z, g, b):
    mu = jnp.mean(z, axis=-1, keepdims=True)
    d = z - mu
    var = jnp.mean(d * d, axis=-1, keepdims=True)
    return d * lax.rsqrt(var + LN_EPS) * g + b


def _sigmoid(x):
    return 1.0 / (1.0 + jnp.exp(-x))


def _params(*sem):
    return pltpu.CompilerParams(dimension_semantics=sem, vmem_limit_bytes=VMEM_LIMIT)


def _token_tile(n):
    for t in (512, 256, 128, 64, 32, 16, 8):
        if n % t == 0:
            return t
    raise ValueError(f"token count {n} is not a multiple of 8")


def _ffn_ln_kernel(x_ref, wg_ref, wu_ref, wd_ref, g_ref, b_ref, o_ref, xb_ref, acc_ref, *, alpha):
    j = pl.program_id(1)

    @pl.when(j == 0)
    def _():
        xb_ref[...] = x_ref[...].astype(BF16)
        acc_ref[...] = jnp.zeros_like(acc_ref)

    xb = xb_ref[...]
    gate = jnp.dot(xb, wg_ref[...], preferred_element_type=F32)
    up = jnp.dot(xb, wu_ref[...], preferred_element_type=F32)
    h = gate * _sigmoid(gate) * up
    acc_ref[...] += jnp.dot(h.astype(BF16), wd_ref[...], preferred_element_type=F32)

    @pl.when(j == pl.num_programs(1) - 1)
    def _():
        o_ref[...] = _ln(alpha * x_ref[...] + 0.5 * acc_ref[...], g_ref[...], b_ref[...])


def _ffn_ln(x, wg, wu, wd, g, b, alpha):
    n, d = x.shape
    dff = wg.shape[1]
    tm = _token_tile(n)
    tf = 512 if dff % 512 == 0 else dff
    return pl.pallas_call(
        functools.partial(_ffn_ln_kernel, alpha=alpha),
        grid=(n // tm, dff // tf),
        in_specs=[
            pl.BlockSpec((tm, d), lambda i, j: (i, 0)),
            pl.BlockSpec((d, tf), lambda i, j: (0, j)),
            pl.BlockSpec((d, tf), lambda i, j: (0, j)),
            pl.BlockSpec((tf, d), lambda i, j: (j, 0)),
            pl.BlockSpec((1, d), lambda i, j: (0, 0)),
            pl.BlockSpec((1, d), lambda i, j: (0, 0)),
        ],
        out_specs=pl.BlockSpec((tm, d), lambda i, j: (i, 0)),
        out_shape=jax.ShapeDtypeStruct((n, d), F32),
        scratch_shapes=[pltpu.VMEM((tm, d), BF16), pltpu.VMEM((tm, d), F32)],
        compiler_params=_params("parallel", "arbitrary"),
        name="ffn_ln",
    )(x, wg, wu, wd, g, b)


def _proj_kernel(x_ref, w_ref, o_ref):
    o_ref[...] = _mm(x_ref[...], w_ref[...])


def _column_tile(nout, cap=2304):
    best = None
    for t in range(LANES, min(nout, cap) + 1, LANES):
        if nout % t == 0:
            best = t
    return best if best is not None else nout


def _proj(x, w):
    n, d = x.shape
    nout = w.shape[1]
    tm = _token_tile(n)
    tn = _column_tile(nout)
    return pl.pallas_call(
        _proj_kernel,
        grid=(nout // tn, n // tm),
        in_specs=[
            pl.BlockSpec((tm, d), lambda j, i: (i, 0)),
            pl.BlockSpec((d, tn), lambda j, i: (0, j)),
        ],
        out_specs=pl.BlockSpec((tm, tn), lambda j, i: (i, j)),
        out_shape=jax.ShapeDtypeStruct((n, nout), F32),
        compiler_params=_params("parallel", "parallel"),
        name="proj",
    )(x, w)


def _unit_lower_inverses(ms, t):
    row = lax.broadcasted_iota(jnp.int32, (t, t), 0)
    col = lax.broadcasted_iota(jnp.int32, (t, t), 1)
    eye = jnp.where(row == col, 1.0, 0.0).astype(F32)
    invs = None
    k = 1
    while k < t:
        shift = (2 * k).bit_length() - 1
        join = ((row >> shift) == (col >> shift)) & ((row & k) != 0) & ((col & k) == 0)
        cs = [jnp.where(join, m, 0.0) for m in ms]
        if k == 1:
            invs = [eye - c for c in cs]
        else:
            xs = [_mm(c, inv) for c, inv in zip(cs, invs)]
            invs = [inv - _mm(inv, x) for inv, x in zip(invs, xs)]
        k *= 2
    return invs


def _rwkv_kernel(p_ref, shift_ref, s0_ref, mu_ref, w0_ref, wup_ref, a0_ref, aup_ref, gup_ref,
                 kk_ref, ka_ref, rk_ref, gnw_ref, gnb_ref, seg_ref, segt_ref,
                 o_ref, sout_ref, prev_ref, state_ref, *, t, heads):
    c = pl.program_id(1)
    r_w = heads * RWKV_HEAD

    @pl.when(c == 0)
    def _():
        prev_ref[...] = shift_ref[0]
        state_ref[...] = s0_ref[0]

    p = p_ref[0]
    row = lax.broadcasted_iota(jnp.int32, p.shape, 0)
    prev = jnp.where(row == 0, prev_ref[...], pltpu.roll(p, 1, axis=0))
    prev_ref[...] = p[t - 1:t, :]
    ps = p + mu_ref[...] * (prev - p)

    r = ps[:, 0:r_w]
    k = ps[:, r_w:2 * r_w]
    v = ps[:, 2 * r_w:3 * r_w]
    wa_in = ps[:, 3 * r_w:3 * r_w + RANK_W + RANK_A]
    lg = ps[:, 3 * r_w + RANK_W + RANK_A:]

    log_w = -math.exp(-0.5) * _sigmoid(w0_ref[...] + _mm(jnp.tanh(wa_in), wup_ref[...]))
    a = _sigmoid(a0_ref[...] + _mm(wa_in, aup_ref[...]))
    g = _mm(_sigmoid(lg), gup_ref[...])

    seg = seg_ref[...]
    segt = segt_ref[...]

    def head_sum(x):
        return _mm_f32(_mm_f32(x, seg), segt)

    kk = k * kk_ref[...]
    kk = kk * lax.rsqrt(jnp.maximum(head_sum(kk * kk), 1e-24))
    k2 = k * (1.0 + (a - 1.0) * ka_ref[...])
    b = kk * a

    tri_r = lax.broadcasted_iota(jnp.int32, (t, t), 0)
    tri_c = lax.broadcasted_iota(jnp.int32, (t, t), 1)
    incl = tri_r >= tri_c
    strict = tri_r > tri_c
    cum = _mm_f32(jnp.where(incl, 1.0, 0.0).astype(F32), log_w)
    cum_last = cum[t - 1:t, :]
    e_cum = jnp.exp(cum)
    e_neg = jnp.exp(-cum)
    e_tail = jnp.exp(cum_last - cum)
    kq = kk * jnp.exp(cum - log_w)
    rq = r * e_cum
    bd = b * e_neg
    kd = k2 * e_neg
    bt = b * e_tail
    kt = k2 * e_tail
    p_last = e_cum[t - 1:t, :]

    hs = range(heads)
    sls = [slice(h * RWKV_HEAD, (h + 1) * RWKV_HEAD) for h in hs]
    stack = lambda x, y: [jnp.concatenate([x[:, sl], y[:, sl]], axis=0).astype(BF16) for sl in sls]
    lhs = stack(kq, rq)
    rhs = stack(bd, kd)
    tail = stack(bt, kt)
    vs = [v[:, sl].astype(BF16) for sl in sls]
    s_in = [state_ref[h] for h in hs]
    ms = [_mm(lhs[h], rhs[h], NT) for h in hs]
    qs = [_mm(lhs[h], s_in[h], NT) for h in hs]
    mkv = [_mm(jnp.where(strict, ms[h][:t, t:], 0.0), vs[h]) for h in hs]
    invs = _unit_lower_inverses([jnp.where(strict, ms[h][:t, :t], 0.0) for h in hs], t)
    us = [_mm(invs[h], -(qs[h][:t] + mkv[h])) for h in hs]
    uvs = [jnp.concatenate([us[h].astype(BF16), vs[h]], axis=0) for h in hs]
    incl2 = (lax.broadcasted_iota(jnp.int32, (t, 2 * t), 0)
             >= (lax.broadcasted_iota(jnp.int32, (t, 2 * t), 1) & (t - 1)))
    ys = [qs[h][t:] + _mm(jnp.where(incl2, ms[h][t:, :], 0.0), uvs[h]) for h in hs]
    for h in hs:
        state_ref[h] = s_in[h] * p_last[:, sls[h]] + _mm(uvs[h], tail[h], TN)
    y = jnp.concatenate(ys, axis=1)

    inv_hd = 1.0 / RWKV_HEAD
    mu_y = head_sum(y) * inv_hd
    dy = y - mu_y
    var = head_sum(dy * dy) * inv_hd
    yn = dy * lax.rsqrt(var + GN_EPS) * gnw_ref[...] + gnb_ref[...]
    bonus = head_sum(r * k2 * rk_ref[...]) * v
    o_ref[0] = (yn + bonus) * g

    @pl.when(c == pl.num_programs(1) - 1)
    def _():
        sout_ref[0] = state_ref[...]


def _rwkv(p3, shift_prev, s0, lw):
    bsz, seq, _ = p3.shape
    heads = s0.shape[1]
    r_w = heads * RWKV_HEAD
    cols = 3 * r_w + RANK_W + RANK_A + RANK_G
    t = min(RWKV_CHUNK, seq)
    assert seq % t == 0 and t & (t - 1) == 0
    row = lambda x: x.reshape(1, -1).astype(F32)
    zeros = jnp.zeros((RANK_W, r_w), F32)
    wup = jnp.concatenate([lw['w_up'], zeros], axis=0).astype(BF16)
    aup = jnp.concatenate([zeros, lw['a_up']], axis=0).astype(BF16)
    seg = (jnp.arange(r_w)[:, None] // RWKV_HEAD == jnp.arange(heads)[None, :]).astype(F32)
    vec = lambda n: pl.BlockSpec((1, n), lambda b, c: (0, 0))
    full = lambda a, bb: pl.BlockSpec((a, bb), lambda b, c: (0, 0))
    return pl.pallas_call(
        functools.partial(_rwkv_kernel, t=t, heads=heads),
        grid=(bsz, seq // t),
        in_specs=[
            pl.BlockSpec((1, t, cols), lambda b, c: (b, c, 0)),
            pl.BlockSpec((1, 1, cols), lambda b, c: (b, 0, 0)),
            pl.BlockSpec((1, heads, RWKV_HEAD, RWKV_HEAD), lambda b, c: (b, 0, 0, 0)),
            vec(cols), vec(r_w), full(RANK_W + RANK_A, r_w), vec(r_w), full(RANK_W + RANK_A, r_w),
            full(RANK_G, r_w), vec(r_w), vec(r_w), vec(r_w), vec(r_w), vec(r_w),
            full(r_w, heads), full(heads, r_w),
        ],
        out_specs=[
            pl.BlockSpec((1, t, r_w), lambda b, c: (b, c, 0)),
            pl.BlockSpec((1, heads, RWKV_HEAD, RWKV_HEAD), lambda b, c: (b, 0, 0, 0)),
        ],
        out_shape=[
            jax.ShapeDtypeStruct((bsz, seq, r_w), F32),
            jax.ShapeDtypeStruct((bsz, heads, RWKV_HEAD, RWKV_HEAD), F32),
        ],
        scratch_shapes=[pltpu.VMEM((1, cols), F32), pltpu.VMEM((heads, RWKV_HEAD, RWKV_HEAD), F32)],
        compiler_params=_params("parallel", "arbitrary"),
        name="rwkv",
    )(p3, shift_prev.astype(F32), s0.astype(F32), row(lw['shift_mu']), row(lw['w0']), wup, row(lw['a0']), aup,
      lw['g_up'].astype(BF16), row(lw['k_k']), row(lw['k_a']), row(lw['r_k']), row(lw['gn_w']),
      row(lw['gn_b']), seg, seg.T)


def _s5_prep_kernel(are_ref, aim_ref, ldt_ref, btre_ref, btim_ref, cre_ref, cim_ref, d_ref,
                    kst_ref, wre_ref, wim_ref, vre_ref, vimn_ref, lre_ref, lim_ref, *, tc):
    a_re = are_ref[0]
    a_im = aim_ref[0]
    dt = jnp.exp(ldt_ref[0])
    mag = jnp.exp(a_re * dt)
    l_re = mag * jnp.cos(a_im * dt)
    l_im = mag * jnp.sin(a_im * dt)
    den = a_re * a_re + a_im * a_im
    x_re = l_re - 1.0
    co_re = (x_re * a_re + l_im * a_im) / den
    co_im = (l_im * a_re - x_re * a_im) / den
    pw = [(jnp.ones_like(l_re), jnp.zeros_like(l_re))]
    for _ in range(tc):
        q_re, q_im = pw[-1]
        pw.append((q_re * l_re - q_im * l_im, q_re * l_im + q_im * l_re))

    c_re = cre_ref[0]
    c_im = cim_ref[0]
    bt_re = btre_ref[0]
    bt_im = btim_ref[0]
    cc_re = c_re * co_re - c_im * co_im
    cc_im = c_re * co_im + c_im * co_re
    cl_re = jnp.concatenate([cc_re * q[0] - cc_im * q[1] for q in pw[:tc]], axis=0)
    cl_im = jnp.concatenate([cc_re * q[1] + cc_im * q[0] for q in pw[:tc]], axis=0)
    kst = _mm_f32(cl_re, bt_re, NT) - _mm_f32(cl_im, bt_im, NT)
    rr = lax.broadcasted_iota(jnp.int32, kst.shape, 0)
    cc = lax.broadcasted_iota(jnp.int32, kst.shape, 1)
    kst_ref[0] = kst + jnp.where(rr == cc, d_ref[0], 0.0)

    w_re, w_im = [], []
    for s in range(tc):
        q_re, q_im = pw[tc - 1 - s]
        f_re = q_re * co_re - q_im * co_im
        f_im = q_re * co_im + q_im * co_re
        w_re.append(bt_re * f_re - bt_im * f_im)
        w_im.append(bt_re * f_im + bt_im * f_re)
    wre_ref[0] = jnp.concatenate(w_re, axis=0)
    wim_ref[0] = jnp.concatenate(w_im, axis=0)
    vre_ref[0] = jnp.concatenate([c_re * q[0] - c_im * q[1] for q in pw[1:]], axis=0)
    vimn_ref[0] = jnp.concatenate([-(c_re * q[1] + c_im * q[0]) for q in pw[1:]], axis=0)
    lre_ref[0] = pw[tc][0]
    lim_ref[0] = pw[tc][1]


def _s5_prep(lw):
    groups, state = lw['a_re'].shape
    ch = S5_GROUP
    tc = S5_CHUNK
    g3 = lambda x, a, b: x.astype(F32).reshape(groups, a, b)
    spec = lambda a, b: pl.BlockSpec((1, a, b), lambda g: (g, 0, 0))
    outs = pl.pallas_call(
        functools.partial(_s5_prep_kernel, tc=tc),
        grid=(groups,),
        in_specs=[spec(1, state), spec(1, state), spec(1, 1), spec(ch, state), spec(ch, state),
                  spec(ch, state), spec(ch, state), spec(1, ch)],
        out_specs=[spec(tc * ch, ch), spec(tc * ch, state), spec(tc * ch, state), spec(tc * ch, state),
                   spec(tc * ch, state), spec(1, state), spec(1, state)],
        out_shape=[jax.ShapeDtypeStruct((groups, tc * ch, ch), F32)]
        + [jax.ShapeDtypeStruct((groups, tc * ch, state), F32)] * 4
        + [jax.ShapeDtypeStruct((groups, 1, state), F32)] * 2,
        compiler_params=_params("parallel"),
        name="s5_prep",
    )(g3(lw['a_re'], 1, state), g3(lw['a_im'], 1, state), g3(lw['log_dt'], 1, 1),
      jnp.swapaxes(lw['b_re'], 1, 2).astype(F32), jnp.swapaxes(lw['b_im'], 1, 2).astype(F32),
      lw['c_re'].astype(F32), lw['c_im'].astype(F32), g3(lw['d'], 1, ch))
    kst, w_re, w_im, v_re, v_imn, l_re, l_im = outs
    k4 = kst.reshape(groups, tc, ch, ch)
    tt = jnp.arange(tc)
    tau = tt[:, None] - tt[None, :]
    blocks = jnp.where((tau >= 0)[None, :, :, None, None], k4[:, jnp.clip(tau, 0, tc - 1)], 0.0)
    toep = jnp.transpose(blocks, (0, 1, 3, 2, 4)).reshape(groups, tc * ch, tc * ch)
    return dict(
        toep=toep.astype(BF16),
        w=jnp.concatenate([w_re, w_im, w_im, w_re], axis=-1).astype(BF16),
        vt=jnp.concatenate([v_re, v_imn], axis=-1).astype(BF16),
        a16=jnp.concatenate([l_re, l_re], axis=-1),
        b16=jnp.concatenate([-l_im, l_im], axis=-1),
    )


def _s5_kernel(u_ref, toep_ref, w_ref, vt_ref, a_ref, b_ref, h0_ref, h0s_ref,
               y_ref, hout_ref, e_ref, hs_ref, *, bsz, n_chunks):
    u = u_ref[0]
    e_ref[...] = _mm(u, w_ref[0])
    a = a_ref[0]
    b = b_ref[0]
    width = a.shape[-1]
    steps = min(8, n_chunks)
    rows = steps * bsz

    def body(i, carry):
        x, s = carry
        start = pl.multiple_of(i * rows, rows)
        blk = e_ref[pl.ds(start, rows), :]
        starts = []
        for j in range(steps):
            starts.append(x)
            e = blk[j * bsz:(j + 1) * bsz]
            x, s = a * x + b * s + e[:, :width], a * s - b * x + e[:, width:]
        hs_ref[pl.ds(start, rows), :] = jnp.concatenate(starts, axis=0)
        return x, s

    x, _ = lax.fori_loop(0, n_chunks // steps, body, (h0_ref[0], h0s_ref[0]))
    hout_ref[0] = x
    y_ref[0] = _mm(u, toep_ref[0], NT) + _mm(hs_ref[...], vt_ref[0], NT)


def _s5(u, h_re, h_im, ops):
    bsz, seq, width = u.shape
    groups, state = h_re.shape[1:]
    ch, tc = S5_GROUP, S5_CHUNK
    assert seq % tc == 0 and width == groups * ch
    n_chunks = seq // tc
    assert n_chunks % min(8, n_chunks) == 0
    rows = n_chunks * bsz
    ug = u.reshape(bsz, n_chunks, tc, groups, ch).transpose(3, 1, 0, 2, 4).reshape(groups, rows, tc * ch)
    h0 = jnp.concatenate([h_re, h_im], axis=-1).astype(F32).transpose(1, 0, 2)
    h0s = jnp.concatenate([h_im, h_re], axis=-1).astype(F32).transpose(1, 0, 2)
    spec = lambda a, b: pl.BlockSpec((1, a, b), lambda g: (g, 0, 0))
    y, h_out = pl.pallas_call(
        functools.partial(_s5_kernel, bsz=bsz, n_chunks=n_chunks),
        grid=(groups,),
        in_specs=[spec(rows, tc * ch), spec(tc * ch, tc * ch), spec(tc * ch, 4 * state), spec(tc * ch, 2 * state),
                  spec(1, 2 * state), spec(1, 2 * state), spec(bsz, 2 * state), spec(bsz, 2 * state)],
        out_specs=[spec(rows, tc * ch), spec(bsz, 2 * state)],
        out_shape=[jax.ShapeDtypeStruct((groups, rows, tc * ch), F32),
                   jax.ShapeDtypeStruct((groups, bsz, 2 * state), F32)],
        scratch_shapes=[pltpu.VMEM((rows, 4 * state), F32), pltpu.VMEM((rows, 2 * state), F32)],
        compiler_params=_params("parallel"),
        name="s5",
    )(ug.astype(BF16), ops['toep'], ops['w'], ops['vt'], ops['a16'], ops['b16'], h0, h0s)
    y = y.reshape(groups, n_chunks, bsz, tc, ch).transpose(2, 1, 3, 0, 4).reshape(bsz, seq, width)
    h_out = h_out.transpose(1, 0, 2)
    return y, h_out[..., :state], h_out[..., state:]


def _gelu_tanh(x):
    return 0.5 * x * (1.0 + jnp.tanh(math.sqrt(2.0 / math.pi) * (x + 0.044715 * (x * x * x))))


def _mix_kernel(x_ref, rw_ref, y_ref, gw_ref, gb_ref, wo1_ref, wo2_ref, g_ref, b_ref, o_ref, *, alpha):
    y = _gelu_tanh(y_ref[...])
    s5o = y * _sigmoid(_mm(y, gw_ref[...]) + gb_ref[...])
    mix = _mm(rw_ref[...], wo1_ref[...]) + _mm(s5o, wo2_ref[...])
    o_ref[...] = _ln(alpha * x_ref[...] + mix, g_ref[...], b_ref[...])


def _mix(x, rw, y, glu_w, glu_b, w_out, g, b, alpha):
    n, d = x.shape
    r_w = rw.shape[1]
    s_w = y.shape[1]
    tm = min(_token_tile(n), 256)
    tile = lambda w: pl.BlockSpec((tm, w), lambda i: (i, 0))
    full = lambda a, bb: pl.BlockSpec((a, bb), lambda i: (0, 0))
    return pl.pallas_call(
        functools.partial(_mix_kernel, alpha=alpha),
        grid=(n // tm,),
        in_specs=[tile(d), tile(r_w), tile(s_w), full(s_w, s_w), full(1, s_w), full(r_w, d), full(s_w, d),
                  full(1, d), full(1, d)],
        out_specs=tile(d),
        out_shape=jax.ShapeDtypeStruct((n, d), F32),
        compiler_params=_params("parallel"),
        name="mix",
    )(x, rw, y, glu_w.astype(BF16), glu_b.reshape(1, -1).astype(F32), w_out[:r_w].astype(BF16),
      w_out[r_w:].astype(BF16), g, b)


def _attn_kernel(x_ref, mk_ref, mv_ref, wq_ref, wo_ref, g_ref, b_ref, o_ref, *, alpha, heads):
    x = x_ref[...]
    d = x.shape[-1]
    hd = d // heads
    q = _mm(x, wq_ref[...]) * (hd ** -0.5)
    mk = mk_ref[0]
    mv = mv_ref[0]
    outs = []
    for h in range(heads):
        sl = slice(h * hd, (h + 1) * hd)
        s = _mm(q[:, sl], mk[:, sl], NT)
        e = jnp.exp(s - jnp.max(s, axis=-1, keepdims=True))
        pr = e / jnp.sum(e, axis=-1, keepdims=True)
        outs.append(_mm(pr, mv[:, sl]))
    o = jnp.concatenate(outs, axis=-1)
    o_ref[...] = _ln(alpha * x + _mm(o, wo_ref[...]), g_ref[...], b_ref[...])


def _attn(x, mk, mv, wq, wo, g, b, alpha, seq):
    n, d = x.shape
    n_mem = mk.shape[1]
    tm = min(_token_tile(n), _token_tile(seq), 256)
    per_batch = seq // tm
    tile = pl.BlockSpec((tm, d), lambda i: (i, 0))
    mem = pl.BlockSpec((1, n_mem, d), lambda i: (i // per_batch, 0, 0))
    full = lambda a, bb: pl.BlockSpec((a, bb), lambda i: (0, 0))
    return pl.pallas_call(
        functools.partial(_attn_kernel, alpha=alpha, heads=XATTN_HEADS),
        grid=(n // tm,),
        in_specs=[tile, mem, mem, full(d, d), full(d, d), full(1, d), full(1, d)],
        out_specs=tile,
        out_shape=jax.ShapeDtypeStruct((n, d), F32),
        compiler_params=_params("parallel"),
        name="attn",
    )(x, mk.astype(BF16), mv.astype(BF16), wq, wo, g, b)


def _layer(x, mk, mv, shift_prev, s_rwkv, h_re, h_im, lw, s5_ops, alpha):
    bsz, seq, d = x.shape
    n = bsz * seq
    ln = lambda i: (lw['ln_g'][i].reshape(1, d).astype(F32), lw['ln_b'][i].reshape(1, d).astype(F32))
    cols = shift_prev.shape[-1]
    x1 = _ffn_ln(x.reshape(n, d), lw['f1g'], lw['f1u'], lw['f1d'], *ln(0), alpha)
    p = _proj(x1, lw['w_in']).reshape(bsz, seq, -1)
    rw, s_new = _rwkv(p, shift_prev, s_rwkv, lw)
    y, hr, hi = _s5(p[..., cols:], h_re, h_im, s5_ops)
    x2 = _mix(x1, rw.reshape(n, -1), y.reshape(n, -1), lw['glu_w'], lw['glu_b'], lw['w_out'], *ln(1), alpha)
    x3 = _attn(x2, mk.reshape(bsz, -1, d), mv.reshape(bsz, -1, d), lw['xq'], lw['xo'], *ln(2), alpha, seq)
    x4 = _ffn_ln(x3, lw['f2g'], lw['f2u'], lw['f2d'], *ln(3), alpha)
    return x4.reshape(bsz, seq, d), p[:, -1:, :cols], s_new, hr, hi


def kernel(x_prompt, x_sample, mem_prompt, cache_mem_k, cache_mem_v, state_rwkv, cache_shift, state_s5_re, state_s5_im, ln_g, ln_b, ffn1_gate, ffn1_up, ffn1_down, w_in, shift_mu, rwkv_w0, rwkv_w_up, rwkv_a0, rwkv_a_up, rwkv_g_up, rwkv_k_k, rwkv_k_a, rwkv_r_k, rwkv_gn_w, rwkv_gn_b, s5_a_re, s5_a_im, s5_log_dt, s5_b_re, s5_b_im, s5_c_re, s5_c_im, s5_d, s5_glu_w, s5_glu_b, w_mix_out, xattn_q, xattn_k, xattn_v, xattn_o, ffn2_gate, ffn2_up, ffn2_down):
    depth = ln_g.shape[0]
    bp, _, d = x_prompt.shape
    n_mem = mem_prompt.shape[1]
    heads, hd = state_rwkv.shape[2], state_rwkv.shape[3]
    cols = cache_shift.shape[-1]
    groups, state = state_s5_re.shape[2:]
    alpha = (2.0 * depth) ** 0.25
    bf = lambda w: w.astype(BF16)
    xp, xs = x_prompt, x_sample
    outs = [[] for _ in range(10)]
    for l in range(depth):
        lw = dict(ln_g=ln_g[l], ln_b=ln_b[l], f1g=bf(ffn1_gate[l]), f1u=bf(ffn1_up[l]), f1d=bf(ffn1_down[l]),
                  w_in=bf(w_in[l]), shift_mu=shift_mu[l], w0=rwkv_w0[l], w_up=rwkv_w_up[l], a0=rwkv_a0[l],
                  a_up=rwkv_a_up[l], g_up=rwkv_g_up[l], k_k=rwkv_k_k[l], k_a=rwkv_k_a[l], r_k=rwkv_r_k[l],
                  gn_w=rwkv_gn_w[l], gn_b=rwkv_gn_b[l], a_re=s5_a_re[l], a_im=s5_a_im[l], log_dt=s5_log_dt[l],
                  b_re=s5_b_re[l], b_im=s5_b_im[l], c_re=s5_c_re[l], c_im=s5_c_im[l], d=s5_d[l],
                  glu_w=s5_glu_w[l], glu_b=s5_glu_b[l], w_out=w_mix_out[l], xq=bf(xattn_q[l]), xo=bf(xattn_o[l]),
                  f2g=bf(ffn2_gate[l]), f2u=bf(ffn2_up[l]), f2d=bf(ffn2_down[l]))
        s5_ops = _s5_prep(lw)
        mem2 = mem_prompt.reshape(bp * n_mem, d)
        mk_p = _proj(mem2, bf(xattn_k[l])).reshape(bp, n_mem, XATTN_HEADS, d // XATTN_HEADS)
        mv_p = _proj(mem2, bf(xattn_v[l])).reshape(bp, n_mem, XATTN_HEADS, d // XATTN_HEADS)
        xp, sh_p, rw_p, hr_p, hi_p = _layer(
            xp, mk_p, mv_p, jnp.zeros((bp, 1, cols), F32), jnp.zeros((bp, heads, hd, hd), F32),
            jnp.zeros((bp, groups, state), F32), jnp.zeros((bp, groups, state), F32), lw, s5_ops, alpha)
        xs, sh_s, rw_s, hr_s, hi_s = _layer(
            xs, cache_mem_k[l], cache_mem_v[l], cache_shift[l], state_rwkv[l],
            state_s5_re[l], state_s5_im[l], lw, s5_ops, alpha)
        for acc, val in zip(outs, (mk_p, mv_p, rw_p, sh_p, hr_p, hi_p, rw_s, sh_s, hr_s, hi_s)):
            acc.append(val)
    return (xp, xs) + tuple(jnp.stack(o) for o in outs)
```

```python
import functools
import math

import jax
import jax.numpy as jnp
from jax import lax
from jax.experimental import pallas as pl
from jax.experimental.pallas import tpu as pltpu

F32 = jnp.float32
BF16 = jnp.bfloat16

LN_EPS = 1e-5
GN_EPS = 64e-5
RWKV_HEAD = 64
RANK_W = 64
RANK_A = 64
RANK_G = 128
S5_GROUP = 16
S5_STATE = 64
S5_CHUNK = 16
RWKV_CHUNK = 64
XATTN_HEADS = 4
LANES = 128
S5_SUB = LANES // S5_GROUP
S5_TILE_CHUNKS = 256
VMEM_LIMIT = 56 * 1024 * 1024

NT = (((1,), (1,)), ((), ()))
TN = (((0,), (0,)), ((), ()))


def _mm(a, b, dims=None):
    a = a.astype(BF16)
    b = b.astype(BF16)
    if dims is None:
        return jnp.dot(a, b, preferred_element_type=F32)
    return lax.dot_general(a, b, dims, preferred_element_type=F32)


def _mm_f32(a, b, dims=None):
    if dims is None:
        return jnp.dot(a, b, preferred_element_type=F32, precision=lax.Precision.HIGHEST)
    return lax.dot_general(a, b, dims, preferred_element_type=F32, precision=lax.Precision.HIGHEST)


def _ln(z, g, b):
    mu = jnp.mean(z, axis=-1, keepdims=True)
    d = z - mu
    var = jnp.mean(d * d, axis=-1, keepdims=True)
    return d * lax.rsqrt(var + LN_EPS) * g + b


def _sigmoid(x):
    return 1.0 / (1.0 + jnp.exp(-x))


def _params(*sem):
    return pltpu.CompilerParams(dimension_semantics=sem, vmem_limit_bytes=VMEM_LIMIT)


def _token_tile(n):
    for t in (512, 256, 128, 64, 32, 16, 8):
        if n % t == 0:
            return t
    raise ValueError(f"token count {n} is not a multiple of 8")


def _ffn_ln_kernel(x_ref, wg_ref, wu_ref, wd_ref, g_ref, b_ref, o_ref, xb_ref, acc_ref, *, alpha):
    j = pl.program_id(1)

    @pl.when(j == 0)
    def _():
        xb_ref[...] = x_ref[...].astype(BF16)
        acc_ref[...] = jnp.zeros_like(acc_ref)

    xb = xb_ref[...]
    gate = jnp.dot(xb, wg_ref[...], preferred_element_type=F32)
    up = jnp.dot(xb, wu_ref[...], preferred_element_type=F32)
    h = gate * _sigmoid(gate) * up
    acc_ref[...] += jnp.dot(h.astype(BF16), wd_ref[...], preferred_element_type=F32)

    @pl.when(j == pl.num_programs(1) - 1)
    def _():
        o_ref[...] = _ln(alpha * x_ref[...] + 0.5 * acc_ref[...], g_ref[...], b_ref[...])


def _ffn_ln(x, wg, wu, wd, g, b, alpha):
    n, d = x.shape
    dff = wg.shape[1]
    tm = _token_tile(n)
    tf = 512 if dff % 512 == 0 else dff
    return pl.pallas_call(
        functools.partial(_ffn_ln_kernel, alpha=alpha),
        grid=(n // tm, dff // tf),
        in_specs=[
            pl.BlockSpec((tm, d), lambda i, j: (i, 0)),
            pl.BlockSpec((d, tf), lambda i, j: (0, j)),
            pl.BlockSpec((d, tf), lambda i, j: (0, j)),
            pl.BlockSpec((tf, d), lambda i, j: (j, 0)),
            pl.BlockSpec((1, d), lambda i, j: (0, 0)),
            pl.BlockSpec((1, d), lambda i, j: (0, 0)),
        ],
        out_specs=pl.BlockSpec((tm, d), lambda i, j: (i, 0)),
        out_shape=jax.ShapeDtypeStruct((n, d), F32),
        scratch_shapes=[pltpu.VMEM((tm, d), BF16), pltpu.VMEM((tm, d), F32)],
        compiler_params=_params("parallel", "arbitrary"),
        name="ffn_ln",
    )(x, wg, wu, wd, g, b)


def _proj_kernel(x_ref, w_ref, o_ref):
    o_ref[...] = _mm(x_ref[...], w_ref[...])


def _column_tile(nout, cap=2304):
    best = None
    for t in range(LANES, min(nout, cap) + 1, LANES):
        if nout % t == 0:
            best = t
    return best if best is not None else nout


def _proj(x, w):
    n, d = x.shape
    nout = w.shape[1]
    tm = _token_tile(n)
    tn = _column_tile(nout)
    return pl.pallas_call(
        _proj_kernel,
        grid=(nout // tn, n // tm),
        in_specs=[
            pl.BlockSpec((tm, d), lambda j, i: (i, 0)),
            pl.BlockSpec((d, tn), lambda j, i: (0, j)),
        ],
        out_specs=pl.BlockSpec((tm, tn), lambda j, i: (i, j)),
        out_shape=jax.ShapeDtypeStruct((n, nout), F32),
        compiler_params=_params("parallel", "parallel"),
        name="proj",
    )(x, w)


def _unit_lower_inverses(ms, t):
    row = lax.broadcasted_iota(jnp.int32, (t, t), 0)
    col = lax.broadcasted_iota(jnp.int32, (t, t), 1)
    eye = jnp.where(row == col, 1.0, 0.0).astype(F32)
    invs = None
    k = 1
    while k < t:
        shift = (2 * k).bit_length() - 1
        join = ((row >> shift) == (col >> shift)) & ((row & k) != 0) & ((col & k) == 0)
        cs = [jnp.where(join, m, 0.0) for m in ms]
        if k == 1:
            invs = [eye - c for c in cs]
        else:
            xs = [_mm(c, inv) for c, inv in zip(cs, invs)]
            invs = [inv - _mm(inv, x) for inv, x in zip(invs, xs)]
        k *= 2
    return invs


def _rwkv_kernel(p_ref, shift_ref, s0_ref, mu_ref, w0_ref, wup_ref, a0_ref, aup_ref, gup_ref,
                 kk_ref, ka_ref, rk_ref, gnw_ref, gnb_ref, seg_ref, segt_ref,
                 o_ref, sout_ref, prev_ref, state_ref, *, t, heads):
    c = pl.program_id(1)
    r_w = heads * RWKV_HEAD

    @pl.when(c == 0)
    def _():
        prev_ref[...] = shift_ref[0]
        state_ref[...] = s0_ref[0]

    p = p_ref[0]
    row = lax.broadcasted_iota(jnp.int32, p.shape, 0)
    prev = jnp.where(row == 0, prev_ref[...], pltpu.roll(p, 1, axis=0))
    prev_ref[...] = p[t - 1:t, :]
    ps = p + mu_ref[...] * (prev - p)

    r = ps[:, 0:r_w]
    k = ps[:, r_w:2 * r_w]
    v = ps[:, 2 * r_w:3 * r_w]
    wa_in = ps[:, 3 * r_w:3 * r_w + RANK_W + RANK_A]
    lg = ps[:, 3 * r_w + RANK_W + RANK_A:]

    log_w = -math.exp(-0.5) * _sigmoid(w0_ref[...] + _mm(jnp.tanh(wa_in), wup_ref[...]))
    a = _sigmoid(a0_ref[...] + _mm(wa_in, aup_ref[...]))
    g = _mm(_sigmoid(lg), gup_ref[...])

    seg = seg_ref[...]
    segt = segt_ref[...]

    def head_sum(x):
        return _mm_f32(_mm_f32(x, seg), segt)

    kk = k * kk_ref[...]
    kk = kk * lax.rsqrt(jnp.maximum(head_sum(kk * kk), 1e-24))
    k2 = k * (1.0 + (a - 1.0) * ka_ref[...])
    b = kk * a

    tri_r = lax.broadcasted_iota(jnp.int32, (t, t), 0)
    tri_c = lax.broadcasted_iota(jnp.int32, (t, t), 1)
    incl = tri_r >= tri_c
    strict = tri_r > tri_c
    cum = _mm_f32(jnp.where(incl, 1.0, 0.0).astype(F32), log_w)
    cum_last = cum[t - 1:t, :]
    e_cum = jnp.exp(cum)
    e_neg = jnp.exp(-cum)
    e_tail = jnp.exp(cum_last - cum)
    kq = kk * jnp.exp(cum - log_w)
    rq = r * e_cum
    bd = b * e_neg
    kd = k2 * e_neg
    bt = b * e_tail
    kt = k2 * e_tail
    p_last = e_cum[t - 1:t, :]

    hs = range(heads)
    sls = [slice(h * RWKV_HEAD, (h + 1) * RWKV_HEAD) for h in hs]
    stack = lambda x, y: [jnp.concatenate([x[:, sl], y[:, sl]], axis=0).astype(BF16) for sl in sls]
    lhs = stack(kq, rq)
    rhs = stack(bd, kd)
    tail = stack(bt, kt)
    vs = [v[:, sl].astype(BF16) for sl in sls]
    s_in = [state_ref[h] for h in hs]
    ms = [_mm(lhs[h], rhs[h], NT) for h in hs]
    qs = [_mm(lhs[h], s_in[h], NT) for h in hs]
    mkv = [_mm(jnp.where(strict, ms[h][:t, t:], 0.0), vs[h]) for h in hs]
    invs = _unit_lower_inverses([jnp.where(strict, ms[h][:t, :t], 0.0) for h in hs], t)
    us = [_mm(invs[h], -(qs[h][:t] + mkv[h])) for h in hs]
    uvs = [jnp.concatenate([us[h].astype(BF16), vs[h]], axis=0) for h in hs]
    incl2 = (lax.broadcasted_iota(jnp.int32, (t, 2 * t), 0)
             >= (lax.broadcasted_iota(jnp.int32, (t, 2 * t), 1) & (t - 1)))
    ys = [qs[h][t:] + _mm(jnp.where(incl2, ms[h][t:, :], 0.0), uvs[h]) for h in hs]
    for h in hs:
        state_ref[h] = s_in[h] * p_last[:, sls[h]] + _mm(uvs[h], tail[h], TN)
    y = jnp.concatenate(ys, axis=1)

    inv_hd = 1.0 / RWKV_HEAD
    mu_y = head_sum(y) * inv_hd
    dy = y - mu_y
    var = head_sum(dy * dy) * inv_hd
    yn = dy * lax.rsqrt(var + GN_EPS) * gnw_ref[...] + gnb_ref[...]
    bonus = head_sum(r * k2 * rk_ref[...]) * v
    o_ref[0] = (yn + bonus) * g

    @pl.when(c == pl.num_programs(1) - 1)
    def _():
        sout_ref[0] = state_ref[...]


def _rwkv(p3, shift_prev, s0, lw):
    bsz, seq, _ = p3.shape
    heads = s0.shape[1]
    r_w = heads * RWKV_HEAD
    cols = 3 * r_w + RANK_W + RANK_A + RANK_G
    t = min(RWKV_CHUNK, seq)
    assert seq % t == 0 and t & (t - 1) == 0
    row = lambda x: x.reshape(1, -1).astype(F32)
    zeros = jnp.zeros((RANK_W, r_w), F32)
    wup = jnp.concatenate([lw['w_up'], zeros], axis=0).astype(BF16)
    aup = jnp.concatenate([zeros, lw['a_up']], axis=0).astype(BF16)
    seg = (jnp.arange(r_w)[:, None] // RWKV_HEAD == jnp.arange(heads)[None, :]).astype(F32)
    vec = lambda n: pl.BlockSpec((1, n), lambda b, c: (0, 0))
    full = lambda a, bb: pl.BlockSpec((a, bb), lambda b, c: (0, 0))
    return pl.pallas_call(
        functools.partial(_rwkv_kernel, t=t, heads=heads),
        grid=(bsz, seq // t),
        in_specs=[
            pl.BlockSpec((1, t, cols), lambda b, c: (b, c, 0)),
            pl.BlockSpec((1, 1, cols), lambda b, c: (b, 0, 0)),
            pl.BlockSpec((1, heads, RWKV_HEAD, RWKV_HEAD), lambda b, c: (b, 0, 0, 0)),
            vec(cols), vec(r_w), full(RANK_W + RANK_A, r_w), vec(r_w), full(RANK_W + RANK_A, r_w),
            full(RANK_G, r_w), vec(r_w), vec(r_w), vec(r_w), vec(r_w), vec(r_w),
            full(r_w, heads), full(heads, r_w),
        ],
        out_specs=[
            pl.BlockSpec((1, t, r_w), lambda b, c: (b, c, 0)),
            pl.BlockSpec((1, heads, RWKV_HEAD, RWKV_HEAD), lambda b, c: (b, 0, 0, 0)),
        ],
        out_shape=[
            jax.ShapeDtypeStruct((bsz, seq, r_w), F32),
            jax.ShapeDtypeStruct((bsz, heads, RWKV_HEAD, RWKV_HEAD), F32),
        ],
        scratch_shapes=[pltpu.VMEM((1, cols), F32), pltpu.VMEM((heads, RWKV_HEAD, RWKV_HEAD), F32)],
        compiler_params=_params("parallel", "arbitrary"),
        name="rwkv",
    )(p3, shift_prev.astype(F32), s0.astype(F32), row(lw['shift_mu']), row(lw['w0']), wup, row(lw['a0']), aup,
      lw['g_up'].astype(BF16), row(lw['k_k']), row(lw['k_a']), row(lw['r_k']), row(lw['gn_w']),
      row(lw['gn_b']), seg, seg.T)


def _s5_prep_kernel(are_ref, aim_ref, ldt_ref, btre_ref, btim_ref, cre_ref, cim_ref, d_ref,
                    kst_ref, wre_ref, wim_ref, vre_ref, vimn_ref, lre_ref, lim_ref, *, tc):
    a_re = are_ref[0]
    a_im = aim_ref[0]
    dt = jnp.exp(ldt_ref[0])
    mag = jnp.exp(a_re * dt)
    l_re = mag * jnp.cos(a_im * dt)
    l_im = mag * jnp.sin(a_im * dt)
    den = a_re * a_re + a_im * a_im
    x_re = l_re - 1.0
    co_re = (x_re * a_re + l_im * a_im) / den
    co_im = (l_im * a_re - x_re * a_im) / den
    pw = [(jnp.ones_like(l_re), jnp.zeros_like(l_re))]
    for _ in range(tc):
        q_re, q_im = pw[-1]
        pw.append((q_re * l_re - q_im * l_im, q_re * l_im + q_im * l_re))

    c_re = cre_ref[0]
    c_im = cim_ref[0]
    bt_re = btre_ref[0]
    bt_im = btim_ref[0]
    cc_re = c_re * co_re - c_im * co_im
    cc_im = c_re * co_im + c_im * co_re
    cl_re = jnp.concatenate([cc_re * q[0] - cc_im * q[1] for q in pw[:tc]], axis=0)
    cl_im = jnp.concatenate([cc_re * q[1] + cc_im * q[0] for q in pw[:tc]], axis=0)
    kst = _mm_f32(cl_re, bt_re, NT) - _mm_f32(cl_im, bt_im, NT)
    rr = lax.broadcasted_iota(jnp.int32, kst.shape, 0)
    cc = lax.broadcasted_iota(jnp.int32, kst.shape, 1)
    kst_ref[0] = kst + jnp.where(rr == cc, d_ref[0], 0.0)

    w_re, w_im = [], []
    for s in range(tc):
        q_re, q_im = pw[tc - 1 - s]
        f_re = q_re * co_re - q_im * co_im
        f_im = q_re * co_im + q_im * co_re
        w_re.append(bt_re * f_re - bt_im * f_im)
        w_im.append(bt_re * f_im + bt_im * f_re)
    wre_ref[0] = jnp.concatenate(w_re, axis=0)
    wim_ref[0] = jnp.concatenate(w_im, axis=0)
    vre_ref[0] = jnp.concatenate([c_re * q[0] - c_im * q[1] for q in pw[1:]], axis=0)
    vimn_ref[0] = jnp.concatenate([-(c_re * q[1] + c_im * q[0]) for q in pw[1:]], axis=0)
    lre_ref[0] = pw[tc][0]
    lim_ref[0] = pw[tc][1]


def _s5_prep(lw):
    groups, state = lw['a_re'].shape
    ch = S5_GROUP
    tc = S5_CHUNK
    g3 = lambda x, a, b: x.astype(F32).reshape(groups, a, b)
    spec = lambda a, b: pl.BlockSpec((1, a, b), lambda g: (g, 0, 0))
    outs = pl.pallas_call(
        functools.partial(_s5_prep_kernel, tc=tc),
        grid=(groups,),
        in_specs=[spec(1, state), spec(1, state), spec(1, 1), spec(ch, state), spec(ch, state),
                  spec(ch, state), spec(ch, state), spec(1, ch)],
        out_specs=[spec(tc * ch, ch), spec(tc * ch, state), spec(tc * ch, state), spec(tc * ch, state),
                   spec(tc * ch, state), spec(1, state), spec(1, state)],
        out_shape=[jax.ShapeDtypeStruct((groups, tc * ch, ch), F32)]
        + [jax.ShapeDtypeStruct((groups, tc * ch, state), F32)] * 4
        + [jax.ShapeDtypeStruct((groups, 1, state), F32)] * 2,
        compiler_params=_params("parallel"),
        name="s5_prep",
    )(g3(lw['a_re'], 1, state), g3(lw['a_im'], 1, state), g3(lw['log_dt'], 1, 1),
      jnp.swapaxes(lw['b_re'], 1, 2).astype(F32), jnp.swapaxes(lw['b_im'], 1, 2).astype(F32),
      lw['c_re'].astype(F32), lw['c_im'].astype(F32), g3(lw['d'], 1, ch))
    kst, w_re, w_im, v_re, v_imn, l_re, l_im = outs
    k4 = kst.reshape(groups, tc, ch, ch)
    tt = jnp.arange(tc)
    tau = tt[:, None] - tt[None, :]
    blocks = jnp.where((tau >= 0)[None, :, :, None, None], k4[:, jnp.clip(tau, 0, tc - 1)], 0.0)
    toep = jnp.transpose(blocks, (0, 1, 3, 2, 4)).reshape(groups, tc * ch, tc * ch)
    return dict(
        toep=toep.astype(BF16),
        w=jnp.concatenate([w_re, w_im, w_im, w_re], axis=-1).astype(BF16),
        vt=jnp.concatenate([v_re, v_imn], axis=-1).astype(BF16),
        a16=jnp.concatenate([l_re, l_re], axis=-1),
        b16=jnp.concatenate([-l_im, l_im], axis=-1),
    )


def _lane_block_transpose(arrs, masks):
    n = len(arrs)
    rolled = [[a if j == 0 else pltpu.roll(a, S5_GROUP * j, axis=1) for j in range(n)] for a in arrs]
    outs = []
    for j in range(n):
        out = rolled[0][(0 - j) % n]
        for i in range(1, n):
            out = jnp.where(masks[i], rolled[i][(i - j) % n], out)
        outs.append(out)
    return outs


def _s5_kernel(x_ref, toep_ref, w_ref, vt_ref, a_ref, b_ref, h0_ref, h0s_ref,
               y_ref, hout_ref, x_st, s_st, ex_ref, es_ref, hs_ref, *, nct):
    ci = pl.program_id(2)
    sub = S5_SUB
    halves = S5_CHUNK // sub
    width = a_ref.shape[-1]

    @pl.when(ci == 0)
    def _():
        x_st[...] = h0_ref[0, 0]
        s_st[...] = h0s_ref[0, 0]

    lane = lax.broadcasted_iota(jnp.int32, (nct, LANES), 1)
    masks = [(lane >= S5_GROUP * j) & (lane < S5_GROUP * (j + 1)) for j in range(sub)]

    rows_of = lambda t: pl.ds(t, nct, stride=S5_CHUNK)
    u_halves = [_lane_block_transpose([x_ref[0, rows_of(hf * sub + tt), :] for tt in range(sub)], masks)
                for hf in range(halves)]
    us = [jnp.concatenate([u_halves[hf][g] for hf in range(halves)], axis=1).astype(BF16) for g in range(sub)]

    for g in range(sub):
        e = _mm(us[g], w_ref[g])
        ex_ref[pl.ds(g, nct, stride=sub), :] = e[:, :width]
        es_ref[pl.ds(g, nct, stride=sub), :] = e[:, width:]

    a = a_ref[0]
    b = b_ref[0]
    steps = min(8, nct)

    def body(i, carry):
        x, s = carry
        for j in range(steps):
            rows = pl.ds(pl.multiple_of((i * steps + j) * sub, sub), sub)
            hs_ref[rows, :] = x
            x, s = a * x + b * s + ex_ref[rows, :], a * s - b * x + es_ref[rows, :]
        return x, s

    x, s = lax.fori_loop(0, nct // steps, body, (x_st[...], s_st[...]))
    x_st[...] = x
    s_st[...] = s
    hout_ref[0, 0] = x

    ys = [_mm(us[g], toep_ref[g], NT) + _mm(hs_ref[pl.ds(g, nct, stride=sub), :], vt_ref[g], NT)
          for g in range(sub)]
    for hf in range(halves):
        outs = _lane_block_transpose([ys[g][:, hf * LANES:(hf + 1) * LANES] for g in range(sub)], masks)
        for tt in range(sub):
            y_ref[0, rows_of(hf * sub + tt), :] = outs[tt]


def _s5(p3, col0, h_re, h_im, ops):
    bsz, seq, n_in = p3.shape
    groups, state = h_re.shape[1:]
    ch, tc, sub = S5_GROUP, S5_CHUNK, S5_SUB
    assert seq % tc == 0 and groups % sub == 0 and col0 % LANES == 0 and tc % sub == 0
    n_chunks = seq // tc
    nct = min(S5_TILE_CHUNKS, n_chunks)
    assert n_chunks % nct == 0 and nct % min(8, nct) == 0
    gbs = groups // sub
    lane0 = col0 // LANES
    pack = lambda x, y: jnp.concatenate([x, y], axis=-1).astype(F32).reshape(bsz, gbs, sub, 2 * state)
    per_g = lambda last: pl.BlockSpec((sub, tc * ch, last), lambda gb, b, ci: (gb, 0, 0))
    vec = pl.BlockSpec((1, sub, 2 * state), lambda gb, b, ci: (gb, 0, 0))
    st = pl.BlockSpec((1, 1, sub, 2 * state), lambda gb, b, ci: (b, gb, 0, 0))
    y, h_out = pl.pallas_call(
        functools.partial(_s5_kernel, nct=nct),
        grid=(gbs, bsz, n_chunks // nct),
        in_specs=[pl.BlockSpec((1, nct * tc, LANES), lambda gb, b, ci: (b, ci, lane0 + gb)),
                  per_g(tc * ch), per_g(4 * state), per_g(2 * state), vec, vec, st, st],
        out_specs=[pl.BlockSpec((1, nct * tc, LANES), lambda gb, b, ci: (b, ci, gb)), st],
        out_shape=[jax.ShapeDtypeStruct((bsz, seq, groups * ch), F32),
                   jax.ShapeDtypeStruct((bsz, gbs, sub, 2 * state), F32)],
        scratch_shapes=[pltpu.VMEM((sub, 2 * state), F32), pltpu.VMEM((sub, 2 * state), F32),
                        pltpu.VMEM((nct * sub, 2 * state), F32), pltpu.VMEM((nct * sub, 2 * state), F32),
                        pltpu.VMEM((nct * sub, 2 * state), F32)],
        compiler_params=_params("parallel", "parallel", "arbitrary"),
        name="s5",
    )(p3, ops['toep'], ops['w'], ops['vt'],
      ops['a16'].reshape(gbs, sub, 2 * state), ops['b16'].reshape(gbs, sub, 2 * state),
      pack(h_re, h_im), pack(h_im, h_re))
    h_out = h_out.reshape(bsz, groups, 2 * state)
    return y, h_out[..., :state], h_out[..., state:]


def _gelu_tanh(x):
    return 0.5 * x * (1.0 + jnp.tanh(math.sqrt(2.0 / math.pi) * (x + 0.044715 * (x * x * x))))


def _mix_kernel(x_ref, rw_ref, y_ref, gw_ref, gb_ref, wo1_ref, wo2_ref, g_ref, b_ref, o_ref, *, alpha):
    y = _gelu_tanh(y_ref[...])
    s5o = y * _sigmoid(_mm(y, gw_ref[...]) + gb_ref[...])
    mix = _mm(rw_ref[...], wo1_ref[...]) + _mm(s5o, wo2_ref[...])
    o_ref[...] = _ln(alpha * x_ref[...] + mix, g_ref[...], b_ref[...])


def _mix(x, rw, y, glu_w, glu_b, w_out, g, b, alpha):
    n, d = x.shape
    r_w = rw.shape[1]
    s_w = y.shape[1]
    tm = min(_token_tile(n), 256)
    tile = lambda w: pl.BlockSpec((tm, w), lambda i: (i, 0))
    full = lambda a, bb: pl.BlockSpec((a, bb), lambda i: (0, 0))
    return pl.pallas_call(
        functools.partial(_mix_kernel, alpha=alpha),
        grid=(n // tm,),
        in_specs=[tile(d), tile(r_w), tile(s_w), full(s_w, s_w), full(1, s_w), full(r_w, d), full(s_w, d),
                  full(1, d), full(1, d)],
        out_specs=tile(d),
        out_shape=jax.ShapeDtypeStruct((n, d), F32),
        compiler_params=_params("parallel"),
        name="mix",
    )(x, rw, y, glu_w.astype(BF16), glu_b.reshape(1, -1).astype(F32), w_out[:r_w].astype(BF16),
      w_out[r_w:].astype(BF16), g, b)


def _attn_kernel(x_ref, mk_ref, mv_ref, wq_ref, wo_ref, g_ref, b_ref, o_ref, *, alpha, heads):
    x = x_ref[...]
    d = x.shape[-1]
    hd = d // heads
    q = _mm(x, wq_ref[...]) * (hd ** -0.5)
    mk = mk_ref[0]
    mv = mv_ref[0]
    outs = []
    for h in range(heads):
        sl = slice(h * hd, (h + 1) * hd)
        s = _mm(q[:, sl], mk[:, sl], NT)
        e = jnp.exp(s - jnp.max(s, axis=-1, keepdims=True))
        pr = e / jnp.sum(e, axis=-1, keepdims=True)
        outs.append(_mm(pr, mv[:, sl]))
    o = jnp.concatenate(outs, axis=-1)
    o_ref[...] = _ln(alpha * x + _mm(o, wo_ref[...]), g_ref[...], b_ref[...])


def _attn(x, mk, mv, wq, wo, g, b, alpha, seq):
    n, d = x.shape
    n_mem = mk.shape[1]
    tm = min(_token_tile(n), _token_tile(seq), 256)
    per_batch = seq // tm
    tile = pl.BlockSpec((tm, d), lambda i: (i, 0))
    mem = pl.BlockSpec((1, n_mem, d), lambda i: (i // per_batch, 0, 0))
    full = lambda a, bb: pl.BlockSpec((a, bb), lambda i: (0, 0))
    return pl.pallas_call(
        functools.partial(_attn_kernel, alpha=alpha, heads=XATTN_HEADS),
        grid=(n // tm,),
        in_specs=[tile, mem, mem, full(d, d), full(d, d), full(1, d), full(1, d)],
        out_specs=tile,
        out_shape=jax.ShapeDtypeStruct((n, d), F32),
        compiler_params=_params("parallel"),
        name="attn",
    )(x, mk.astype(BF16), mv.astype(BF16), wq, wo, g, b)


def _layer(x, mk, mv, shift_prev, s_rwkv, h_re, h_im, lw, s5_ops, alpha):
    bsz, seq, d = x.shape
    n = bsz * seq
    ln = lambda i: (lw['ln_g'][i].reshape(1, d).astype(F32), lw['ln_b'][i].reshape(1, d).astype(F32))
    cols = shift_prev.shape[-1]
    x1 = _ffn_ln(x.reshape(n, d), lw['f1g'], lw['f1u'], lw['f1d'], *ln(0), alpha)
    p = _proj(x1, lw['w_in']).reshape(bsz, seq, -1)
    rw, s_new = _rwkv(p, shift_prev, s_rwkv, lw)
    y, hr, hi = _s5(p, cols, h_re, h_im, s5_ops)
    x2 = _mix(x1, rw.reshape(n, -1), y.reshape(n, -1), lw['glu_w'], lw['glu_b'], lw['w_out'], *ln(1), alpha)
    x3 = _attn(x2, mk.reshape(bsz, -1, d), mv.reshape(bsz, -1, d), lw['xq'], lw['xo'], *ln(2), alpha, seq)
    x4 = _ffn_ln(x3, lw['f2g'], lw['f2u'], lw['f2d'], *ln(3), alpha)
    return x4.reshape(bsz, seq, d), p[:, -1:, :cols], s_new, hr, hi


def kernel(x_prompt, x_sample, mem_prompt, cache_mem_k, cache_mem_v, state_rwkv, cache_shift, state_s5_re, state_s5_im, ln_g, ln_b, ffn1_gate, ffn1_up, ffn1_down, w_in, shift_mu, rwkv_w0, rwkv_w_up, rwkv_a0, rwkv_a_up, rwkv_g_up, rwkv_k_k, rwkv_k_a, rwkv_r_k, rwkv_gn_w, rwkv_gn_b, s5_a_re, s5_a_im, s5_log_dt, s5_b_re, s5_b_im, s5_c_re, s5_c_im, s5_d, s5_glu_w, s5_glu_b, w_mix_out, xattn_q, xattn_k, xattn_v, xattn_o, ffn2_gate, ffn2_up, ffn2_down):
    depth = ln_g.shape[0]
    bp, _, d = x_prompt.shape
    n_mem = mem_prompt.shape[1]
    heads, hd = state_rwkv.shape[2], state_rwkv.shape[3]
    cols = cache_shift.shape[-1]
    groups, state = state_s5_re.shape[2:]
    alpha = (2.0 * depth) ** 0.25
    bf = lambda w: w.astype(BF16)
    xp, xs = x_prompt, x_sample
    outs = [[] for _ in range(10)]
    for l in range(depth):
        lw = dict(ln_g=ln_g[l], ln_b=ln_b[l], f1g=bf(ffn1_gate[l]), f1u=bf(ffn1_up[l]), f1d=bf(ffn1_down[l]),
                  w_in=bf(w_in[l]), shift_mu=shift_mu[l], w0=rwkv_w0[l], w_up=rwkv_w_up[l], a0=rwkv_a0[l],
                  a_up=rwkv_a_up[l], g_up=rwkv_g_up[l], k_k=rwkv_k_k[l], k_a=rwkv_k_a[l], r_k=rwkv_r_k[l],
                  gn_w=rwkv_gn_w[l], gn_b=rwkv_gn_b[l], a_re=s5_a_re[l], a_im=s5_a_im[l], log_dt=s5_log_dt[l],
                  b_re=s5_b_re[l], b_im=s5_b_im[l], c_re=s5_c_re[l], c_im=s5_c_im[l], d=s5_d[l],
                  glu_w=s5_glu_w[l], glu_b=s5_glu_b[l], w_out=w_mix_out[l], xq=bf(xattn_q[l]), xo=bf(xattn_o[l]),
                  f2g=bf(ffn2_gate[l]), f2u=bf(ffn2_up[l]), f2d=bf(ffn2_down[l]))
        s5_ops = _s5_prep(lw)
        mem2 = mem_prompt.reshape(bp * n_mem, d)
        mk_p = _proj(mem2, bf(xattn_k[l])).reshape(bp, n_mem, XATTN_HEADS, d // XATTN_HEADS)
        mv_p = _proj(mem2, bf(xattn_v[l])).reshape(bp, n_mem, XATTN_HEADS, d // XATTN_HEADS)
        xp, sh_p, rw_p, hr_p, hi_p = _layer(
            xp, mk_p, mv_p, jnp.zeros((bp, 1, cols), F32), jnp.zeros((bp, heads, hd, hd), F32),
            jnp.zeros((bp, groups, state), F32), jnp.zeros((bp, groups, state), F32), lw, s5_ops, alpha)
        xs, sh_s, rw_s, hr_s, hi_s = _layer(
            xs, cache_mem_k[l], cache_mem_v[l], cache_shift[l], state_rwkv[l],
            state_s5_re[l], state_s5_im[l], lw, s5_ops, alpha)
        for acc, val in zip(outs, (mk_p, mv_p, rw_p, sh_p, hr_p, hi_p, rw_s, sh_s, hr_s, hi_s)):
            acc.append(val)
    return (xp, xs) + tuple(jnp.stack(o) for o in outs)
```

```python
import functools
import math

import jax
import jax.numpy as jnp
from jax import lax
from jax.experimental import pallas as pl
from jax.experimental.pallas import tpu as pltpu

F32 = jnp.float32
BF16 = jnp.bfloat16

LN_EPS = 1e-5
GN_EPS = 64e-5
RWKV_HEAD = 64
RANK_W = 64
RANK_A = 64
RANK_G = 128
S5_GROUP = 16
S5_STATE = 64
S5_CHUNK = 16
RWKV_CHUNK = 64
RWKV_SUBCHUNKS = 4
XATTN_HEADS = 4
LANES = 128
MXU_TILE = 256
S5_SUB = LANES // S5_GROUP
S5_TILE_CHUNKS = 256
VMEM_LIMIT = 56 * 1024 * 1024

NT = (((1,), (1,)), ((), ()))
TN = (((0,), (0,)), ((), ()))


def _mm(a, b, dims=None):
    a = a.astype(BF16)
    b = b.astype(BF16)
    if dims is None:
        return jnp.dot(a, b, preferred_element_type=F32)
    return lax.dot_general(a, b, dims, preferred_element_type=F32)


def _mm_f32(a, b, dims=None):
    if dims is None:
        return jnp.dot(a, b, preferred_element_type=F32, precision=lax.Precision.HIGHEST)
    return lax.dot_general(a, b, dims, preferred_element_type=F32, precision=lax.Precision.HIGHEST)


def _split_bf16(x, parts):
    pieces = []
    for _ in range(parts):
        piece = x.astype(BF16)
        pieces.append(piece)
        x = x - piece.astype(F32)
    return pieces


def _ln(z, g, b):
    mu = jnp.mean(z, axis=-1, keepdims=True)
    d = z - mu
    var = jnp.mean(d * d, axis=-1, keepdims=True)
    return d * lax.rsqrt(var + LN_EPS) * g + b


def _sigmoid(x):
    return 1.0 / (1.0 + jnp.exp(-x))


def _params(*sem):
    return pltpu.CompilerParams(dimension_semantics=sem, vmem_limit_bytes=VMEM_LIMIT)


def _token_tile(n):
    for t in (512, 256, 128, 64, 32, 16, 8):
        if n % t == 0:
            return t
    raise ValueError(f"token count {n} is not a multiple of 8")


def _ffn_ln_kernel(x_ref, wg_ref, wu_ref, wd_ref, g_ref, b_ref, o_ref, xb_ref, acc_ref, *, alpha):
    j = pl.program_id(1)

    @pl.when(j == 0)
    def _():
        xb_ref[...] = x_ref[...].astype(BF16)
        acc_ref[...] = jnp.zeros_like(acc_ref)

    xb = xb_ref[...]
    gate = jnp.dot(xb, wg_ref[...], preferred_element_type=F32)
    up = jnp.dot(xb, wu_ref[...], preferred_element_type=F32)
    h = gate * _sigmoid(gate) * up
    acc_ref[...] += jnp.dot(h.astype(BF16), wd_ref[...], preferred_element_type=F32)

    @pl.when(j == pl.num_programs(1) - 1)
    def _():
        o_ref[...] = _ln(alpha * x_ref[...] + 0.5 * acc_ref[...], g_ref[...], b_ref[...])


def _ffn_ln(x, wg, wu, wd, g, b, alpha):
    n, d = x.shape
    dff = wg.shape[1]
    tm = _token_tile(n)
    tf = 512 if dff % 512 == 0 else dff
    return pl.pallas_call(
        functools.partial(_ffn_ln_kernel, alpha=alpha),
        grid=(n // tm, dff // tf),
        in_specs=[
            pl.BlockSpec((tm, d), lambda i, j: (i, 0)),
            pl.BlockSpec((d, tf), lambda i, j: (0, j)),
            pl.BlockSpec((d, tf), lambda i, j: (0, j)),
            pl.BlockSpec((tf, d), lambda i, j: (j, 0)),
            pl.BlockSpec((1, d), lambda i, j: (0, 0)),
            pl.BlockSpec((1, d), lambda i, j: (0, 0)),
        ],
        out_specs=pl.BlockSpec((tm, d), lambda i, j: (i, 0)),
        out_shape=jax.ShapeDtypeStruct((n, d), F32),
        scratch_shapes=[pltpu.VMEM((tm, d), BF16), pltpu.VMEM((tm, d), F32)],
        compiler_params=_params("parallel", "arbitrary"),
        name="ffn_ln",
    )(x, wg, wu, wd, g, b)


def _proj_kernel(x_ref, w_ref, o_ref):
    o_ref[...] = _mm(x_ref[...], w_ref[...])


def _column_tile(nout, cap=2304):
    best = None
    for t in range(LANES, min(nout, cap) + 1, LANES):
        if nout % t == 0:
            best = t
    return best if best is not None else nout


def _proj(x, w):
    n, d = x.shape
    nout = w.shape[1]
    tm = _token_tile(n)
    tn = _column_tile(nout)
    return pl.pallas_call(
        _proj_kernel,
        grid=(nout // tn, n // tm),
        in_specs=[
            pl.BlockSpec((tm, d), lambda j, i: (i, 0)),
            pl.BlockSpec((d, tn), lambda j, i: (0, j)),
        ],
        out_specs=pl.BlockSpec((tm, tn), lambda j, i: (i, j)),
        out_shape=jax.ShapeDtypeStruct((n, nout), F32),
        compiler_params=_params("parallel", "parallel"),
        name="proj",
    )(x, w)


def _rwkv_kernel(p_ref, shift_ref, s0_ref, mu_ref, w0_ref, wup_ref, a0_ref, aup_ref, gup_ref,
                 kk_ref, ka_ref, rk_ref, gnw_ref, gnb_ref,
                 o_ref, sout_ref, prev_ref, state_ref, *, t, n_sub, heads):
    c = pl.program_id(1)
    r_w = heads * RWKV_HEAD
    gw = min(MXU_TILE, r_w)
    per = gw // RWKV_HEAD
    mw = per * t
    head_shift = RWKV_HEAD.bit_length() - 1
    bf = lambda x: x.astype(BF16)

    @pl.when(c == 0)
    def _():
        prev_ref[...] = shift_ref[0]
        state_ref[...] = s0_ref[0]

    same_head = ((lax.broadcasted_iota(jnp.int32, (gw, gw), 0) >> head_shift)
                 == (lax.broadcasted_iota(jnp.int32, (gw, gw), 1) >> head_shift))
    ones_bd = jnp.where(same_head, 1.0, 0.0).astype(BF16)

    def head_sum(x):
        return jnp.concatenate(
            [sum(jnp.dot(piece, ones_bd, preferred_element_type=F32) for piece in _split_bf16(x[:, q:q + gw], 2))
             for q in range(0, r_w, gw)], axis=1)

    tri = jnp.where(lax.broadcasted_iota(jnp.int32, (t, t), 0) >= lax.broadcasted_iota(jnp.int32, (t, t), 1),
                    1.0, 0.0).astype(BF16)

    def prologue(p, prev_row):
        row = lax.broadcasted_iota(jnp.int32, p.shape, 0)
        prev = jnp.where(row == 0, prev_row, pltpu.roll(p, 1, axis=0))
        ps = p + mu_ref[...] * (prev - p)
        r = ps[:, 0:r_w]
        k = ps[:, r_w:2 * r_w]
        v = ps[:, 2 * r_w:3 * r_w]
        wa_in = ps[:, 3 * r_w:3 * r_w + RANK_W + RANK_A]
        lg = ps[:, 3 * r_w + RANK_W + RANK_A:]
        log_w = -math.exp(-0.5) * _sigmoid(w0_ref[...] + _mm(jnp.tanh(wa_in), wup_ref[...]))
        a = _sigmoid(a0_ref[...] + _mm(wa_in, aup_ref[...]))
        g = _mm(_sigmoid(lg), gup_ref[...])
        kk = k * kk_ref[...]
        kk = kk * lax.rsqrt(jnp.maximum(head_sum(kk * kk), 1e-24))
        k2 = k * (1.0 + (a - 1.0) * ka_ref[...])
        b = kk * a
        cum = sum(jnp.dot(tri, piece, preferred_element_type=F32) for piece in _split_bf16(log_w, 3))
        e_cum = jnp.exp(cum)
        e_neg = jnp.exp(-cum)
        e_tail = jnp.exp(cum[t - 1:t, :] - cum)
        return dict(kq=kk * jnp.exp(cum - log_w), rq=r * e_cum, bd=b * e_neg, kd=k2 * e_neg, bt=b * e_tail,
                    kt=k2 * e_tail, v=v, p_last=e_cum[t - 1:t, :], bonus=head_sum(r * k2 * rk_ref[...]) * v, g=g)

    pro = []
    prev_row = prev_ref[...]
    for j in range(n_sub):
        p = p_ref[0, j * t:(j + 1) * t, :]
        pro.append(prologue(p, prev_row))
        prev_row = p[t - 1:t, :]
    prev_ref[...] = prev_row

    def block_rows(x, bw):
        xb = bf(x)
        blk = lax.broadcasted_iota(jnp.int32, xb.shape, 1) >> (bw.bit_length() - 1)
        return jnp.concatenate([jnp.where(blk == h, xb, jnp.zeros_like(xb)) for h in range(per)], axis=0)

    lane_blk = lax.broadcasted_iota(jnp.int32, (RWKV_HEAD, gw), 1) >> head_shift

    def diag_blocks(z):
        return sum(jnp.where(lane_blk == h, z[h * RWKV_HEAD:(h + 1) * RWKV_HEAD, :], 0.0) for h in range(per))

    quads = range(0, r_w, gw)
    probs = [(j, q) for j in range(n_sub) for q in quads]
    ps_ = range(len(probs))
    part = lambda name: [pro[j][name][:, q:q + gw] for j, q in probs]
    kq, rq, bd, kd, bt, kt, v = map(part, ("kq", "rq", "bd", "kd", "bt", "kt", "v"))
    row_m = lax.broadcasted_iota(jnp.int32, (t, mw), 0)
    col_m = lax.broadcasted_iota(jnp.int32, (t, mw), 1) & (t - 1)
    strict = row_m > col_m
    incl = row_m >= col_m

    vbd = [block_rows(v[i], RWKV_HEAD) for i in ps_]
    ms = [_mm(jnp.concatenate([kq[i], rq[i]], axis=0),
              jnp.concatenate([block_rows(bd[i], RWKV_HEAD), block_rows(kd[i], RWKV_HEAD)], axis=0), NT) for i in ps_]
    m_b = [jnp.where(strict, ms[i][:t, :mw], 0.0) for i in ps_]
    mkv = [_mm(jnp.where(strict, ms[i][:t, mw:], 0.0), vbd[i]) for i in ps_]
    l_b = [bf(jnp.where(incl, ms[i][t:, :mw], 0.0)) for i in ps_]
    l_k = [bf(jnp.where(incl, ms[i][t:, mw:], 0.0)) for i in ps_]
    invs = None
    k = 1
    while k < t:
        shift = (2 * k).bit_length() - 1
        join = ((row_m >> shift) == (col_m >> shift)) & ((row_m & k) != 0) & ((col_m & k) == 0)
        cs = [jnp.where(join, m, 0.0) for m in m_b]
        if k == 1:
            invs = [jnp.where(row_m == col_m, 1.0, 0.0) - c for c in cs]
        else:
            xs = [_mm(cs[i], block_rows(invs[i], t)) for i in ps_]
            invs = [invs[i] - _mm(invs[i], block_rows(xs[i], t)) for i in ps_]
        k *= 2
    invs = [bf(x) for x in invs]
    gk = [_mm(invs[i], block_rows(kq[i], RWKV_HEAD)) for i in ps_]
    u0 = [-_mm(invs[i], block_rows(mkv[i], RWKV_HEAD)) for i in ps_]
    rp = [bf(rq[i] - _mm(l_b[i], block_rows(gk[i], RWKV_HEAD))) for i in ps_]
    y0 = [_mm(jnp.concatenate([l_b[i], l_k[i]], axis=1),
              jnp.concatenate([block_rows(u0[i], RWKV_HEAD), vbd[i]], axis=0)) for i in ps_]
    phi = [block_rows(diag_blocks(_mm(gk[i], bt[i], TN)), RWKV_HEAD) for i in ps_]
    psi = [diag_blocks(_mm(jnp.concatenate([u0[i], v[i]], axis=0), jnp.concatenate([bt[i], kt[i]], axis=0), TN))
           for i in ps_]

    states = [state_ref[:, q:q + gw] for q in quads]
    n_q = len(states)
    inv_hd = 1.0 / RWKV_HEAD
    for j in range(n_sub):
        ys = []
        for qi in range(n_q):
            i = j * n_q + qi
            s = states[qi]
            ys.append(y0[i] + _mm(rp[i], block_rows(s, RWKV_HEAD), NT))
            states[qi] = s * pro[j]["p_last"][:, quads[qi]:quads[qi] + gw] - _mm(s, phi[i]) + psi[i]
        y = jnp.concatenate(ys, axis=1)
        dy = y - head_sum(y) * inv_hd
        var = head_sum(dy * dy) * inv_hd
        yn = dy * lax.rsqrt(var + GN_EPS) * gnw_ref[...] + gnb_ref[...]
        o_ref[0, j * t:(j + 1) * t, :] = (yn + pro[j]["bonus"]) * pro[j]["g"]
    for qi, q in enumerate(quads):
        state_ref[:, q:q + gw] = states[qi]

    @pl.when(c == pl.num_programs(1) - 1)
    def _():
        sout_ref[0] = state_ref[...]


def _rwkv(p3, shift_prev, s0, lw):
    bsz, seq, _ = p3.shape
    heads = s0.shape[1]
    r_w = heads * RWKV_HEAD
    cols = 3 * r_w + RANK_W + RANK_A + RANK_G
    t = min(RWKV_CHUNK, seq)
    assert seq % t == 0 and t & (t - 1) == 0
    n_sub = RWKV_SUBCHUNKS if seq % (RWKV_SUBCHUNKS * t) == 0 else 1
    tt = n_sub * t
    row = lambda x: x.reshape(1, -1).astype(F32)
    zeros = jnp.zeros((RANK_W, r_w), F32)
    wup = jnp.concatenate([lw['w_up'], zeros], axis=0).astype(BF16)
    aup = jnp.concatenate([zeros, lw['a_up']], axis=0).astype(BF16)
    vec = lambda n: pl.BlockSpec((1, n), lambda b, c: (0, 0))
    full = lambda a, bb: pl.BlockSpec((a, bb), lambda b, c: (0, 0))
    lanes_hk = lambda s: jnp.swapaxes(s.astype(F32), 1, 2).reshape(bsz, RWKV_HEAD, r_w)
    out, s_new = pl.pallas_call(
        functools.partial(_rwkv_kernel, t=t, n_sub=n_sub, heads=heads),
        grid=(bsz, seq // tt),
        in_specs=[
            pl.BlockSpec((1, tt, cols), lambda b, c: (b, c, 0)),
            pl.BlockSpec((1, 1, cols), lambda b, c: (b, 0, 0)),
            pl.BlockSpec((1, RWKV_HEAD, r_w), lambda b, c: (b, 0, 0)),
            vec(cols), vec(r_w), full(RANK_W + RANK_A, r_w), vec(r_w), full(RANK_W + RANK_A, r_w),
            full(RANK_G, r_w), vec(r_w), vec(r_w), vec(r_w), vec(r_w), vec(r_w),
        ],
        out_specs=[
            pl.BlockSpec((1, tt, r_w), lambda b, c: (b, c, 0)),
            pl.BlockSpec((1, RWKV_HEAD, r_w), lambda b, c: (b, 0, 0)),
        ],
        out_shape=[
            jax.ShapeDtypeStruct((bsz, seq, r_w), F32),
            jax.ShapeDtypeStruct((bsz, RWKV_HEAD, r_w), F32),
        ],
        scratch_shapes=[pltpu.VMEM((1, cols), F32), pltpu.VMEM((RWKV_HEAD, r_w), F32)],
        compiler_params=_params("parallel", "arbitrary"),
        name="rwkv",
    )(p3, shift_prev.astype(F32), lanes_hk(s0), row(lw['shift_mu']), row(lw['w0']), wup, row(lw['a0']), aup,
      lw['g_up'].astype(BF16), row(lw['k_k']), row(lw['k_a']), row(lw['r_k']), row(lw['gn_w']),
      row(lw['gn_b']))
    return out, jnp.swapaxes(s_new.reshape(bsz, RWKV_HEAD, heads, RWKV_HEAD), 1, 2)


def _s5_prep_kernel(are_ref, aim_ref, ldt_ref, btre_ref, btim_ref, cre_ref, cim_ref, d_ref,
                    kst_ref, wre_ref, wim_ref, vre_ref, vimn_ref, lre_ref, lim_ref, *, tc):
    a_re = are_ref[0]
    a_im = aim_ref[0]
    dt = jnp.exp(ldt_ref[0])
    mag = jnp.exp(a_re * dt)
    l_re = mag * jnp.cos(a_im * dt)
    l_im = mag * jnp.sin(a_im * dt)
    den = a_re * a_re + a_im * a_im
    x_re = l_re - 1.0
    co_re = (x_re * a_re + l_im * a_im) / den
    co_im = (l_im * a_re - x_re * a_im) / den
    pw = [(jnp.ones_like(l_re), jnp.zeros_like(l_re))]
    for _ in range(tc):
        q_re, q_im = pw[-1]
        pw.append((q_re * l_re - q_im * l_im, q_re * l_im + q_im * l_re))

    c_re = cre_ref[0]
    c_im = cim_ref[0]
    bt_re = btre_ref[0]
    bt_im = btim_ref[0]
    cc_re = c_re * co_re - c_im * co_im
    cc_im = c_re * co_im + c_im * co_re
    cl_re = jnp.concatenate([cc_re * q[0] - cc_im * q[1] for q in pw[:tc]], axis=0)
    cl_im = jnp.concatenate([cc_re * q[1] + cc_im * q[0] for q in pw[:tc]], axis=0)
    kst = _mm_f32(cl_re, bt_re, NT) - _mm_f32(cl_im, bt_im, NT)
    rr = lax.broadcasted_iota(jnp.int32, kst.shape, 0)
    cc = lax.broadcasted_iota(jnp.int32, kst.shape, 1)
    kst_ref[0] = kst + jnp.where(rr == cc, d_ref[0], 0.0)

    w_re, w_im = [], []
    for s in range(tc):
        q_re, q_im = pw[tc - 1 - s]
        f_re = q_re * co_re - q_im * co_im
        f_im = q_re * co_im + q_im * co_re
        w_re.append(bt_re * f_re - bt_im * f_im)
        w_im.append(bt_re * f_im + bt_im * f_re)
    wre_ref[0] = jnp.concatenate(w_re, axis=0)
    wim_ref[0] = jnp.concatenate(w_im, axis=0)
    vre_ref[0] = jnp.concatenate([c_re * q[0] - c_im * q[1] for q in pw[1:]], axis=0)
    vimn_ref[0] = jnp.concatenate([-(c_re * q[1] + c_im * q[0]) for q in pw[1:]], axis=0)
    lre_ref[0] = pw[tc][0]
    lim_ref[0] = pw[tc][1]


def _s5_prep(lw):
    groups, state = lw['a_re'].shape
    ch = S5_GROUP
    tc = S5_CHUNK
    g3 = lambda x, a, b: x.astype(F32).reshape(groups, a, b)
    spec = lambda a, b: pl.BlockSpec((1, a, b), lambda g: (g, 0, 0))
    outs = pl.pallas_call(
        functools.partial(_s5_prep_kernel, tc=tc),
        grid=(groups,),
        in_specs=[spec(1, state), spec(1, state), spec(1, 1), spec(ch, state), spec(ch, state),
                  spec(ch, state), spec(ch, state), spec(1, ch)],
        out_specs=[spec(tc * ch, ch), spec(tc * ch, state), spec(tc * ch, state), spec(tc * ch, state),
                   spec(tc * ch, state), spec(1, state), spec(1, state)],
        out_shape=[jax.ShapeDtypeStruct((groups, tc * ch, ch), F32)]
        + [jax.ShapeDtypeStruct((groups, tc * ch, state), F32)] * 4
        + [jax.ShapeDtypeStruct((groups, 1, state), F32)] * 2,
        compiler_params=_params("parallel"),
        name="s5_prep",
    )(g3(lw['a_re'], 1, state), g3(lw['a_im'], 1, state), g3(lw['log_dt'], 1, 1),
      jnp.swapaxes(lw['b_re'], 1, 2).astype(F32), jnp.swapaxes(lw['b_im'], 1, 2).astype(F32),
      lw['c_re'].astype(F32), lw['c_im'].astype(F32), g3(lw['d'], 1, ch))
    kst, w_re, w_im, v_re, v_imn, l_re, l_im = outs
    k4 = kst.reshape(groups, tc, ch, ch)
    tt = jnp.arange(tc)
    tau = tt[:, None] - tt[None, :]
    blocks = jnp.where((tau >= 0)[None, :, :, None, None], k4[:, jnp.clip(tau, 0, tc - 1)], 0.0)
    toep = jnp.transpose(blocks, (0, 1, 3, 2, 4)).reshape(groups, tc * ch, tc * ch)
    return dict(
        toep=toep.astype(BF16),
        w=jnp.concatenate([w_re, w_im, w_im, w_re], axis=-1).astype(BF16),
        vt=jnp.concatenate([v_re, v_imn], axis=-1).astype(BF16),
        a16=jnp.concatenate([l_re, l_re], axis=-1),
        b16=jnp.concatenate([-l_im, l_im], axis=-1),
    )


def _lane_block_transpose(arrs, masks):
    n = len(arrs)
    rolled = [[a if j == 0 else pltpu.roll(a, S5_GROUP * j, axis=1) for j in range(n)] for a in arrs]
    outs = []
    for j in range(n):
        out = rolled[0][(0 - j) % n]
        for i in range(1, n):
            out = jnp.where(masks[i], rolled[i][(i - j) % n], out)
        outs.append(out)
    return outs


def _s5_kernel(x_ref, toep_ref, w_ref, vt_ref, a_ref, b_ref, h0_ref, h0s_ref,
               y_ref, hout_ref, x_st, s_st, ex_ref, es_ref, hs_ref, *, nct):
    ci = pl.program_id(2)
    sub = S5_SUB
    halves = S5_CHUNK // sub
    width = a_ref.shape[-1]

    @pl.when(ci == 0)
    def _():
        x_st[...] = h0_ref[0, 0]
        s_st[...] = h0s_ref[0, 0]

    lane = lax.broadcasted_iota(jnp.int32, (nct, LANES), 1)
    masks = [(lane >= S5_GROUP * j) & (lane < S5_GROUP * (j + 1)) for j in range(sub)]

    rows_of = lambda t: pl.ds(t, nct, stride=S5_CHUNK)
    u_halves = [_lane_block_transpose([x_ref[0, rows_of(hf * sub + tt), :] for tt in range(sub)], masks)
                for hf in range(halves)]
    us = [jnp.concatenate([u_halves[hf][g] for hf in range(halves)], axis=1).astype(BF16) for g in range(sub)]

    for g in range(sub):
        e = _mm(us[g], w_ref[g])
        ex_ref[pl.ds(g, nct, stride=sub), :] = e[:, :width]
        es_ref[pl.ds(g, nct, stride=sub), :] = e[:, width:]

    a = a_ref[0]
    b = b_ref[0]
    steps = min(8, nct)

    def body(i, carry):
        x, s = carry
        for j in range(steps):
            rows = pl.ds(pl.multiple_of((i * steps + j) * sub, sub), sub)
            hs_ref[rows, :] = x
            x, s = a * x + b * s + ex_ref[rows, :], a * s - b * x + es_ref[rows, :]
        return x, s

    x, s = lax.fori_loop(0, nct // steps, body, (x_st[...], s_st[...]))
    x_st[...] = x
    s_st[...] = s
    hout_ref[0, 0] = x

    ys = [_mm(us[g], toep_ref[g], NT) + _mm(hs_ref[pl.ds(g, nct, stride=sub), :], vt_ref[g], NT)
          for g in range(sub)]
    for hf in range(halves):
        outs = _lane_block_transpose([ys[g][:, hf * LANES:(hf + 1) * LANES] for g in range(sub)], masks)
        for tt in range(sub):
            y_ref[0, rows_of(hf * sub + tt), :] = outs[tt]


def _s5(p3, col0, h_re, h_im, ops):
    bsz, seq, n_in = p3.shape
    groups, state = h_re.shape[1:]
    ch, tc, sub = S5_GROUP, S5_CHUNK, S5_SUB
    assert seq % tc == 0 and groups % sub == 0 and col0 % LANES == 0 and tc % sub == 0
    n_chunks = seq // tc
    nct = min(S5_TILE_CHUNKS, n_chunks)
    assert n_chunks % nct == 0 and nct % min(8, nct) == 0
    gbs = groups // sub
    lane0 = col0 // LANES
    pack = lambda x, y: jnp.concatenate([x, y], axis=-1).astype(F32).reshape(bsz, gbs, sub, 2 * state)
    per_g = lambda last: pl.BlockSpec((sub, tc * ch, last), lambda gb, b, ci: (gb, 0, 0))
    vec = pl.BlockSpec((1, sub, 2 * state), lambda gb, b, ci: (gb, 0, 0))
    st = pl.BlockSpec((1, 1, sub, 2 * state), lambda gb, b, ci: (b, gb, 0, 0))
    y, h_out = pl.pallas_call(
        functools.partial(_s5_kernel, nct=nct),
        grid=(gbs, bsz, n_chunks // nct),
        in_specs=[pl.BlockSpec((1, nct * tc, LANES), lambda gb, b, ci: (b, ci, lane0 + gb)),
                  per_g(tc * ch), per_g(4 * state), per_g(2 * state), vec, vec, st, st],
        out_specs=[pl.BlockSpec((1, nct * tc, LANES), lambda gb, b, ci: (b, ci, gb)), st],
        out_shape=[jax.ShapeDtypeStruct((bsz, seq, groups * ch), F32),
                   jax.ShapeDtypeStruct((bsz, gbs, sub, 2 * state), F32)],
        scratch_shapes=[pltpu.VMEM((sub, 2 * state), F32), pltpu.VMEM((sub, 2 * state), F32),
                        pltpu.VMEM((nct * sub, 2 * state), F32), pltpu.VMEM((nct * sub, 2 * state), F32),
                        pltpu.VMEM((nct * sub, 2 * state), F32)],
        compiler_params=_params("parallel", "parallel", "arbitrary"),
        name="s5",
    )(p3, ops['toep'], ops['w'], ops['vt'],
      ops['a16'].reshape(gbs, sub, 2 * state), ops['b16'].reshape(gbs, sub, 2 * state),
      pack(h_re, h_im), pack(h_im, h_re))
    h_out = h_out.reshape(bsz, groups, 2 * state)
    return y, h_out[..., :state], h_out[..., state:]


def _gelu_tanh(x):
    return 0.5 * x * (1.0 + jnp.tanh(math.sqrt(2.0 / math.pi) * (x + 0.044715 * (x * x * x))))


def _mix_kernel(x_ref, rw_ref, y_ref, gw_ref, gb_ref, wo1_ref, wo2_ref, g_ref, b_ref, o_ref, *, alpha):
    y = _gelu_tanh(y_ref[...])
    s5o = y * _sigmoid(_mm(y, gw_ref[...]) + gb_ref[...])
    mix = _mm(rw_ref[...], wo1_ref[...]) + _mm(s5o, wo2_ref[...])
    o_ref[...] = _ln(alpha * x_ref[...] + mix, g_ref[...], b_ref[...])


def _mix(x, rw, y, glu_w, glu_b, w_out, g, b, alpha):
    n, d = x.shape
    r_w = rw.shape[1]
    s_w = y.shape[1]
    tm = min(_token_tile(n), 256)
    tile = lambda w: pl.BlockSpec((tm, w), lambda i: (i, 0))
    full = lambda a, bb: pl.BlockSpec((a, bb), lambda i: (0, 0))
    return pl.pallas_call(
        functools.partial(_mix_kernel, alpha=alpha),
        grid=(n // tm,),
        in_specs=[tile(d), tile(r_w), tile(s_w), full(s_w, s_w), full(1, s_w), full(r_w, d), full(s_w, d),
                  full(1, d), full(1, d)],
        out_specs=tile(d),
        out_shape=jax.ShapeDtypeStruct((n, d), F32),
        compiler_params=_params("parallel"),
        name="mix",
    )(x, rw, y, glu_w.astype(BF16), glu_b.reshape(1, -1).astype(F32), w_out[:r_w].astype(BF16),
      w_out[r_w:].astype(BF16), g, b)


def _attn_kernel(x_ref, mk_ref, mv_ref, wq_ref, wo_ref, g_ref, b_ref, o_ref, *, alpha, heads):
    x = x_ref[...]
    d = x.shape[-1]
    hd = d // heads
    q = _mm(x, wq_ref[...]) * (hd ** -0.5)
    mk = mk_ref[0]
    mv = mv_ref[0]
    outs = []
    for h in range(heads):
        sl = slice(h * hd, (h + 1) * hd)
        s = _mm(q[:, sl], mk[:, sl], NT)
        e = jnp.exp(s - jnp.max(s, axis=-1, keepdims=True))
        pr = e / jnp.sum(e, axis=-1, keepdims=True)
        outs.append(_mm(pr, mv[:, sl]))
    o = jnp.concatenate(outs, axis=-1)
    o_ref[...] = _ln(alpha * x + _mm(o, wo_ref[...]), g_ref[...], b_ref[...])


def _attn(x, mk, mv, wq, wo, g, b, alpha, seq):
    n, d = x.shape
    n_mem = mk.shape[1]
    tm = min(_token_tile(n), _token_tile(seq), 256)
    per_batch = seq // tm
    tile = pl.BlockSpec((tm, d), lambda i: (i, 0))
    mem = pl.BlockSpec((1, n_mem, d), lambda i: (i // per_batch, 0, 0))
    full = lambda a, bb: pl.BlockSpec((a, bb), lambda i: (0, 0))
    return pl.pallas_call(
        functools.partial(_attn_kernel, alpha=alpha, heads=XATTN_HEADS),
        grid=(n // tm,),
        in_specs=[tile, mem, mem, full(d, d), full(d, d), full(1, d), full(1, d)],
        out_specs=tile,
        out_shape=jax.ShapeDtypeStruct((n, d), F32),
        compiler_params=_params("parallel"),
        name="attn",
    )(x, mk.astype(BF16), mv.astype(BF16), wq, wo, g, b)


def _layer(x, mk, mv, shift_prev, s_rwkv, h_re, h_im, lw, s5_ops, alpha):
    bsz, seq, d = x.shape
    n = bsz * seq
    ln = lambda i: (lw['ln_g'][i].reshape(1, d).astype(F32), lw['ln_b'][i].reshape(1, d).astype(F32))
    cols = shift_prev.shape[-1]
    x1 = _ffn_ln(x.reshape(n, d), lw['f1g'], lw['f1u'], lw['f1d'], *ln(0), alpha)
    p = _proj(x1, lw['w_in']).reshape(bsz, seq, -1)
    rw, s_new = _rwkv(p, shift_prev, s_rwkv, lw)
    y, hr, hi = _s5(p, cols, h_re, h_im, s5_ops)
    x2 = _mix(x1, rw.reshape(n, -1), y.reshape(n, -1), lw['glu_w'], lw['glu_b'], lw['w_out'], *ln(1), alpha)
    x3 = _attn(x2, mk.reshape(bsz, -1, d), mv.reshape(bsz, -1, d), lw['xq'], lw['xo'], *ln(2), alpha, seq)
    x4 = _ffn_ln(x3, lw['f2g'], lw['f2u'], lw['f2d'], *ln(3), alpha)
    return x4.reshape(bsz, seq, d), p[:, -1:, :cols], s_new, hr, hi


def kernel(x_prompt, x_sample, mem_prompt, cache_mem_k, cache_mem_v, state_rwkv, cache_shift, state_s5_re, state_s5_im, ln_g, ln_b, ffn1_gate, ffn1_up, ffn1_down, w_in, shift_mu, rwkv_w0, rwkv_w_up, rwkv_a0, rwkv_a_up, rwkv_g_up, rwkv_k_k, rwkv_k_a, rwkv_r_k, rwkv_gn_w, rwkv_gn_b, s5_a_re, s5_a_im, s5_log_dt, s5_b_re, s5_b_im, s5_c_re, s5_c_im, s5_d, s5_glu_w, s5_glu_b, w_mix_out, xattn_q, xattn_k, xattn_v, xattn_o, ffn2_gate, ffn2_up, ffn2_down):
    depth = ln_g.shape[0]
    bp, _, d = x_prompt.shape
    n_mem = mem_prompt.shape[1]
    heads, hd = state_rwkv.shape[2], state_rwkv.shape[3]
    cols = cache_shift.shape[-1]
    groups, state = state_s5_re.shape[2:]
    alpha = (2.0 * depth) ** 0.25
    bf = lambda w: w.astype(BF16)
    xp, xs = x_prompt, x_sample
    outs = [[] for _ in range(10)]
    for l in range(depth):
        lw = dict(ln_g=ln_g[l], ln_b=ln_b[l], f1g=bf(ffn1_gate[l]), f1u=bf(ffn1_up[l]), f1d=bf(ffn1_down[l]),
                  w_in=bf(w_in[l]), shift_mu=shift_mu[l], w0=rwkv_w0[l], w_up=rwkv_w_up[l], a0=rwkv_a0[l],
                  a_up=rwkv_a_up[l], g_up=rwkv_g_up[l], k_k=rwkv_k_k[l], k_a=rwkv_k_a[l], r_k=rwkv_r_k[l],
                  gn_w=rwkv_gn_w[l], gn_b=rwkv_gn_b[l], a_re=s5_a_re[l], a_im=s5_a_im[l], log_dt=s5_log_dt[l],
                  b_re=s5_b_re[l], b_im=s5_b_im[l], c_re=s5_c_re[l], c_im=s5_c_im[l], d=s5_d[l],
                  glu_w=s5_glu_w[l], glu_b=s5_glu_b[l], w_out=w_mix_out[l], xq=bf(xattn_q[l]), xo=bf(xattn_o[l]),
                  f2g=bf(ffn2_gate[l]), f2u=bf(ffn2_up[l]), f2d=bf(ffn2_down[l]))
        s5_ops = _s5_prep(lw)
        mem2 = mem_prompt.reshape(bp * n_mem, d)
        mk_p = _proj(mem2, bf(xattn_k[l])).reshape(bp, n_mem, XATTN_HEADS, d // XATTN_HEADS)
        mv_p = _proj(mem2, bf(xattn_v[l])).reshape(bp, n_mem, XATTN_HEADS, d // XATTN_HEADS)
        xp, sh_p, rw_p, hr_p, hi_p = _layer(
            xp, mk_p, mv_p, jnp.zeros((bp, 1, cols), F32), jnp.zeros((bp, heads, hd, hd), F32),
            jnp.zeros((bp, groups, state), F32), jnp.zeros((bp, groups, state), F32), lw, s5_ops, alpha)
        xs, sh_s, rw_s, hr_s, hi_s = _layer(
            xs, cache_mem_k[l], cache_mem_v[l], cache_shift[l], state_rwkv[l],
            state_s5_re[l], state_s5_im[l], lw, s5_ops, alpha)
        for acc, val in zip(outs, (mk_p, mv_p, rw_p, sh_p, hr_p, hi_p, rw_s, sh_s, hr_s, hi_s)):
            acc.append(val)
    return (xp, xs) + tuple(jnp.stack(o) for o in outs)
```

```python
import functools
import math

import jax
import jax.numpy as jnp
from jax import lax
from jax.experimental import pallas as pl
from jax.experimental.pallas import tpu as pltpu

F32 = jnp.float32
BF16 = jnp.bfloat16

LN_EPS = 1e-5
GN_EPS = 64e-5
RWKV_HEAD = 64
RANK_W = 64
RANK_A = 64
RANK_G = 128
S5_GROUP = 16
S5_STATE = 64
S5_CHUNK = 16
RWKV_CHUNK = 64
RWKV_SUBCHUNKS = 4
XATTN_HEADS = 4
LANES = 128
MXU_TILE = 256
S5_SUB = LANES // S5_GROUP
S5_TILE_CHUNKS = 256
VMEM_LIMIT = 56 * 1024 * 1024
ROW_BLOCK = 256
EPILOGUE_SPLIT = 2

NT = (((1,), (1,)), ((), ()))
TN = (((0,), (0,)), ((), ()))


def _mm(a, b, dims=None):
    a = a.astype(BF16)
    b = b.astype(BF16)
    if dims is None:
        return jnp.dot(a, b, preferred_element_type=F32)
    return lax.dot_general(a, b, dims, preferred_element_type=F32)


def _mm_f32(a, b, dims=None):
    if dims is None:
        return jnp.dot(a, b, preferred_element_type=F32, precision=lax.Precision.HIGHEST)
    return lax.dot_general(a, b, dims, preferred_element_type=F32, precision=lax.Precision.HIGHEST)


def _split_bf16(x, parts):
    pieces = []
    for _ in range(parts):
        piece = x.astype(BF16)
        pieces.append(piece)
        x = x - piece.astype(F32)
    return pieces


def _ln(z, g, b):
    mu = jnp.mean(z, axis=-1, keepdims=True)
    d = z - mu
    var = jnp.mean(d * d, axis=-1, keepdims=True)
    return d * lax.rsqrt(var + LN_EPS) * g + b


def _sigmoid(x):
    return 1.0 / (1.0 + jnp.exp(-x))


def _params(*sem):
    return pltpu.CompilerParams(dimension_semantics=sem, vmem_limit_bytes=VMEM_LIMIT)


def _token_tile(n):
    for t in (512, 256, 128, 64, 32, 16, 8):
        if n % t == 0:
            return t
    raise ValueError(f"token count {n} is not a multiple of 8")


def _ffn_ln_kernel(x_ref, wg_ref, wu_ref, wd_ref, g_ref, b_ref, o_ref, xb_ref, acc_ref, *, alpha):
    j = pl.program_id(1)

    @pl.when(j == 0)
    def _():
        xb_ref[...] = x_ref[...].astype(BF16)
        acc_ref[...] = jnp.zeros_like(acc_ref)

    xb = xb_ref[...]
    gate = jnp.dot(xb, wg_ref[...], preferred_element_type=F32)
    up = jnp.dot(xb, wu_ref[...], preferred_element_type=F32)
    h = (gate * _sigmoid(gate) * up).astype(BF16)
    last = pl.num_programs(1) - 1

    @pl.when(j < last)
    def _():
        acc_ref[...] += jnp.dot(h, wd_ref[...], preferred_element_type=F32)

    @pl.when(j == last)
    def _():
        tm = h.shape[0]
        rb = tm // EPILOGUE_SPLIT if tm % (EPILOGUE_SPLIT * 8) == 0 else tm
        for r in range(0, tm, rb):
            ffn = acc_ref[r:r + rb, :] + jnp.dot(h[r:r + rb, :], wd_ref[...], preferred_element_type=F32)
            o_ref[r:r + rb, :] = _ln(alpha * x_ref[r:r + rb, :] + 0.5 * ffn, g_ref[...], b_ref[...])


def _ffn_ln(x, wg, wu, wd, g, b, alpha):
    n, d = x.shape
    dff = wg.shape[1]
    tm = _token_tile(n)
    tf = 512 if dff % 512 == 0 else dff
    return pl.pallas_call(
        functools.partial(_ffn_ln_kernel, alpha=alpha),
        grid=(n // tm, dff // tf),
        in_specs=[
            pl.BlockSpec((tm, d), lambda i, j: (i, 0)),
            pl.BlockSpec((d, tf), lambda i, j: (0, j)),
            pl.BlockSpec((d, tf), lambda i, j: (0, j)),
            pl.BlockSpec((tf, d), lambda i, j: (j, 0)),
            pl.BlockSpec((1, d), lambda i, j: (0, 0)),
            pl.BlockSpec((1, d), lambda i, j: (0, 0)),
        ],
        out_specs=pl.BlockSpec((tm, d), lambda i, j: (i, 0)),
        out_shape=jax.ShapeDtypeStruct((n, d), F32),
        scratch_shapes=[pltpu.VMEM((tm, d), BF16), pltpu.VMEM((tm, d), F32)],
        compiler_params=_params("parallel", "arbitrary"),
        name="ffn_ln",
    )(x, wg, wu, wd, g, b)


def _proj_kernel(x_ref, w_ref, o_ref):
    o_ref[...] = _mm(x_ref[...], w_ref[...])


def _column_tile(nout, cap=2304):
    best = None
    for t in range(LANES, min(nout, cap) + 1, LANES):
        if nout % t == 0:
            best = t
    return best if best is not None else nout


def _proj(x, w):
    n, d = x.shape
    nout = w.shape[1]
    tm = _token_tile(n)
    tn = _column_tile(nout)
    return pl.pallas_call(
        _proj_kernel,
        grid=(nout // tn, n // tm),
        in_specs=[
            pl.BlockSpec((tm, d), lambda j, i: (i, 0)),
            pl.BlockSpec((d, tn), lambda j, i: (0, j)),
        ],
        out_specs=pl.BlockSpec((tm, tn), lambda j, i: (i, j)),
        out_shape=jax.ShapeDtypeStruct((n, nout), F32),
        compiler_params=_params("parallel", "parallel"),
        name="proj",
    )(x, w)


def _rwkv_kernel(p_ref, shift_ref, s0_ref, mu_ref, w0_ref, wup_ref, a0_ref, aup_ref, gup_ref,
                 kk_ref, ka_ref, rk_ref, gnw_ref, gnb_ref,
                 o_ref, sout_ref, prev_ref, state_ref, *, t, n_sub, heads):
    c = pl.program_id(1)
    r_w = heads * RWKV_HEAD
    gw = min(MXU_TILE, r_w)
    per = gw // RWKV_HEAD
    mw = per * t
    head_shift = RWKV_HEAD.bit_length() - 1
    bf = lambda x: x.astype(BF16)

    @pl.when(c == 0)
    def _():
        prev_ref[...] = shift_ref[0]
        state_ref[...] = s0_ref[0]

    same_head = ((lax.broadcasted_iota(jnp.int32, (gw, gw), 0) >> head_shift)
                 == (lax.broadcasted_iota(jnp.int32, (gw, gw), 1) >> head_shift))
    ones_bd = jnp.where(same_head, 1.0, 0.0).astype(BF16)

    def head_sum(x):
        return jnp.concatenate(
            [sum(jnp.dot(piece, ones_bd, preferred_element_type=F32) for piece in _split_bf16(x[:, q:q + gw], 2))
             for q in range(0, r_w, gw)], axis=1)

    tri = jnp.where(lax.broadcasted_iota(jnp.int32, (t, t), 0) >= lax.broadcasted_iota(jnp.int32, (t, t), 1),
                    1.0, 0.0).astype(BF16)

    def prologue(p, prev_row):
        row = lax.broadcasted_iota(jnp.int32, p.shape, 0)
        prev = jnp.where(row == 0, prev_row, pltpu.roll(p, 1, axis=0))
        ps = p + mu_ref[...] * (prev - p)
        r = ps[:, 0:r_w]
        k = ps[:, r_w:2 * r_w]
        v = ps[:, 2 * r_w:3 * r_w]
        wa_in = ps[:, 3 * r_w:3 * r_w + RANK_W + RANK_A]
        lg = ps[:, 3 * r_w + RANK_W + RANK_A:]
        log_w = -math.exp(-0.5) * _sigmoid(w0_ref[...] + _mm(jnp.tanh(wa_in), wup_ref[...]))
        a = _sigmoid(a0_ref[...] + _mm(wa_in, aup_ref[...]))
        g = _mm(_sigmoid(lg), gup_ref[...])
        kk = k * kk_ref[...]
        kk = kk * lax.rsqrt(jnp.maximum(head_sum(kk * kk), 1e-24))
        k2 = k * (1.0 + (a - 1.0) * ka_ref[...])
        b = kk * a
        cum = sum(jnp.dot(tri, piece, preferred_element_type=F32) for piece in _split_bf16(log_w, 3))
        e_cum = jnp.exp(cum)
        e_neg = jnp.exp(-cum)
        e_tail = jnp.exp(cum[t - 1:t, :] - cum)
        return dict(kq=kk * jnp.exp(cum - log_w), rq=r * e_cum, bd=b * e_neg, kd=k2 * e_neg, bt=b * e_tail,
                    kt=k2 * e_tail, v=v, p_last=e_cum[t - 1:t, :], bonus=head_sum(r * k2 * rk_ref[...]) * v, g=g)

    pro = []
    prev_row = prev_ref[...]
    for j in range(n_sub):
        p = p_ref[0, j * t:(j + 1) * t, :]
        pro.append(prologue(p, prev_row))
        prev_row = p[t - 1:t, :]
    prev_ref[...] = prev_row

    def block_rows(x, bw):
        xb = bf(x)
        blk = lax.broadcasted_iota(jnp.int32, xb.shape, 1) >> (bw.bit_length() - 1)
        return jnp.concatenate([jnp.where(blk == h, xb, jnp.zeros_like(xb)) for h in range(per)], axis=0)

    lane_blk = lax.broadcasted_iota(jnp.int32, (RWKV_HEAD, gw), 1) >> head_shift

    def diag_blocks(z):
        return sum(jnp.where(lane_blk == h, z[h * RWKV_HEAD:(h + 1) * RWKV_HEAD, :], 0.0) for h in range(per))

    quads = range(0, r_w, gw)
    probs = [(j, q) for j in range(n_sub) for q in quads]
    ps_ = range(len(probs))
    part = lambda name: [pro[j][name][:, q:q + gw] for j, q in probs]
    kq, rq, bd, kd, bt, kt, v = map(part, ("kq", "rq", "bd", "kd", "bt", "kt", "v"))
    row_m = lax.broadcasted_iota(jnp.int32, (t, mw), 0)
    col_m = lax.broadcasted_iota(jnp.int32, (t, mw), 1) & (t - 1)
    strict = row_m > col_m
    incl = row_m >= col_m

    vbd = [block_rows(v[i], RWKV_HEAD) for i in ps_]
    ms = [_mm(jnp.concatenate([kq[i], rq[i]], axis=0),
              jnp.concatenate([block_rows(bd[i], RWKV_HEAD), block_rows(kd[i], RWKV_HEAD)], axis=0), NT) for i in ps_]
    m_b = [jnp.where(strict, ms[i][:t, :mw], 0.0) for i in ps_]
    mkv = [_mm(jnp.where(strict, ms[i][:t, mw:], 0.0), vbd[i]) for i in ps_]
    l_b = [bf(jnp.where(incl, ms[i][t:, :mw], 0.0)) for i in ps_]
    l_k = [bf(jnp.where(incl, ms[i][t:, mw:], 0.0)) for i in ps_]
    invs = None
    k = 1
    while k < t:
        shift = (2 * k).bit_length() - 1
        join = ((row_m >> shift) == (col_m >> shift)) & ((row_m & k) != 0) & ((col_m & k) == 0)
        cs = [jnp.where(join, m, 0.0) for m in m_b]
        if k == 1:
            invs = [jnp.where(row_m == col_m, 1.0, 0.0) - c for c in cs]
        else:
            xs = [_mm(cs[i], block_rows(invs[i], t)) for i in ps_]
            invs = [invs[i] - _mm(invs[i], block_rows(xs[i], t)) for i in ps_]
        k *= 2
    invs = [bf(x) for x in invs]
    gk = [_mm(invs[i], block_rows(kq[i], RWKV_HEAD)) for i in ps_]
    u0 = [-_mm(invs[i], block_rows(mkv[i], RWKV_HEAD)) for i in ps_]
    rp = [bf(rq[i] - _mm(l_b[i], block_rows(gk[i], RWKV_HEAD))) for i in ps_]
    y0 = [_mm(jnp.concatenate([l_b[i], l_k[i]], axis=1),
              jnp.concatenate([block_rows(u0[i], RWKV_HEAD), vbd[i]], axis=0)) for i in ps_]
    phi = [block_rows(diag_blocks(_mm(gk[i], bt[i], TN)), RWKV_HEAD) for i in ps_]
    psi = [diag_blocks(_mm(jnp.concatenate([u0[i], v[i]], axis=0), jnp.concatenate([bt[i], kt[i]], axis=0), TN))
           for i in ps_]

    states = [state_ref[:, q:q + gw] for q in quads]
    n_q = len(states)
    inv_hd = 1.0 / RWKV_HEAD
    for j in range(n_sub):
        ys = []
        for qi in range(n_q):
            i = j * n_q + qi
            s = states[qi]
            ys.append(y0[i] + _mm(rp[i], block_rows(s, RWKV_HEAD), NT))
            states[qi] = s * pro[j]["p_last"][:, quads[qi]:quads[qi] + gw] - _mm(s, phi[i]) + psi[i]
        y = jnp.concatenate(ys, axis=1)
        dy = y - head_sum(y) * inv_hd
        var = head_sum(dy * dy) * inv_hd
        yn = dy * lax.rsqrt(var + GN_EPS) * gnw_ref[...] + gnb_ref[...]
        o_ref[0, j * t:(j + 1) * t, :] = (yn + pro[j]["bonus"]) * pro[j]["g"]
    for qi, q in enumerate(quads):
        state_ref[:, q:q + gw] = states[qi]

    @pl.when(c == pl.num_programs(1) - 1)
    def _():
        sout_ref[0] = state_ref[...]


def _rwkv(p3, shift_prev, s0, lw):
    bsz, seq, _ = p3.shape
    heads = s0.shape[1]
    r_w = heads * RWKV_HEAD
    cols = 3 * r_w + RANK_W + RANK_A + RANK_G
    t = min(RWKV_CHUNK, seq)
    assert seq % t == 0 and t & (t - 1) == 0
    n_sub = RWKV_SUBCHUNKS if seq % (RWKV_SUBCHUNKS * t) == 0 else 1
    tt = n_sub * t
    row = lambda x: x.reshape(1, -1).astype(F32)
    zeros = jnp.zeros((RANK_W, r_w), F32)
    wup = jnp.concatenate([lw['w_up'], zeros], axis=0).astype(BF16)
    aup = jnp.concatenate([zeros, lw['a_up']], axis=0).astype(BF16)
    vec = lambda n: pl.BlockSpec((1, n), lambda b, c: (0, 0))
    full = lambda a, bb: pl.BlockSpec((a, bb), lambda b, c: (0, 0))
    lanes_hk = lambda s: jnp.swapaxes(s.astype(F32), 1, 2).reshape(bsz, RWKV_HEAD, r_w)
    out, s_new = pl.pallas_call(
        functools.partial(_rwkv_kernel, t=t, n_sub=n_sub, heads=heads),
        grid=(bsz, seq // tt),
        in_specs=[
            pl.BlockSpec((1, tt, cols), lambda b, c: (b, c, 0)),
            pl.BlockSpec((1, 1, cols), lambda b, c: (b, 0, 0)),
            pl.BlockSpec((1, RWKV_HEAD, r_w), lambda b, c: (b, 0, 0)),
            vec(cols), vec(r_w), full(RANK_W + RANK_A, r_w), vec(r_w), full(RANK_W + RANK_A, r_w),
            full(RANK_G, r_w), vec(r_w), vec(r_w), vec(r_w), vec(r_w), vec(r_w),
        ],
        out_specs=[
            pl.BlockSpec((1, tt, r_w), lambda b, c: (b, c, 0)),
            pl.BlockSpec((1, RWKV_HEAD, r_w), lambda b, c: (b, 0, 0)),
        ],
        out_shape=[
            jax.ShapeDtypeStruct((bsz, seq, r_w), F32),
            jax.ShapeDtypeStruct((bsz, RWKV_HEAD, r_w), F32),
        ],
        scratch_shapes=[pltpu.VMEM((1, cols), F32), pltpu.VMEM((RWKV_HEAD, r_w), F32)],
        compiler_params=_params("parallel", "arbitrary"),
        name="rwkv",
    )(p3, shift_prev.astype(F32), lanes_hk(s0), row(lw['shift_mu']), row(lw['w0']), wup, row(lw['a0']), aup,
      lw['g_up'].astype(BF16), row(lw['k_k']), row(lw['k_a']), row(lw['r_k']), row(lw['gn_w']),
      row(lw['gn_b']))
    return out, jnp.swapaxes(s_new.reshape(bsz, RWKV_HEAD, heads, RWKV_HEAD), 1, 2)


def _s5_prep_kernel(are_ref, aim_ref, ldt_ref, btre_ref, btim_ref, cre_ref, cim_ref, d_ref,
                    kst_ref, wre_ref, wim_ref, vre_ref, vimn_ref, lre_ref, lim_ref, *, tc):
    a_re = are_ref[0]
    a_im = aim_ref[0]
    dt = jnp.exp(ldt_ref[0])
    mag = jnp.exp(a_re * dt)
    l_re = mag * jnp.cos(a_im * dt)
    l_im = mag * jnp.sin(a_im * dt)
    den = a_re * a_re + a_im * a_im
    x_re = l_re - 1.0
    co_re = (x_re * a_re + l_im * a_im) / den
    co_im = (l_im * a_re - x_re * a_im) / den
    pw = [(jnp.ones_like(l_re), jnp.zeros_like(l_re))]
    for _ in range(tc):
        q_re, q_im = pw[-1]
        pw.append((q_re * l_re - q_im * l_im, q_re * l_im + q_im * l_re))

    c_re = cre_ref[0]
    c_im = cim_ref[0]
    bt_re = btre_ref[0]
    bt_im = btim_ref[0]
    cc_re = c_re * co_re - c_im * co_im
    cc_im = c_re * co_im + c_im * co_re
    cl_re = jnp.concatenate([cc_re * q[0] - cc_im * q[1] for q in pw[:tc]], axis=0)
    cl_im = jnp.concatenate([cc_re * q[1] + cc_im * q[0] for q in pw[:tc]], axis=0)
    kst = _mm_f32(cl_re, bt_re, NT) - _mm_f32(cl_im, bt_im, NT)
    rr = lax.broadcasted_iota(jnp.int32, kst.shape, 0)
    cc = lax.broadcasted_iota(jnp.int32, kst.shape, 1)
    kst_ref[0] = kst + jnp.where(rr == cc, d_ref[0], 0.0)

    w_re, w_im = [], []
    for s in range(tc):
        q_re, q_im = pw[tc - 1 - s]
        f_re = q_re * co_re - q_im * co_im
        f_im = q_re * co_im + q_im * co_re
        w_re.append(bt_re * f_re - bt_im * f_im)
        w_im.append(bt_re * f_im + bt_im * f_re)
    wre_ref[0] = jnp.concatenate(w_re, axis=0)
    wim_ref[0] = jnp.concatenate(w_im, axis=0)
    vre_ref[0] = jnp.concatenate([c_re * q[0] - c_im * q[1] for q in pw[1:]], axis=0)
    vimn_ref[0] = jnp.concatenate([-(c_re * q[1] + c_im * q[0]) for q in pw[1:]], axis=0)
    lre_ref[0] = pw[tc][0]
    lim_ref[0] = pw[tc][1]


def _s5_prep(lw):
    groups, state = lw['a_re'].shape
    ch = S5_GROUP
    tc = S5_CHUNK
    g3 = lambda x, a, b: x.astype(F32).reshape(groups, a, b)
    spec = lambda a, b: pl.BlockSpec((1, a, b), lambda g: (g, 0, 0))
    outs = pl.pallas_call(
        functools.partial(_s5_prep_kernel, tc=tc),
        grid=(groups,),
        in_specs=[spec(1, state), spec(1, state), spec(1, 1), spec(ch, state), spec(ch, state),
                  spec(ch, state), spec(ch, state), spec(1, ch)],
        out_specs=[spec(tc * ch, ch), spec(tc * ch, state), spec(tc * ch, state), spec(tc * ch, state),
                   spec(tc * ch, state), spec(1, state), spec(1, state)],
        out_shape=[jax.ShapeDtypeStruct((groups, tc * ch, ch), F32)]
        + [jax.ShapeDtypeStruct((groups, tc * ch, state), F32)] * 4
        + [jax.ShapeDtypeStruct((groups, 1, state), F32)] * 2,
        compiler_params=_params("parallel"),
        name="s5_prep",
    )(g3(lw['a_re'], 1, state), g3(lw['a_im'], 1, state), g3(lw['log_dt'], 1, 1),
      jnp.swapaxes(lw['b_re'], 1, 2).astype(F32), jnp.swapaxes(lw['b_im'], 1, 2).astype(F32),
      lw['c_re'].astype(F32), lw['c_im'].astype(F32), g3(lw['d'], 1, ch))
    kst, w_re, w_im, v_re, v_imn, l_re, l_im = outs
    k4 = kst.reshape(groups, tc, ch, ch)
    tt = jnp.arange(tc)
    tau = tt[:, None] - tt[None, :]
    blocks = jnp.where((tau >= 0)[None, :, :, None, None], k4[:, jnp.clip(tau, 0, tc - 1)], 0.0)
    toep = jnp.transpose(blocks, (0, 1, 3, 2, 4)).reshape(groups, tc * ch, tc * ch)
    return dict(
        toep=toep.astype(BF16),
        w=jnp.concatenate([w_re, w_im, w_im, w_re], axis=-1).astype(BF16),
        vt=jnp.concatenate([v_re, v_imn], axis=-1).astype(BF16),
        a16=jnp.concatenate([l_re, l_re], axis=-1),
        b16=jnp.concatenate([-l_im, l_im], axis=-1),
    )


def _lane_block_transpose(arrs, masks):
    n = len(arrs)
    rolled = [[a if j == 0 else pltpu.roll(a, S5_GROUP * j, axis=1) for j in range(n)] for a in arrs]
    outs = []
    for j in range(n):
        out = rolled[0][(0 - j) % n]
        for i in range(1, n):
            out = jnp.where(masks[i], rolled[i][(i - j) % n], out)
        outs.append(out)
    return outs


def _s5_kernel(x_ref, toep_ref, w_ref, vt_ref, a_ref, b_ref, h0_ref, h0s_ref,
               y_ref, hout_ref, x_st, s_st, ex_ref, es_ref, hs_ref, *, nct):
    ci = pl.program_id(2)
    sub = S5_SUB
    halves = S5_CHUNK // sub
    width = a_ref.shape[-1]

    @pl.when(ci == 0)
    def _():
        x_st[...] = h0_ref[0, 0]
        s_st[...] = h0s_ref[0, 0]

    lane = lax.broadcasted_iota(jnp.int32, (nct, LANES), 1)
    masks = [(lane >= S5_GROUP * j) & (lane < S5_GROUP * (j + 1)) for j in range(sub)]

    rows_of = lambda t: pl.ds(t, nct, stride=S5_CHUNK)
    u_halves = [_lane_block_transpose([x_ref[0, rows_of(hf * sub + tt), :] for tt in range(sub)], masks)
                for hf in range(halves)]
    us = [jnp.concatenate([u_halves[hf][g] for hf in range(halves)], axis=1).astype(BF16) for g in range(sub)]

    for g in range(sub):
        e = _mm(us[g], w_ref[g])
        ex_ref[pl.ds(g, nct, stride=sub), :] = e[:, :width]
        es_ref[pl.ds(g, nct, stride=sub), :] = e[:, width:]

    a = a_ref[0]
    b = b_ref[0]
    steps = min(8, nct)

    def body(i, carry):
        x, s = carry
        for j in range(steps):
            rows = pl.ds(pl.multiple_of((i * steps + j) * sub, sub), sub)
            hs_ref[rows, :] = x
            x, s = a * x + b * s + ex_ref[rows, :], a * s - b * x + es_ref[rows, :]
        return x, s

    x, s = lax.fori_loop(0, nct // steps, body, (x_st[...], s_st[...]))
    x_st[...] = x
    s_st[...] = s
    hout_ref[0, 0] = x

    ys = [_mm(us[g], toep_ref[g], NT) + _mm(hs_ref[pl.ds(g, nct, stride=sub), :], vt_ref[g], NT)
          for g in range(sub)]
    for hf in range(halves):
        outs = _lane_block_transpose([ys[g][:, hf * LANES:(hf + 1) * LANES] for g in range(sub)], masks)
        for tt in range(sub):
            y_ref[0, rows_of(hf * sub + tt), :] = outs[tt]


def _s5(p3, col0, h_re, h_im, ops):
    bsz, seq, n_in = p3.shape
    groups, state = h_re.shape[1:]
    ch, tc, sub = S5_GROUP, S5_CHUNK, S5_SUB
    assert seq % tc == 0 and groups % sub == 0 and col0 % LANES == 0 and tc % sub == 0
    n_chunks = seq // tc
    nct = min(S5_TILE_CHUNKS, n_chunks)
    assert n_chunks % nct == 0 and nct % min(8, nct) == 0
    gbs = groups // sub
    lane0 = col0 // LANES
    pack = lambda x, y: jnp.concatenate([x, y], axis=-1).astype(F32).reshape(bsz, gbs, sub, 2 * state)
    per_g = lambda last: pl.BlockSpec((sub, tc * ch, last), lambda gb, b, ci: (gb, 0, 0))
    vec = pl.BlockSpec((1, sub, 2 * state), lambda gb, b, ci: (gb, 0, 0))
    st = pl.BlockSpec((1, 1, sub, 2 * state), lambda gb, b, ci: (b, gb, 0, 0))
    y, h_out = pl.pallas_call(
        functools.partial(_s5_kernel, nct=nct),
        grid=(gbs, bsz, n_chunks // nct),
        in_specs=[pl.BlockSpec((1, nct * tc, LANES), lambda gb, b, ci: (b, ci, lane0 + gb)),
                  per_g(tc * ch), per_g(4 * state), per_g(2 * state), vec, vec, st, st],
        out_specs=[pl.BlockSpec((1, nct * tc, LANES), lambda gb, b, ci: (b, ci, gb)), st],
        out_shape=[jax.ShapeDtypeStruct((bsz, seq, groups * ch), F32),
                   jax.ShapeDtypeStruct((bsz, gbs, sub, 2 * state), F32)],
        scratch_shapes=[pltpu.VMEM((sub, 2 * state), F32), pltpu.VMEM((sub, 2 * state), F32),
                        pltpu.VMEM((nct * sub, 2 * state), F32), pltpu.VMEM((nct * sub, 2 * state), F32),
                        pltpu.VMEM((nct * sub, 2 * state), F32)],
        compiler_params=_params("parallel", "parallel", "arbitrary"),
        name="s5",
    )(p3, ops['toep'], ops['w'], ops['vt'],
      ops['a16'].reshape(gbs, sub, 2 * state), ops['b16'].reshape(gbs, sub, 2 * state),
      pack(h_re, h_im), pack(h_im, h_re))
    h_out = h_out.reshape(bsz, groups, 2 * state)
    return y, h_out[..., :state], h_out[..., state:]


def _gelu_tanh(x):
    return 0.5 * x * (1.0 + jnp.tanh(math.sqrt(2.0 / math.pi) * (x + 0.044715 * (x * x * x))))


def _mix_kernel(x_ref, rw_ref, y_ref, gw_ref, gb_ref, wo1_ref, wo2_ref, g_ref, b_ref, o_ref, *, alpha, rb):
    for r in range(0, x_ref.shape[0], rb):
        rows = slice(r, r + rb)
        y = _gelu_tanh(y_ref[rows, :])
        s5o = y * _sigmoid(_mm(y, gw_ref[...]) + gb_ref[...])
        mix = _mm(rw_ref[rows, :], wo1_ref[...]) + _mm(s5o, wo2_ref[...])
        o_ref[rows, :] = _ln(alpha * x_ref[rows, :] + mix, g_ref[...], b_ref[...])


def _mix(x, rw, y, glu_w, glu_b, w_out, g, b, alpha):
    n, d = x.shape
    r_w = rw.shape[1]
    s_w = y.shape[1]
    tm = _token_tile(n)
    tile = lambda w: pl.BlockSpec((tm, w), lambda i: (i, 0))
    full = lambda a, bb: pl.BlockSpec((a, bb), lambda i: (0, 0), pipeline_mode=pl.Buffered(1))
    return pl.pallas_call(
        functools.partial(_mix_kernel, alpha=alpha, rb=min(tm, ROW_BLOCK)),
        grid=(n // tm,),
        in_specs=[tile(d), tile(r_w), tile(s_w), full(s_w, s_w), full(1, s_w), full(r_w, d), full(s_w, d),
                  full(1, d), full(1, d)],
        out_specs=tile(d),
        out_shape=jax.ShapeDtypeStruct((n, d), F32),
        compiler_params=_params("parallel"),
        name="mix",
    )(x, rw, y, glu_w.astype(BF16), glu_b.reshape(1, -1).astype(F32), w_out[:r_w].astype(BF16),
      w_out[r_w:].astype(BF16), g, b)


def _attn_kernel(x_ref, mk_ref, mv_ref, wq_ref, wo_ref, g_ref, b_ref, o_ref, *, alpha, heads, rb):
    d = x_ref.shape[-1]
    hd = d // heads
    mk = mk_ref[0]
    mv = mv_ref[0]
    for r in range(0, x_ref.shape[0], rb):
        x = x_ref[r:r + rb, :]
        q = _mm(x, wq_ref[...]) * (hd ** -0.5)
        outs = []
        for h in range(heads):
            sl = slice(h * hd, (h + 1) * hd)
            s = _mm(q[:, sl], mk[:, sl], NT)
            e = jnp.exp(s - jnp.max(s, axis=-1, keepdims=True))
            pr = e / jnp.sum(e, axis=-1, keepdims=True)
            outs.append(_mm(pr, mv[:, sl]))
        o = jnp.concatenate(outs, axis=-1)
        o_ref[r:r + rb, :] = _ln(alpha * x + _mm(o, wo_ref[...]), g_ref[...], b_ref[...])


def _attn(x, mk, mv, wq, wo, g, b, alpha, seq):
    n, d = x.shape
    n_mem = mk.shape[1]
    tm = min(_token_tile(n), _token_tile(seq))
    per_batch = seq // tm
    tile = pl.BlockSpec((tm, d), lambda i: (i, 0))
    mem = pl.BlockSpec((1, n_mem, d), lambda i: (i // per_batch, 0, 0))
    full = lambda a, bb: pl.BlockSpec((a, bb), lambda i: (0, 0), pipeline_mode=pl.Buffered(1))
    return pl.pallas_call(
        functools.partial(_attn_kernel, alpha=alpha, heads=XATTN_HEADS, rb=min(tm, ROW_BLOCK)),
        grid=(n // tm,),
        in_specs=[tile, mem, mem, full(d, d), full(d, d), full(1, d), full(1, d)],
        out_specs=tile,
        out_shape=jax.ShapeDtypeStruct((n, d), F32),
        compiler_params=_params("parallel"),
        name="attn",
    )(x, mk.astype(BF16), mv.astype(BF16), wq, wo, g, b)


def _layer(x, mk, mv, shift_prev, s_rwkv, h_re, h_im, lw, s5_ops, alpha):
    bsz, seq, d = x.shape
    n = bsz * seq
    ln = lambda i: (lw['ln_g'][i].reshape(1, d).astype(F32), lw['ln_b'][i].reshape(1, d).astype(F32))
    cols = shift_prev.shape[-1]
    x1 = _ffn_ln(x.reshape(n, d), lw['f1g'], lw['f1u'], lw['f1d'], *ln(0), alpha)
    p = _proj(x1, lw['w_in']).reshape(bsz, seq, -1)
    rw, s_new = _rwkv(p, shift_prev, s_rwkv, lw)
    y, hr, hi = _s5(p, cols, h_re, h_im, s5_ops)
    x2 = _mix(x1, rw.reshape(n, -1), y.reshape(n, -1), lw['glu_w'], lw['glu_b'], lw['w_out'], *ln(1), alpha)
    x3 = _attn(x2, mk.reshape(bsz, -1, d), mv.reshape(bsz, -1, d), lw['xq'], lw['xo'], *ln(2), alpha, seq)
    x4 = _ffn_ln(x3, lw['f2g'], lw['f2u'], lw['f2d'], *ln(3), alpha)
    return x4.reshape(bsz, seq, d), p[:, -1:, :cols], s_new, hr, hi


def kernel(x_prompt, x_sample, mem_prompt, cache_mem_k, cache_mem_v, state_rwkv, cache_shift, state_s5_re, state_s5_im, ln_g, ln_b, ffn1_gate, ffn1_up, ffn1_down, w_in, shift_mu, rwkv_w0, rwkv_w_up, rwkv_a0, rwkv_a_up, rwkv_g_up, rwkv_k_k, rwkv_k_a, rwkv_r_k, rwkv_gn_w, rwkv_gn_b, s5_a_re, s5_a_im, s5_log_dt, s5_b_re, s5_b_im, s5_c_re, s5_c_im, s5_d, s5_glu_w, s5_glu_b, w_mix_out, xattn_q, xattn_k, xattn_v, xattn_o, ffn2_gate, ffn2_up, ffn2_down):
    depth = ln_g.shape[0]
    bp, _, d = x_prompt.shape
    n_mem = mem_prompt.shape[1]
    heads, hd = state_rwkv.shape[2], state_rwkv.shape[3]
    cols = cache_shift.shape[-1]
    groups, state = state_s5_re.shape[2:]
    alpha = (2.0 * depth) ** 0.25
    bf = lambda w: w.astype(BF16)
    xp, xs = x_prompt, x_sample
    outs = [[] for _ in range(10)]
    for l in range(depth):
        lw = dict(ln_g=ln_g[l], ln_b=ln_b[l], f1g=bf(ffn1_gate[l]), f1u=bf(ffn1_up[l]), f1d=bf(ffn1_down[l]),
                  w_in=bf(w_in[l]), shift_mu=shift_mu[l], w0=rwkv_w0[l], w_up=rwkv_w_up[l], a0=rwkv_a0[l],
                  a_up=rwkv_a_up[l], g_up=rwkv_g_up[l], k_k=rwkv_k_k[l], k_a=rwkv_k_a[l], r_k=rwkv_r_k[l],
                  gn_w=rwkv_gn_w[l], gn_b=rwkv_gn_b[l], a_re=s5_a_re[l], a_im=s5_a_im[l], log_dt=s5_log_dt[l],
                  b_re=s5_b_re[l], b_im=s5_b_im[l], c_re=s5_c_re[l], c_im=s5_c_im[l], d=s5_d[l],
                  glu_w=s5_glu_w[l], glu_b=s5_glu_b[l], w_out=w_mix_out[l], xq=bf(xattn_q[l]), xo=bf(xattn_o[l]),
                  f2g=bf(ffn2_gate[l]), f2u=bf(ffn2_up[l]), f2d=bf(ffn2_down[l]))
        s5_ops = _s5_prep(lw)
        mem2 = mem_prompt.reshape(bp * n_mem, d)
        mk_p = _proj(mem2, bf(xattn_k[l])).reshape(bp, n_mem, XATTN_HEADS, d // XATTN_HEADS)
        mv_p = _proj(mem2, bf(xattn_v[l])).reshape(bp, n_mem, XATTN_HEADS, d // XATTN_HEADS)
        xp, sh_p, rw_p, hr_p, hi_p = _layer(
            xp, mk_p, mv_p, jnp.zeros((bp, 1, cols), F32), jnp.zeros((bp, heads, hd, hd), F32),
            jnp.zeros((bp, groups, state), F32), jnp.zeros((bp, groups, state), F32), lw, s5_ops, alpha)
        xs, sh_s, rw_s, hr_s, hi_s = _layer(
            xs, cache_mem_k[l], cache_mem_v[l], cache_shift[l], state_rwkv[l],
            state_s5_re[l], state_s5_im[l], lw, s5_ops, alpha)
        for acc, val in zip(outs, (mk_p, mv_p, rw_p, sh_p, hr_p, hi_p, rw_s, sh_s, hr_s, hi_s)):
            acc.append(val)
    return (xp, xs) + tuple(jnp.stack(o) for o in outs)
```

```python
import functools
import math

import jax
import jax.numpy as jnp
from jax import lax
from jax.experimental import pallas as pl
from jax.experimental.pallas import tpu as pltpu

F32 = jnp.float32
BF16 = jnp.bfloat16

LN_EPS = 1e-5
GN_EPS = 64e-5
RWKV_HEAD = 64
RANK_W = 64
RANK_A = 64
RANK_G = 128
S5_GROUP = 16
S5_STATE = 64
S5_CHUNK = 16
RWKV_CHUNK = 64
RWKV_SUBCHUNKS = 4
XATTN_HEADS = 4
LANES = 128
MXU_TILE = 256
S5_SUB = LANES // S5_GROUP
S5_TILE_CHUNKS = 256
VMEM_LIMIT = 56 * 1024 * 1024
ROW_BLOCK = 256

NT = (((1,), (1,)), ((), ()))
TN = (((0,), (0,)), ((), ()))


def _mm(a, b, dims=None):
    a = a.astype(BF16)
    b = b.astype(BF16)
    if dims is None:
        return jnp.dot(a, b, preferred_element_type=F32)
    return lax.dot_general(a, b, dims, preferred_element_type=F32)


def _mm_f32(a, b, dims=None):
    if dims is None:
        return jnp.dot(a, b, preferred_element_type=F32, precision=lax.Precision.HIGHEST)
    return lax.dot_general(a, b, dims, preferred_element_type=F32, precision=lax.Precision.HIGHEST)


def _split_bf16(x, parts):
    pieces = []
    for _ in range(parts):
        piece = x.astype(BF16)
        pieces.append(piece)
        x = x - piece.astype(F32)
    return pieces


def _ln(z, g, b):
    mu = jnp.mean(z, axis=-1, keepdims=True)
    d = z - mu
    var = jnp.mean(d * d, axis=-1, keepdims=True)
    return d * lax.rsqrt(var + LN_EPS) * g + b


def _sigmoid(x):
    return 1.0 / (1.0 + jnp.exp(-x))


def _params(*sem):
    return pltpu.CompilerParams(dimension_semantics=sem, vmem_limit_bytes=VMEM_LIMIT)


def _token_tile(n):
    for t in (512, 256, 128, 64, 32, 16, 8):
        if n % t == 0:
            return t
    raise ValueError(f"token count {n} is not a multiple of 8")


def _ffn_ln_kernel(x_ref, wg_ref, wu_ref, wd_ref, g_ref, b_ref, o_ref, xb_ref, acc_ref, *, alpha):
    j = pl.program_id(1)

    @pl.when(j == 0)
    def _():
        xb_ref[...] = x_ref[...].astype(BF16)
        acc_ref[...] = jnp.zeros_like(acc_ref)

    xb = xb_ref[...]
    gate = jnp.dot(xb, wg_ref[...], preferred_element_type=F32)
    up = jnp.dot(xb, wu_ref[...], preferred_element_type=F32)
    h = gate * _sigmoid(gate) * up
    acc_ref[...] += jnp.dot(h.astype(BF16), wd_ref[...], preferred_element_type=F32)

    @pl.when(j == pl.num_programs(1) - 1)
    def _():
        o_ref[...] = _ln(alpha * x_ref[...] + 0.5 * acc_ref[...], g_ref[...], b_ref[...])


def _ffn_ln(x, wg, wu, wd, layer, g, b, alpha):
    n, d = x.shape
    dff = wg.shape[-1]
    tm = _token_tile(n)
    tf = 512 if dff % 512 == 0 else dff
    return pl.pallas_call(
        functools.partial(_ffn_ln_kernel, alpha=alpha),
        grid=(n // tm, dff // tf),
        in_specs=[
            pl.BlockSpec((tm, d), lambda i, j: (i, 0)),
            pl.BlockSpec((None, d, tf), lambda i, j: (layer, 0, j)),
            pl.BlockSpec((None, d, tf), lambda i, j: (layer, 0, j)),
            pl.BlockSpec((None, tf, d), lambda i, j: (layer, j, 0)),
            pl.BlockSpec((1, d), lambda i, j: (0, 0)),
            pl.BlockSpec((1, d), lambda i, j: (0, 0)),
        ],
        out_specs=pl.BlockSpec((tm, d), lambda i, j: (i, 0)),
        out_shape=jax.ShapeDtypeStruct((n, d), F32),
        scratch_shapes=[pltpu.VMEM((tm, d), BF16), pltpu.VMEM((tm, d), F32)],
        compiler_params=_params("parallel", "arbitrary"),
        name="ffn_ln",
    )(x, wg, wu, wd, g, b)


def _proj_kernel(x_ref, w_ref, o_ref):
    o_ref[...] = _mm(x_ref[...], w_ref[...])


def _column_tile(nout, cap=2304):
    best = None
    for t in range(LANES, min(nout, cap) + 1, LANES):
        if nout % t == 0:
            best = t
    return best if best is not None else nout


def _proj(x, w, layer):
    n, d = x.shape
    nout = w.shape[-1]
    tm = _token_tile(n)
    tn = _column_tile(nout)
    return pl.pallas_call(
        _proj_kernel,
        grid=(nout // tn, n // tm),
        in_specs=[
            pl.BlockSpec((tm, d), lambda j, i: (i, 0)),
            pl.BlockSpec((None, d, tn), lambda j, i: (layer, 0, j)),
        ],
        out_specs=pl.BlockSpec((tm, tn), lambda j, i: (i, j)),
        out_shape=jax.ShapeDtypeStruct((n, nout), F32),
        compiler_params=_params("parallel", "parallel"),
        name="proj",
    )(x, w)


def _rwkv_kernel(p_ref, shift_ref, s0_ref, mu_ref, w0_ref, wup_ref, a0_ref, aup_ref, gup_ref,
                 kk_ref, ka_ref, rk_ref, gnw_ref, gnb_ref,
                 o_ref, sout_ref, prev_ref, state_ref, *, t, n_sub, heads):
    c = pl.program_id(1)
    r_w = heads * RWKV_HEAD
    gw = min(MXU_TILE, r_w)
    per = gw // RWKV_HEAD
    mw = per * t
    head_shift = RWKV_HEAD.bit_length() - 1
    bf = lambda x: x.astype(BF16)

    @pl.when(c == 0)
    def _():
        prev_ref[...] = shift_ref[0]
        state_ref[...] = s0_ref[0]

    same_head = ((lax.broadcasted_iota(jnp.int32, (gw, gw), 0) >> head_shift)
                 == (lax.broadcasted_iota(jnp.int32, (gw, gw), 1) >> head_shift))
    ones_bd = jnp.where(same_head, 1.0, 0.0).astype(BF16)

    def head_sum(x):
        return jnp.concatenate(
            [sum(jnp.dot(piece, ones_bd, preferred_element_type=F32) for piece in _split_bf16(x[:, q:q + gw], 2))
             for q in range(0, r_w, gw)], axis=1)

    tri = jnp.where(lax.broadcasted_iota(jnp.int32, (t, t), 0) >= lax.broadcasted_iota(jnp.int32, (t, t), 1),
                    1.0, 0.0).astype(BF16)

    def prologue(p, prev_row):
        row = lax.broadcasted_iota(jnp.int32, p.shape, 0)
        prev = jnp.where(row == 0, prev_row, pltpu.roll(p, 1, axis=0))
        ps = p + mu_ref[...] * (prev - p)
        r = ps[:, 0:r_w]
        k = ps[:, r_w:2 * r_w]
        v = ps[:, 2 * r_w:3 * r_w]
        wa_in = ps[:, 3 * r_w:3 * r_w + RANK_W + RANK_A]
        lg = ps[:, 3 * r_w + RANK_W + RANK_A:]
        log_w = -math.exp(-0.5) * _sigmoid(w0_ref[...] + _mm(jnp.tanh(wa_in), wup_ref[...]))
        a = _sigmoid(a0_ref[...] + _mm(wa_in, aup_ref[...]))
        g = _mm(_sigmoid(lg), gup_ref[...])
        kk = k * kk_ref[...]
        kk = kk * lax.rsqrt(jnp.maximum(head_sum(kk * kk), 1e-24))
        k2 = k * (1.0 + (a - 1.0) * ka_ref[...])
        b = kk * a
        cum = sum(jnp.dot(tri, piece, preferred_element_type=F32) for piece in _split_bf16(log_w, 3))
        e_cum = jnp.exp(cum)
        e_neg = jnp.exp(-cum)
        e_tail = jnp.exp(cum[t - 1:t, :] - cum)
        return dict(kq=kk * jnp.exp(cum - log_w), rq=r * e_cum, bd=b * e_neg, kd=k2 * e_neg, bt=b * e_tail,
                    kt=k2 * e_tail, v=v, p_last=e_cum[t - 1:t, :], bonus=head_sum(r * k2 * rk_ref[...]) * v, g=g)

    pro = []
    prev_row = prev_ref[...]
    for j in range(n_sub):
        p = p_ref[0, j * t:(j + 1) * t, :]
        pro.append(prologue(p, prev_row))
        prev_row = p[t - 1:t, :]
    prev_ref[...] = prev_row

    def block_rows(x, bw):
        xb = bf(x)
        blk = lax.broadcasted_iota(jnp.int32, xb.shape, 1) >> (bw.bit_length() - 1)
        return jnp.concatenate([jnp.where(blk == h, xb, jnp.zeros_like(xb)) for h in range(per)], axis=0)

    lane_blk = lax.broadcasted_iota(jnp.int32, (RWKV_HEAD, gw), 1) >> head_shift

    def diag_blocks(z):
        return sum(jnp.where(lane_blk == h, z[h * RWKV_HEAD:(h + 1) * RWKV_HEAD, :], 0.0) for h in range(per))

    quads = range(0, r_w, gw)
    probs = [(j, q) for j in range(n_sub) for q in quads]
    ps_ = range(len(probs))
    part = lambda name: [pro[j][name][:, q:q + gw] for j, q in probs]
    kq, rq, bd, kd, bt, kt, v = map(part, ("kq", "rq", "bd", "kd", "bt", "kt", "v"))
    row_m = lax.broadcasted_iota(jnp.int32, (t, mw), 0)
    col_m = lax.broadcasted_iota(jnp.int32, (t, mw), 1) & (t - 1)
    strict = row_m > col_m
    incl = row_m >= col_m

    vbd = [block_rows(v[i], RWKV_HEAD) for i in ps_]
    ms = [_mm(jnp.concatenate([kq[i], rq[i]], axis=0),
              jnp.concatenate([block_rows(bd[i], RWKV_HEAD), block_rows(kd[i], RWKV_HEAD)], axis=0), NT) for i in ps_]
    m_b = [jnp.where(strict, ms[i][:t, :mw], 0.0) for i in ps_]
    mkv = [_mm(jnp.where(strict, ms[i][:t, mw:], 0.0), vbd[i]) for i in ps_]
    l_b = [bf(jnp.where(incl, ms[i][t:, :mw], 0.0)) for i in ps_]
    l_k = [bf(jnp.where(incl, ms[i][t:, mw:], 0.0)) for i in ps_]
    invs = None
    k = 1
    while k < t:
        shift = (2 * k).bit_length() - 1
        join = ((row_m >> shift) == (col_m >> shift)) & ((row_m & k) != 0) & ((col_m & k) == 0)
        cs = [jnp.where(join, m, 0.0) for m in m_b]
        if k == 1:
            invs = [jnp.where(row_m == col_m, 1.0, 0.0) - c for c in cs]
        else:
            xs = [_mm(cs[i], block_rows(invs[i], t)) for i in ps_]
            invs = [invs[i] - _mm(invs[i], block_rows(xs[i], t)) for i in ps_]
        k *= 2
    invs = [bf(x) for x in invs]
    gk = [_mm(invs[i], block_rows(kq[i], RWKV_HEAD)) for i in ps_]
    u0 = [-_mm(invs[i], block_rows(mkv[i], RWKV_HEAD)) for i in ps_]
    rp = [bf(rq[i] - _mm(l_b[i], block_rows(gk[i], RWKV_HEAD))) for i in ps_]
    y0 = [_mm(jnp.concatenate([l_b[i], l_k[i]], axis=1),
              jnp.concatenate([block_rows(u0[i], RWKV_HEAD), vbd[i]], axis=0)) for i in ps_]
    phi = [block_rows(diag_blocks(_mm(gk[i], bt[i], TN)), RWKV_HEAD) for i in ps_]
    psi = [diag_blocks(_mm(jnp.concatenate([u0[i], v[i]], axis=0), jnp.concatenate([bt[i], kt[i]], axis=0), TN))
           for i in ps_]

    states = [state_ref[:, q:q + gw] for q in quads]
    n_q = len(states)
    inv_hd = 1.0 / RWKV_HEAD
    for j in range(n_sub):
        ys = []
        for qi in range(n_q):
            i = j * n_q + qi
            s = states[qi]
            ys.append(y0[i] + _mm(rp[i], block_rows(s, RWKV_HEAD), NT))
            states[qi] = s * pro[j]["p_last"][:, quads[qi]:quads[qi] + gw] - _mm(s, phi[i]) + psi[i]
        y = jnp.concatenate(ys, axis=1)
        dy = y - head_sum(y) * inv_hd
        var = head_sum(dy * dy) * inv_hd
        yn = dy * lax.rsqrt(var + GN_EPS) * gnw_ref[...] + gnb_ref[...]
        o_ref[0, j * t:(j + 1) * t, :] = (yn + pro[j]["bonus"]) * pro[j]["g"]
    for qi, q in enumerate(quads):
        state_ref[:, q:q + gw] = states[qi]

    @pl.when(c == pl.num_programs(1) - 1)
    def _():
        sout_ref[0] = state_ref[...]


def _rwkv(p3, shift_prev, s0, lw):
    bsz, seq, _ = p3.shape
    heads = s0.shape[1]
    r_w = heads * RWKV_HEAD
    cols = 3 * r_w + RANK_W + RANK_A + RANK_G
    t = min(RWKV_CHUNK, seq)
    assert seq % t == 0 and t & (t - 1) == 0
    n_sub = RWKV_SUBCHUNKS if seq % (RWKV_SUBCHUNKS * t) == 0 else 1
    tt = n_sub * t
    row = lambda x: x.reshape(1, -1).astype(F32)
    zeros = jnp.zeros((RANK_W, r_w), F32)
    wup = jnp.concatenate([lw['w_up'], zeros], axis=0).astype(BF16)
    aup = jnp.concatenate([zeros, lw['a_up']], axis=0).astype(BF16)
    vec = lambda n: pl.BlockSpec((1, n), lambda b, c: (0, 0))
    full = lambda a, bb: pl.BlockSpec((a, bb), lambda b, c: (0, 0))
    lanes_hk = lambda s: jnp.swapaxes(s.astype(F32), 1, 2).reshape(bsz, RWKV_HEAD, r_w)
    out, s_new = pl.pallas_call(
        functools.partial(_rwkv_kernel, t=t, n_sub=n_sub, heads=heads),
        grid=(bsz, seq // tt),
        in_specs=[
            pl.BlockSpec((1, tt, cols), lambda b, c: (b, c, 0)),
            pl.BlockSpec((1, 1, cols), lambda b, c: (b, 0, 0)),
            pl.BlockSpec((1, RWKV_HEAD, r_w), lambda b, c: (b, 0, 0)),
            vec(cols), vec(r_w), full(RANK_W + RANK_A, r_w), vec(r_w), full(RANK_W + RANK_A, r_w),
            full(RANK_G, r_w), vec(r_w), vec(r_w), vec(r_w), vec(r_w), vec(r_w),
        ],
        out_specs=[
            pl.BlockSpec((1, tt, r_w), lambda b, c: (b, c, 0)),
            pl.BlockSpec((1, RWKV_HEAD, r_w), lambda b, c: (b, 0, 0)),
        ],
        out_shape=[
            jax.ShapeDtypeStruct((bsz, seq, r_w), F32),
            jax.ShapeDtypeStruct((bsz, RWKV_HEAD, r_w), F32),
        ],
        scratch_shapes=[pltpu.VMEM((1, cols), F32), pltpu.VMEM((RWKV_HEAD, r_w), F32)],
        compiler_params=_params("parallel", "arbitrary"),
        name="rwkv",
    )(p3, shift_prev.astype(F32), lanes_hk(s0), row(lw['shift_mu']), row(lw['w0']), wup, row(lw['a0']), aup,
      lw['g_up'].astype(BF16), row(lw['k_k']), row(lw['k_a']), row(lw['r_k']), row(lw['gn_w']),
      row(lw['gn_b']))
    return out, jnp.swapaxes(s_new.reshape(bsz, RWKV_HEAD, heads, RWKV_HEAD), 1, 2)


def _s5_prep_kernel(are_ref, aim_ref, ldt_ref, btre_ref, btim_ref, cre_ref, cim_ref, d_ref,
                    kst_ref, wre_ref, wim_ref, vre_ref, vimn_ref, lre_ref, lim_ref, *, tc):
    a_re = are_ref[0]
    a_im = aim_ref[0]
    dt = jnp.exp(ldt_ref[0])
    mag = jnp.exp(a_re * dt)
    l_re = mag * jnp.cos(a_im * dt)
    l_im = mag * jnp.sin(a_im * dt)
    den = a_re * a_re + a_im * a_im
    x_re = l_re - 1.0
    co_re = (x_re * a_re + l_im * a_im) / den
    co_im = (l_im * a_re - x_re * a_im) / den
    pw = [(jnp.ones_like(l_re), jnp.zeros_like(l_re))]
    for _ in range(tc):
        q_re, q_im = pw[-1]
        pw.append((q_re * l_re - q_im * l_im, q_re * l_im + q_im * l_re))

    c_re = cre_ref[0]
    c_im = cim_ref[0]
    bt_re = btre_ref[0]
    bt_im = btim_ref[0]
    cc_re = c_re * co_re - c_im * co_im
    cc_im = c_re * co_im + c_im * co_re
    cl_re = jnp.concatenate([cc_re * q[0] - cc_im * q[1] for q in pw[:tc]], axis=0)
    cl_im = jnp.concatenate([cc_re * q[1] + cc_im * q[0] for q in pw[:tc]], axis=0)
    kst = _mm_f32(cl_re, bt_re, NT) - _mm_f32(cl_im, bt_im, NT)
    rr = lax.broadcasted_iota(jnp.int32, kst.shape, 0)
    cc = lax.broadcasted_iota(jnp.int32, kst.shape, 1)
    kst_ref[0] = kst + jnp.where(rr == cc, d_ref[0], 0.0)

    w_re, w_im = [], []
    for s in range(tc):
        q_re, q_im = pw[tc - 1 - s]
        f_re = q_re * co_re - q_im * co_im
        f_im = q_re * co_im + q_im * co_re
        w_re.append(bt_re * f_re - bt_im * f_im)
        w_im.append(bt_re * f_im + bt_im * f_re)
    wre_ref[0] = jnp.concatenate(w_re, axis=0)
    wim_ref[0] = jnp.concatenate(w_im, axis=0)
    vre_ref[0] = jnp.concatenate([c_re * q[0] - c_im * q[1] for q in pw[1:]], axis=0)
    vimn_ref[0] = jnp.concatenate([-(c_re * q[1] + c_im * q[0]) for q in pw[1:]], axis=0)
    lre_ref[0] = pw[tc][0]
    lim_ref[0] = pw[tc][1]


def _s5_prep(lw):
    groups, state = lw['a_re'].shape
    ch = S5_GROUP
    tc = S5_CHUNK
    g3 = lambda x, a, b: x.astype(F32).reshape(groups, a, b)
    spec = lambda a, b: pl.BlockSpec((1, a, b), lambda g: (g, 0, 0))
    outs = pl.pallas_call(
        functools.partial(_s5_prep_kernel, tc=tc),
        grid=(groups,),
        in_specs=[spec(1, state), spec(1, state), spec(1, 1), spec(ch, state), spec(ch, state),
                  spec(ch, state), spec(ch, state), spec(1, ch)],
        out_specs=[spec(tc * ch, ch), spec(tc * ch, state), spec(tc * ch, state), spec(tc * ch, state),
                   spec(tc * ch, state), spec(1, state), spec(1, state)],
        out_shape=[jax.ShapeDtypeStruct((groups, tc * ch, ch), F32)]
        + [jax.ShapeDtypeStruct((groups, tc * ch, state), F32)] * 4
        + [jax.ShapeDtypeStruct((groups, 1, state), F32)] * 2,
        compiler_params=_params("parallel"),
        name="s5_prep",
    )(g3(lw['a_re'], 1, state), g3(lw['a_im'], 1, state), g3(lw['log_dt'], 1, 1),
      jnp.swapaxes(lw['b_re'], 1, 2).astype(F32), jnp.swapaxes(lw['b_im'], 1, 2).astype(F32),
      lw['c_re'].astype(F32), lw['c_im'].astype(F32), g3(lw['d'], 1, ch))
    kst, w_re, w_im, v_re, v_imn, l_re, l_im = outs
    k4 = kst.reshape(groups, tc, ch, ch)
    tt = jnp.arange(tc)
    tau = tt[:, None] - tt[None, :]
    blocks = jnp.where((tau >= 0)[None, :, :, None, None], k4[:, jnp.clip(tau, 0, tc - 1)], 0.0)
    toep = jnp.transpose(blocks, (0, 1, 3, 2, 4)).reshape(groups, tc * ch, tc * ch)
    return dict(
        toep=toep.astype(BF16),
        w=jnp.concatenate([w_re, w_im, w_im, w_re], axis=-1).astype(BF16),
        vt=jnp.concatenate([v_re, v_imn], axis=-1).astype(BF16),
        a16=jnp.concatenate([l_re, l_re], axis=-1),
        b16=jnp.concatenate([-l_im, l_im], axis=-1),
    )


def _lane_block_transpose(arrs, masks):
    n = len(arrs)
    rolled = [[a if j == 0 else pltpu.roll(a, S5_GROUP * j, axis=1) for j in range(n)] for a in arrs]
    outs = []
    for j in range(n):
        out = rolled[0][(0 - j) % n]
        for i in range(1, n):
            out = jnp.where(masks[i], rolled[i][(i - j) % n], out)
        outs.append(out)
    return outs


def _s5_kernel(x_ref, toep_ref, w_ref, vt_ref, a_ref, b_ref, h0_ref, h0s_ref,
               y_ref, hout_ref, x_st, s_st, ex_ref, es_ref, hs_ref, *, nct):
    ci = pl.program_id(2)
    sub = S5_SUB
    halves = S5_CHUNK // sub
    width = a_ref.shape[-1]

    @pl.when(ci == 0)
    def _():
        x_st[...] = h0_ref[0, 0]
        s_st[...] = h0s_ref[0, 0]

    lane = lax.broadcasted_iota(jnp.int32, (nct, LANES), 1)
    masks = [(lane >= S5_GROUP * j) & (lane < S5_GROUP * (j + 1)) for j in range(sub)]

    rows_of = lambda t: pl.ds(t, nct, stride=S5_CHUNK)
    u_halves = [_lane_block_transpose([x_ref[0, rows_of(hf * sub + tt), :] for tt in range(sub)], masks)
                for hf in range(halves)]
    us = [jnp.concatenate([u_halves[hf][g] for hf in range(halves)], axis=1).astype(BF16) for g in range(sub)]

    for g in range(sub):
        e = _mm(us[g], w_ref[g])
        ex_ref[pl.ds(g, nct, stride=sub), :] = e[:, :width]
        es_ref[pl.ds(g, nct, stride=sub), :] = e[:, width:]

    a = a_ref[0]
    b = b_ref[0]
    steps = min(8, nct)

    def body(i, carry):
        x, s = carry
        for j in range(steps):
            rows = pl.ds(pl.multiple_of((i * steps + j) * sub, sub), sub)
            hs_ref[rows, :] = x
            x, s = a * x + b * s + ex_ref[rows, :], a * s - b * x + es_ref[rows, :]
        return x, s

    x, s = lax.fori_loop(0, nct // steps, body, (x_st[...], s_st[...]))
    x_st[...] = x
    s_st[...] = s
    hout_ref[0, 0] = x

    ys = [_mm(us[g], toep_ref[g], NT) + _mm(hs_ref[pl.ds(g, nct, stride=sub), :], vt_ref[g], NT)
          for g in range(sub)]
    for hf in range(halves):
        outs = _lane_block_transpose([ys[g][:, hf * LANES:(hf + 1) * LANES] for g in range(sub)], masks)
        for tt in range(sub):
            y_ref[0, rows_of(hf * sub + tt), :] = outs[tt]


def _s5(p3, col0, h_re, h_im, ops):
    bsz, seq, n_in = p3.shape
    groups, state = h_re.shape[1:]
    ch, tc, sub = S5_GROUP, S5_CHUNK, S5_SUB
    assert seq % tc == 0 and groups % sub == 0 and col0 % LANES == 0 and tc % sub == 0
    n_chunks = seq // tc
    nct = min(S5_TILE_CHUNKS, n_chunks)
    assert n_chunks % nct == 0 and nct % min(8, nct) == 0
    gbs = groups // sub
    lane0 = col0 // LANES
    pack = lambda x, y: jnp.concatenate([x, y], axis=-1).astype(F32).reshape(bsz, gbs, sub, 2 * state)
    per_g = lambda last: pl.BlockSpec((sub, tc * ch, last), lambda gb, b, ci: (gb, 0, 0))
    vec = pl.BlockSpec((1, sub, 2 * state), lambda gb, b, ci: (gb, 0, 0))
    st = pl.BlockSpec((1, 1, sub, 2 * state), lambda gb, b, ci: (b, gb, 0, 0))
    y, h_out = pl.pallas_call(
        functools.partial(_s5_kernel, nct=nct),
        grid=(gbs, bsz, n_chunks // nct),
        in_specs=[pl.BlockSpec((1, nct * tc, LANES), lambda gb, b, ci: (b, ci, lane0 + gb)),
                  per_g(tc * ch), per_g(4 * state), per_g(2 * state), vec, vec, st, st],
        out_specs=[pl.BlockSpec((1, nct * tc, LANES), lambda gb, b, ci: (b, ci, gb)), st],
        out_shape=[jax.ShapeDtypeStruct((bsz, seq, groups * ch), F32),
                   jax.ShapeDtypeStruct((bsz, gbs, sub, 2 * state), F32)],
        scratch_shapes=[pltpu.VMEM((sub, 2 * state), F32), pltpu.VMEM((sub, 2 * state), F32),
                        pltpu.VMEM((nct * sub, 2 * state), F32), pltpu.VMEM((nct * sub, 2 * state), F32),
                        pltpu.VMEM((nct * sub, 2 * state), F32)],
        compiler_params=_params("parallel", "parallel", "arbitrary"),
        name="s5",
    )(p3, ops['toep'], ops['w'], ops['vt'],
      ops['a16'].reshape(gbs, sub, 2 * state), ops['b16'].reshape(gbs, sub, 2 * state),
      pack(h_re, h_im), pack(h_im, h_re))
    h_out = h_out.reshape(bsz, groups, 2 * state)
    return y, h_out[..., :state], h_out[..., state:]


def _gelu_tanh(x):
    return 0.5 * x * (1.0 + jnp.tanh(math.sqrt(2.0 / math.pi) * (x + 0.044715 * (x * x * x))))


def _mix_kernel(x_ref, rw_ref, y_ref, gw_ref, gb_ref, wo1_ref, wo2_ref, g_ref, b_ref, o_ref, *, alpha, rb):
    for r in range(0, x_ref.shape[0], rb):
        rows = slice(r, r + rb)
        y = _gelu_tanh(y_ref[rows, :])
        s5o = y * _sigmoid(_mm(y, gw_ref[...]) + gb_ref[...])
        mix = _mm(rw_ref[rows, :], wo1_ref[...]) + _mm(s5o, wo2_ref[...])
        o_ref[rows, :] = _ln(alpha * x_ref[rows, :] + mix, g_ref[...], b_ref[...])


def _mix(x, rw, y, glu_w, glu_b, w_out, layer, g, b, alpha):
    n, d = x.shape
    r_w = rw.shape[1]
    s_w = y.shape[1]
    tm = _token_tile(n)
    tile = lambda w: pl.BlockSpec((tm, w), lambda i: (i, 0))
    full = lambda a, bb: pl.BlockSpec((a, bb), lambda i: (0, 0), pipeline_mode=pl.Buffered(1))
    slab = lambda a, bb, k: pl.BlockSpec((None, a, bb), lambda i: (layer, k, 0), pipeline_mode=pl.Buffered(1))
    assert r_w == s_w
    return pl.pallas_call(
        functools.partial(_mix_kernel, alpha=alpha, rb=min(tm, ROW_BLOCK)),
        grid=(n // tm,),
        in_specs=[tile(d), tile(r_w), tile(s_w), slab(s_w, s_w, 0), full(1, s_w), slab(r_w, d, 0), slab(s_w, d, 1),
                  full(1, d), full(1, d)],
        out_specs=tile(d),
        out_shape=jax.ShapeDtypeStruct((n, d), F32),
        compiler_params=_params("parallel"),
        name="mix",
    )(x, rw, y, glu_w, glu_b.reshape(1, -1).astype(F32), w_out, w_out, g, b)


def _attn_kernel(x_ref, mk_ref, mv_ref, wq_ref, wo_ref, g_ref, b_ref, o_ref, *, alpha, heads, rb):
    d = x_ref.shape[-1]
    hd = d // heads
    mk = mk_ref[0]
    mv = mv_ref[0]
    for r in range(0, x_ref.shape[0], rb):
        x = x_ref[r:r + rb, :]
        q = _mm(x, wq_ref[...]) * (hd ** -0.5)
        outs = []
        for h in range(heads):
            sl = slice(h * hd, (h + 1) * hd)
            s = _mm(q[:, sl], mk[:, sl], NT)
            e = jnp.exp(s - jnp.max(s, axis=-1, keepdims=True))
            pr = e / jnp.sum(e, axis=-1, keepdims=True)
            outs.append(_mm(pr, mv[:, sl]))
        o = jnp.concatenate(outs, axis=-1)
        o_ref[r:r + rb, :] = _ln(alpha * x + _mm(o, wo_ref[...]), g_ref[...], b_ref[...])


def _attn(x, mk, mv, wq, wo, layer, g, b, alpha, seq):
    n, d = x.shape
    n_mem = mk.shape[1]
    tm = min(_token_tile(n), _token_tile(seq))
    per_batch = seq // tm
    tile = pl.BlockSpec((tm, d), lambda i: (i, 0))
    mem = pl.BlockSpec((1, n_mem, d), lambda i: (i // per_batch, 0, 0))
    full = lambda a, bb: pl.BlockSpec((a, bb), lambda i: (0, 0), pipeline_mode=pl.Buffered(1))
    slab = pl.BlockSpec((None, d, d), lambda i: (layer, 0, 0), pipeline_mode=pl.Buffered(1))
    return pl.pallas_call(
        functools.partial(_attn_kernel, alpha=alpha, heads=XATTN_HEADS, rb=min(tm, ROW_BLOCK)),
        grid=(n // tm,),
        in_specs=[tile, mem, mem, slab, slab, full(1, d), full(1, d)],
        out_specs=tile,
        out_shape=jax.ShapeDtypeStruct((n, d), F32),
        compiler_params=_params("parallel"),
        name="attn",
    )(x, mk.astype(BF16), mv.astype(BF16), wq, wo, g, b)


def _layer(x, mk, mv, shift_prev, s_rwkv, h_re, h_im, lw, wb, layer, s5_ops, alpha):
    bsz, seq, d = x.shape
    n = bsz * seq
    ln = lambda i: (lw['ln_g'][i].reshape(1, d).astype(F32), lw['ln_b'][i].reshape(1, d).astype(F32))
    cols = shift_prev.shape[-1]
    x1 = _ffn_ln(x.reshape(n, d), wb['f1g'], wb['f1u'], wb['f1d'], layer, *ln(0), alpha)
    p = _proj(x1, wb['w_in'], layer).reshape(bsz, seq, -1)
    rw, s_new = _rwkv(p, shift_prev, s_rwkv, lw)
    y, hr, hi = _s5(p, cols, h_re, h_im, s5_ops)
    x2 = _mix(x1, rw.reshape(n, -1), y.reshape(n, -1), wb['glu_w'], lw['glu_b'], wb['w_out'], layer, *ln(1), alpha)
    x3 = _attn(x2, mk.reshape(bsz, -1, d), mv.reshape(bsz, -1, d), wb['xq'], wb['xo'], layer, *ln(2), alpha, seq)
    x4 = _ffn_ln(x3, wb['f2g'], wb['f2u'], wb['f2d'], layer, *ln(3), alpha)
    return x4.reshape(bsz, seq, d), p[:, -1:, :cols], s_new, hr, hi


def kernel(x_prompt, x_sample, mem_prompt, cache_mem_k, cache_mem_v, state_rwkv, cache_shift, state_s5_re, state_s5_im, ln_g, ln_b, ffn1_gate, ffn1_up, ffn1_down, w_in, shift_mu, rwkv_w0, rwkv_w_up, rwkv_a0, rwkv_a_up, rwkv_g_up, rwkv_k_k, rwkv_k_a, rwkv_r_k, rwkv_gn_w, rwkv_gn_b, s5_a_re, s5_a_im, s5_log_dt, s5_b_re, s5_b_im, s5_c_re, s5_c_im, s5_d, s5_glu_w, s5_glu_b, w_mix_out, xattn_q, xattn_k, xattn_v, xattn_o, ffn2_gate, ffn2_up, ffn2_down):
    depth = ln_g.shape[0]
    bp, _, d = x_prompt.shape
    n_mem = mem_prompt.shape[1]
    heads, hd = state_rwkv.shape[2], state_rwkv.shape[3]
    cols = cache_shift.shape[-1]
    groups, state = state_s5_re.shape[2:]
    alpha = (2.0 * depth) ** 0.25
    bf = lambda w: w.astype(BF16)
    wb = dict(f1g=bf(ffn1_gate), f1u=bf(ffn1_up), f1d=bf(ffn1_down), w_in=bf(w_in), glu_w=bf(s5_glu_w),
              w_out=bf(w_mix_out), xq=bf(xattn_q), xk=bf(xattn_k), xv=bf(xattn_v), xo=bf(xattn_o),
              f2g=bf(ffn2_gate), f2u=bf(ffn2_up), f2d=bf(ffn2_down))
    xp, xs = x_prompt, x_sample
    outs = [[] for _ in range(10)]
    for l in range(depth):
        lw = dict(ln_g=ln_g[l], ln_b=ln_b[l], shift_mu=shift_mu[l], w0=rwkv_w0[l], w_up=rwkv_w_up[l], a0=rwkv_a0[l],
                  a_up=rwkv_a_up[l], g_up=rwkv_g_up[l], k_k=rwkv_k_k[l], k_a=rwkv_k_a[l], r_k=rwkv_r_k[l],
                  gn_w=rwkv_gn_w[l], gn_b=rwkv_gn_b[l], a_re=s5_a_re[l], a_im=s5_a_im[l], log_dt=s5_log_dt[l],
                  b_re=s5_b_re[l], b_im=s5_b_im[l], c_re=s5_c_re[l], c_im=s5_c_im[l], d=s5_d[l],
                  glu_b=s5_glu_b[l])
        s5_ops = _s5_prep(lw)
        mem2 = mem_prompt.reshape(bp * n_mem, d)
        mk_p = _proj(mem2, wb['xk'], l).reshape(bp, n_mem, XATTN_HEADS, d // XATTN_HEADS)
        mv_p = _proj(mem2, wb['xv'], l).reshape(bp, n_mem, XATTN_HEADS, d // XATTN_HEADS)
        xp, sh_p, rw_p, hr_p, hi_p = _layer(
            xp, mk_p, mv_p, jnp.zeros((bp, 1, cols), F32), jnp.zeros((bp, heads, hd, hd), F32),
            jnp.zeros((bp, groups, state), F32), jnp.zeros((bp, groups, state), F32), lw, wb, l, s5_ops, alpha)
        xs, sh_s, rw_s, hr_s, hi_s = _layer(
            xs, cache_mem_k[l], cache_mem_v[l], cache_shift[l], state_rwkv[l],
            state_s5_re[l], state_s5_im[l], lw, wb, l, s5_ops, alpha)
        for acc, val in zip(outs, (mk_p, mv_p, rw_p, sh_p, hr_p, hi_p, rw_s, sh_s, hr_s, hi_s)):
            acc.append(val)
    return (xp, xs) + tuple(jnp.stack(o) for o in outs)
```

```python
import functools
import math

import jax
import jax.numpy as jnp
from jax import lax
from jax.experimental import pallas as pl
from jax.experimental.pallas import tpu as pltpu

F32 = jnp.float32
BF16 = jnp.bfloat16

LN_EPS = 1e-5
GN_EPS = 64e-5
RWKV_HEAD = 64
RANK_W = 64
RANK_A = 64
RANK_G = 128
S5_GROUP = 16
S5_STATE = 64
S5_CHUNK = 16
RWKV_CHUNK = 64
RWKV_SUBCHUNKS = 4
XATTN_HEADS = 4
LANES = 128
MXU_TILE = 256
S5_SUB = LANES // S5_GROUP
S5_TILE_CHUNKS = 256
VMEM_LIMIT = 56 * 1024 * 1024
ROW_BLOCK = 256
CAST_BLOCK_BYTES = 6 * 1024 * 1024

NT = (((1,), (1,)), ((), ()))
TN = (((0,), (0,)), ((), ()))


def _mm(a, b, dims=None):
    a = a.astype(BF16)
    b = b.astype(BF16)
    if dims is None:
        return jnp.dot(a, b, preferred_element_type=F32)
    return lax.dot_general(a, b, dims, preferred_element_type=F32)


def _mm_f32(a, b, dims=None):
    if dims is None:
        return jnp.dot(a, b, preferred_element_type=F32, precision=lax.Precision.HIGHEST)
    return lax.dot_general(a, b, dims, preferred_element_type=F32, precision=lax.Precision.HIGHEST)


def _split_bf16(x, parts):
    pieces = []
    for _ in range(parts):
        piece = x.astype(BF16)
        pieces.append(piece)
        x = x - piece.astype(F32)
    return pieces


def _ln(z, g, b):
    mu = jnp.mean(z, axis=-1, keepdims=True)
    d = z - mu
    var = jnp.mean(d * d, axis=-1, keepdims=True)
    return d * lax.rsqrt(var + LN_EPS) * g + b


def _sigmoid(x):
    return 1.0 / (1.0 + jnp.exp(-x))


def _params(*sem):
    return pltpu.CompilerParams(dimension_semantics=sem, vmem_limit_bytes=VMEM_LIMIT)


def _token_tile(n):
    for t in (512, 256, 128, 64, 32, 16, 8):
        if n % t == 0:
            return t
    raise ValueError(f"token count {n} is not a multiple of 8")


def _cast_kernel(x_ref, o_ref):
    o_ref[...] = x_ref[...].astype(BF16)


def _to_bf16(w):
    depth, a, b = w.shape
    rows = depth * a
    tr = 8
    while rows % (2 * tr) == 0 and 2 * tr * b * 4 <= CAST_BLOCK_BYTES:
        tr *= 2
    out = pl.pallas_call(
        _cast_kernel,
        grid=(rows // tr,),
        in_specs=[pl.BlockSpec((tr, b), lambda i: (i, 0))],
        out_specs=pl.BlockSpec((tr, b), lambda i: (i, 0)),
        out_shape=jax.ShapeDtypeStruct((rows, b), BF16),
        compiler_params=_params("parallel"),
        name="to_bf16",
    )(w.reshape(rows, b))
    return out.reshape(depth, a, b)


def _ffn_ln_kernel(x_ref, wg_ref, wu_ref, wd_ref, g_ref, b_ref, o_ref, xb_ref, acc_ref, *,
                   alpha, n_tiles, n_blocks):
    i = pl.program_id(0)
    j = pl.program_id(1)
    slot = i % 2
    rb = o_ref.shape[0] // n_blocks

    def norm_previous_rows():
        r0 = pl.multiple_of((j % n_blocks) * rb, rb)
        z = 0.5 * acc_ref[1 - slot, pl.ds(r0, rb), :]
        o_ref[pl.ds(r0, rb), :] = _ln(z, g_ref[...], b_ref[...])

    @pl.when((i == 0) & (j == 0))
    def _():
        acc_ref[1] = jnp.zeros(acc_ref.shape[1:], F32)

    @pl.when((i < n_tiles) & (j == 0))
    def _():
        x = x_ref[...]
        xb_ref[...] = x.astype(BF16)
        acc_ref[slot] = (2.0 * alpha) * x

    @pl.when(i < n_tiles)
    def _():
        norm_previous_rows()
        xb = xb_ref[...]
        gate = jnp.dot(xb, wg_ref[...], preferred_element_type=F32)
        up = jnp.dot(xb, wu_ref[...], preferred_element_type=F32)
        h = gate * _sigmoid(gate) * up
        acc_ref[slot] += jnp.dot(h.astype(BF16), wd_ref[...], preferred_element_type=F32)

    @pl.when(i == n_tiles)
    def _():
        norm_previous_rows()


def _ffn_ln(x, wg, wu, wd, layer, g, b, alpha):
    n, d = x.shape
    dff = wg.shape[-1]
    tm = _token_tile(n)
    tf = 512 if dff % 512 == 0 else dff
    n_tiles = n // tm
    steps = dff // tf
    n_blocks = 1
    while 2 * n_blocks <= min(steps, 8) and tm % (16 * n_blocks) == 0:
        n_blocks *= 2
    tile_in = lambda i, j: (jnp.minimum(i, n_tiles - 1), 0)
    chunk = lambda i, j: jnp.where(i < n_tiles, j, steps - 1)
    return pl.pallas_call(
        functools.partial(_ffn_ln_kernel, alpha=alpha, n_tiles=n_tiles, n_blocks=n_blocks),
        grid=(n_tiles + 1, steps),
        in_specs=[
            pl.BlockSpec((tm, d), tile_in),
            pl.BlockSpec((None, d, tf), lambda i, j: (layer, 0, chunk(i, j))),
            pl.BlockSpec((None, d, tf), lambda i, j: (layer, 0, chunk(i, j))),
            pl.BlockSpec((None, tf, d), lambda i, j: (layer, chunk(i, j), 0)),
            pl.BlockSpec((1, d), lambda i, j: (0, 0)),
            pl.BlockSpec((1, d), lambda i, j: (0, 0)),
        ],
        out_specs=pl.BlockSpec((tm, d), lambda i, j: (jnp.maximum(i - 1, 0), 0)),
        out_shape=jax.ShapeDtypeStruct((n, d), F32),
        scratch_shapes=[pltpu.VMEM((tm, d), BF16), pltpu.VMEM((2, tm, d), F32)],
        compiler_params=_params("arbitrary", "arbitrary"),
        name="ffn_ln",
    )(x, wg, wu, wd, g, b)


def _proj_kernel(x_ref, w_ref, o_ref):
    o_ref[...] = _mm(x_ref[...], w_ref[...])


def _column_tile(nout, cap=2304):
    best = None
    for t in range(LANES, min(nout, cap) + 1, LANES):
        if nout % t == 0:
            best = t
    return best if best is not None else nout


def _proj(x, w, layer):
    n, d = x.shape
    nout = w.shape[-1]
    tm = _token_tile(n)
    tn = _column_tile(nout)
    return pl.pallas_call(
        _proj_kernel,
        grid=(nout // tn, n // tm),
        in_specs=[
            pl.BlockSpec((tm, d), lambda j, i: (i, 0)),
            pl.BlockSpec((None, d, tn), lambda j, i: (layer, 0, j)),
        ],
        out_specs=pl.BlockSpec((tm, tn), lambda j, i: (i, j)),
        out_shape=jax.ShapeDtypeStruct((n, nout), F32),
        compiler_params=_params("parallel", "parallel"),
        name="proj",
    )(x, w)


def _rwkv_kernel(p_ref, shift_ref, s0_ref, mu_ref, w0_ref, wup_ref, a0_ref, aup_ref, gup_ref,
                 kk_ref, ka_ref, rk_ref, gnw_ref, gnb_ref,
                 o_ref, sout_ref, prev_ref, state_ref, *, t, n_sub, heads):
    c = pl.program_id(1)
    r_w = heads * RWKV_HEAD
    gw = min(MXU_TILE, r_w)
    per = gw // RWKV_HEAD
    mw = per * t
    head_shift = RWKV_HEAD.bit_length() - 1
    bf = lambda x: x.astype(BF16)

    @pl.when(c == 0)
    def _():
        prev_ref[...] = shift_ref[0]
        state_ref[...] = s0_ref[0]

    same_head = ((lax.broadcasted_iota(jnp.int32, (gw, gw), 0) >> head_shift)
                 == (lax.broadcasted_iota(jnp.int32, (gw, gw), 1) >> head_shift))
    ones_bd = jnp.where(same_head, 1.0, 0.0).astype(BF16)

    def head_sum(x):
        return jnp.concatenate(
            [sum(jnp.dot(piece, ones_bd, preferred_element_type=F32) for piece in _split_bf16(x[:, q:q + gw], 2))
             for q in range(0, r_w, gw)], axis=1)

    tri = jnp.where(lax.broadcasted_iota(jnp.int32, (t, t), 0) >= lax.broadcasted_iota(jnp.int32, (t, t), 1),
                    1.0, 0.0).astype(BF16)

    def prologue(p, prev_row):
        row = lax.broadcasted_iota(jnp.int32, p.shape, 0)
        prev = jnp.where(row == 0, prev_row, pltpu.roll(p, 1, axis=0))
        ps = p + mu_ref[...] * (prev - p)
        r = ps[:, 0:r_w]
        k = ps[:, r_w:2 * r_w]
        v = ps[:, 2 * r_w:3 * r_w]
        wa_in = ps[:, 3 * r_w:3 * r_w + RANK_W + RANK_A]
        lg = ps[:, 3 * r_w + RANK_W + RANK_A:]
        log_w = -math.exp(-0.5) * _sigmoid(w0_ref[...] + _mm(jnp.tanh(wa_in), wup_ref[...]))
        a = _sigmoid(a0_ref[...] + _mm(wa_in, aup_ref[...]))
        g = _mm(_sigmoid(lg), gup_ref[...])
        kk = k * kk_ref[...]
        kk = kk * lax.rsqrt(jnp.maximum(head_sum(kk * kk), 1e-24))
        k2 = k * (1.0 + (a - 1.0) * ka_ref[...])
        b = kk * a
        cum = sum(jnp.dot(tri, piece, preferred_element_type=F32) for piece in _split_bf16(log_w, 3))
        e_cum = jnp.exp(cum)
        e_neg = jnp.exp(-cum)
        e_tail = jnp.exp(cum[t - 1:t, :] - cum)
        return dict(kq=kk * jnp.exp(cum - log_w), rq=r * e_cum, bd=b * e_neg, kd=k2 * e_neg, bt=b * e_tail,
                    kt=k2 * e_tail, v=v, p_last=e_cum[t - 1:t, :], bonus=head_sum(r * k2 * rk_ref[...]) * v, g=g)

    pro = []
    prev_row = prev_ref[...]
    for j in range(n_sub):
        p = p_ref[0, j * t:(j + 1) * t, :]
        pro.append(prologue(p, prev_row))
        prev_row = p[t - 1:t, :]
    prev_ref[...] = prev_row

    def block_rows(x, bw):
        xb = bf(x)
        blk = lax.broadcasted_iota(jnp.int32, xb.shape, 1) >> (bw.bit_length() - 1)
        return jnp.concatenate([jnp.where(blk == h, xb, jnp.zeros_like(xb)) for h in range(per)], axis=0)

    lane_blk = lax.broadcasted_iota(jnp.int32, (RWKV_HEAD, gw), 1) >> head_shift

    def diag_blocks(z):
        return sum(jnp.where(lane_blk == h, z[h * RWKV_HEAD:(h + 1) * RWKV_HEAD, :], 0.0) for h in range(per))

    quads = range(0, r_w, gw)
    probs = [(j, q) for j in range(n_sub) for q in quads]
    ps_ = range(len(probs))
    part = lambda name: [pro[j][name][:, q:q + gw] for j, q in probs]
    kq, rq, bd, kd, bt, kt, v = map(part, ("kq", "rq", "bd", "kd", "bt", "kt", "v"))
    row_m = lax.broadcasted_iota(jnp.int32, (t, mw), 0)
    col_m = lax.broadcasted_iota(jnp.int32, (t, mw), 1) & (t - 1)
    strict = row_m > col_m
    incl = row_m >= col_m

    vbd = [block_rows(v[i], RWKV_HEAD) for i in ps_]
    ms = [_mm(jnp.concatenate([kq[i], rq[i]], axis=0),
              jnp.concatenate([block_rows(bd[i], RWKV_HEAD), block_rows(kd[i], RWKV_HEAD)], axis=0), NT) for i in ps_]
    m_b = [jnp.where(strict, ms[i][:t, :mw], 0.0) for i in ps_]
    mkv = [_mm(jnp.where(strict, ms[i][:t, mw:], 0.0), vbd[i]) for i in ps_]
    l_b = [bf(jnp.where(incl, ms[i][t:, :mw], 0.0)) for i in ps_]
    l_k = [bf(jnp.where(incl, ms[i][t:, mw:], 0.0)) for i in ps_]
    invs = None
    k = 1
    while k < t:
        shift = (2 * k).bit_length() - 1
        join = ((row_m >> shift) == (col_m >> shift)) & ((row_m & k) != 0) & ((col_m & k) == 0)
        cs = [jnp.where(join, m, 0.0) for m in m_b]
        if k == 1:
            invs = [jnp.where(row_m == col_m, 1.0, 0.0) - c for c in cs]
        else:
            xs = [_mm(cs[i], block_rows(invs[i], t)) for i in ps_]
            invs = [invs[i] - _mm(invs[i], block_rows(xs[i], t)) for i in ps_]
        k *= 2
    invs = [bf(x) for x in invs]
    gk = [_mm(invs[i], block_rows(kq[i], RWKV_HEAD)) for i in ps_]
    u0 = [-_mm(invs[i], block_rows(mkv[i], RWKV_HEAD)) for i in ps_]
    rp = [bf(rq[i] - _mm(l_b[i], block_rows(gk[i], RWKV_HEAD))) for i in ps_]
    y0 = [_mm(jnp.concatenate([l_b[i], l_k[i]], axis=1),
              jnp.concatenate([block_rows(u0[i], RWKV_HEAD), vbd[i]], axis=0)) for i in ps_]
    phi = [block_rows(diag_blocks(_mm(gk[i], bt[i], TN)), RWKV_HEAD) for i in ps_]
    psi = [diag_blocks(_mm(jnp.concatenate([u0[i], v[i]], axis=0), jnp.concatenate([bt[i], kt[i]], axis=0), TN))
           for i in ps_]

    states = [state_ref[:, q:q + gw] for q in quads]
    n_q = len(states)
    inv_hd = 1.0 / RWKV_HEAD
    for j in range(n_sub):
        ys = []
        for qi in range(n_q):
            i = j * n_q + qi
            s = states[qi]
            ys.append(y0[i] + _mm(rp[i], block_rows(s, RWKV_HEAD), NT))
            states[qi] = s * pro[j]["p_last"][:, quads[qi]:quads[qi] + gw] - _mm(s, phi[i]) + psi[i]
        y = jnp.concatenate(ys, axis=1)
        dy = y - head_sum(y) * inv_hd
        var = head_sum(dy * dy) * inv_hd
        yn = dy * lax.rsqrt(var + GN_EPS) * gnw_ref[...] + gnb_ref[...]
        o_ref[0, j * t:(j + 1) * t, :] = (yn + pro[j]["bonus"]) * pro[j]["g"]
    for qi, q in enumerate(quads):
        state_ref[:, q:q + gw] = states[qi]

    @pl.when(c == pl.num_programs(1) - 1)
    def _():
        sout_ref[0] = state_ref[...]


def _rwkv(p3, shift_prev, s0, lw):
    bsz, seq, _ = p3.shape
    heads = s0.shape[1]
    r_w = heads * RWKV_HEAD
    cols = 3 * r_w + RANK_W + RANK_A + RANK_G
    t = min(RWKV_CHUNK, seq)
    assert seq % t == 0 and t & (t - 1) == 0
    n_sub = RWKV_SUBCHUNKS if seq % (RWKV_SUBCHUNKS * t) == 0 else 1
    tt = n_sub * t
    row = lambda x: x.reshape(1, -1).astype(F32)
    zeros = jnp.zeros((RANK_W, r_w), F32)
    wup = jnp.concatenate([lw['w_up'], zeros], axis=0).astype(BF16)
    aup = jnp.concatenate([zeros, lw['a_up']], axis=0).astype(BF16)
    vec = lambda n: pl.BlockSpec((1, n), lambda b, c: (0, 0))
    full = lambda a, bb: pl.BlockSpec((a, bb), lambda b, c: (0, 0))
    lanes_hk = lambda s: jnp.swapaxes(s.astype(F32), 1, 2).reshape(bsz, RWKV_HEAD, r_w)
    out, s_new = pl.pallas_call(
        functools.partial(_rwkv_kernel, t=t, n_sub=n_sub, heads=heads),
        grid=(bsz, seq // tt),
        in_specs=[
            pl.BlockSpec((1, tt, cols), lambda b, c: (b, c, 0)),
            pl.BlockSpec((1, 1, cols), lambda b, c: (b, 0, 0)),
            pl.BlockSpec((1, RWKV_HEAD, r_w), lambda b, c: (b, 0, 0)),
            vec(cols), vec(r_w), full(RANK_W + RANK_A, r_w), vec(r_w), full(RANK_W + RANK_A, r_w),
            full(RANK_G, r_w), vec(r_w), vec(r_w), vec(r_w), vec(r_w), vec(r_w),
        ],
        out_specs=[
            pl.BlockSpec((1, tt, r_w), lambda b, c: (b, c, 0)),
            pl.BlockSpec((1, RWKV_HEAD, r_w), lambda b, c: (b, 0, 0)),
        ],
        out_shape=[
            jax.ShapeDtypeStruct((bsz, seq, r_w), F32),
            jax.ShapeDtypeStruct((bsz, RWKV_HEAD, r_w), F32),
        ],
        scratch_shapes=[pltpu.VMEM((1, cols), F32), pltpu.VMEM((RWKV_HEAD, r_w), F32)],
        compiler_params=_params("parallel", "arbitrary"),
        name="rwkv",
    )(p3, shift_prev.astype(F32), lanes_hk(s0), row(lw['shift_mu']), row(lw['w0']), wup, row(lw['a0']), aup,
      lw['g_up'].astype(BF16), row(lw['k_k']), row(lw['k_a']), row(lw['r_k']), row(lw['gn_w']),
      row(lw['gn_b']))
    return out, jnp.swapaxes(s_new.reshape(bsz, RWKV_HEAD, heads, RWKV_HEAD), 1, 2)


def _s5_prep_kernel(are_ref, aim_ref, ldt_ref, btre_ref, btim_ref, cre_ref, cim_ref, d_ref,
                    kst_ref, wre_ref, wim_ref, vre_ref, vimn_ref, lre_ref, lim_ref, *, tc):
    for g in range(are_ref.shape[0]):
        a_re = are_ref[g]
        a_im = aim_ref[g]
        dt = jnp.exp(ldt_ref[g])
        mag = jnp.exp(a_re * dt)
        l_re = mag * jnp.cos(a_im * dt)
        l_im = mag * jnp.sin(a_im * dt)
        den = a_re * a_re + a_im * a_im
        x_re = l_re - 1.0
        co_re = (x_re * a_re + l_im * a_im) / den
        co_im = (l_im * a_re - x_re * a_im) / den
        pw = [(jnp.ones_like(l_re), jnp.zeros_like(l_re))]
        for _ in range(tc):
            q_re, q_im = pw[-1]
            pw.append((q_re * l_re - q_im * l_im, q_re * l_im + q_im * l_re))

        c_re = cre_ref[g]
        c_im = cim_ref[g]
        bt_re = btre_ref[g]
        bt_im = btim_ref[g]
        cc_re = c_re * co_re - c_im * co_im
        cc_im = c_re * co_im + c_im * co_re
        cl_re = jnp.concatenate([cc_re * q[0] - cc_im * q[1] for q in pw[:tc]], axis=0)
        cl_im = jnp.concatenate([cc_re * q[1] + cc_im * q[0] for q in pw[:tc]], axis=0)
        kst = _mm_f32(cl_re, bt_re, NT) - _mm_f32(cl_im, bt_im, NT)
        rr = lax.broadcasted_iota(jnp.int32, kst.shape, 0)
        cc = lax.broadcasted_iota(jnp.int32, kst.shape, 1)
        kst_ref[g] = kst + jnp.where(rr == cc, d_ref[g], 0.0)

        w_re, w_im = [], []
        for s in range(tc):
            q_re, q_im = pw[tc - 1 - s]
            f_re = q_re * co_re - q_im * co_im
            f_im = q_re * co_im + q_im * co_re
            w_re.append(bt_re * f_re - bt_im * f_im)
            w_im.append(bt_re * f_im + bt_im * f_re)
        wre_ref[g] = jnp.concatenate(w_re, axis=0)
        wim_ref[g] = jnp.concatenate(w_im, axis=0)
        vre_ref[g] = jnp.concatenate([c_re * q[0] - c_im * q[1] for q in pw[1:]], axis=0)
        vimn_ref[g] = jnp.concatenate([-(c_re * q[1] + c_im * q[0]) for q in pw[1:]], axis=0)
        lre_ref[g] = pw[tc][0]
        lim_ref[g] = pw[tc][1]


def _s5_prep(lw):
    groups, state = lw['a_re'].shape
    ch = S5_GROUP
    tc = S5_CHUNK
    g3 = lambda x, a, b: x.astype(F32).reshape(groups, a, b)
    gpb = S5_SUB if groups % S5_SUB == 0 else 1
    spec = lambda a, b: pl.BlockSpec((gpb, a, b), lambda g: (g, 0, 0))
    outs = pl.pallas_call(
        functools.partial(_s5_prep_kernel, tc=tc),
        grid=(groups // gpb,),
        in_specs=[spec(1, state), spec(1, state), spec(1, 1), spec(ch, state), spec(ch, state),
                  spec(ch, state), spec(ch, state), spec(1, ch)],
        out_specs=[spec(tc * ch, ch), spec(tc * ch, state), spec(tc * ch, state), spec(tc * ch, state),
                   spec(tc * ch, state), spec(1, state), spec(1, state)],
        out_shape=[jax.ShapeDtypeStruct((groups, tc * ch, ch), F32)]
        + [jax.ShapeDtypeStruct((groups, tc * ch, state), F32)] * 4
        + [jax.ShapeDtypeStruct((groups, 1, state), F32)] * 2,
        compiler_params=_params("parallel"),
        name="s5_prep",
    )(g3(lw['a_re'], 1, state), g3(lw['a_im'], 1, state), g3(lw['log_dt'], 1, 1),
      jnp.swapaxes(lw['b_re'], 1, 2).astype(F32), jnp.swapaxes(lw['b_im'], 1, 2).astype(F32),
      lw['c_re'].astype(F32), lw['c_im'].astype(F32), g3(lw['d'], 1, ch))
    kst, w_re, w_im, v_re, v_imn, l_re, l_im = outs
    k4 = kst.reshape(groups, tc, ch, ch)
    tt = jnp.arange(tc)
    tau = tt[:, None] - tt[None, :]
    blocks = jnp.where((tau >= 0)[None, :, :, None, None], k4[:, jnp.clip(tau, 0, tc - 1)], 0.0)
    toep = jnp.transpose(blocks, (0, 1, 3, 2, 4)).reshape(groups, tc * ch, tc * ch)
    return dict(
        toep=toep.astype(BF16),
        w=jnp.concatenate([w_re, w_im, w_im, w_re], axis=-1).astype(BF16),
        vt=jnp.concatenate([v_re, v_imn], axis=-1).astype(BF16),
        a16=jnp.concatenate([l_re, l_re], axis=-1),
        b16=jnp.concatenate([-l_im, l_im], axis=-1),
    )


def _lane_block_transpose(arrs, masks):
    n = len(arrs)
    rolled = [[a if j == 0 else pltpu.roll(a, S5_GROUP * j, axis=1) for j in range(n)] for a in arrs]
    outs = []
    for j in range(n):
        out = rolled[0][(0 - j) % n]
        for i in range(1, n):
            out = jnp.where(masks[i], rolled[i][(i - j) % n], out)
        outs.append(out)
    return outs


def _s5_kernel(x_ref, toep_ref, w_ref, vt_ref, a_ref, b_ref, h0_ref, h0s_ref,
               y_ref, hout_ref, x_st, s_st, ex_ref, es_ref, hs_ref, *, nct):
    ci = pl.program_id(2)
    sub = S5_SUB
    halves = S5_CHUNK // sub
    width = a_ref.shape[-1]

    @pl.when(ci == 0)
    def _():
        x_st[...] = h0_ref[0, 0]
        s_st[...] = h0s_ref[0, 0]

    lane = lax.broadcasted_iota(jnp.int32, (nct, LANES), 1)
    masks = [(lane >= S5_GROUP * j) & (lane < S5_GROUP * (j + 1)) for j in range(sub)]

    rows_of = lambda t: pl.ds(t, nct, stride=S5_CHUNK)
    u_halves = [_lane_block_transpose([x_ref[0, rows_of(hf * sub + tt), :] for tt in range(sub)], masks)
                for hf in range(halves)]
    us = [jnp.concatenate([u_halves[hf][g] for hf in range(halves)], axis=1).astype(BF16) for g in range(sub)]

    for g in range(sub):
        e = _mm(us[g], w_ref[g])
        ex_ref[pl.ds(g, nct, stride=sub), :] = e[:, :width]
        es_ref[pl.ds(g, nct, stride=sub), :] = e[:, width:]

    a = a_ref[0]
    b = b_ref[0]
    steps = min(8, nct)

    def body(i, carry):
        x, s = carry
        for j in range(steps):
            rows = pl.ds(pl.multiple_of((i * steps + j) * sub, sub), sub)
            hs_ref[rows, :] = x
            x, s = a * x + b * s + ex_ref[rows, :], a * s - b * x + es_ref[rows, :]
        return x, s

    x, s = lax.fori_loop(0, nct // steps, body, (x_st[...], s_st[...]))
    x_st[...] = x
    s_st[...] = s
    hout_ref[0, 0] = x

    ys = [_mm(us[g], toep_ref[g], NT) + _mm(hs_ref[pl.ds(g, nct, stride=sub), :], vt_ref[g], NT)
          for g in range(sub)]
    for hf in range(halves):
        outs = _lane_block_transpose([ys[g][:, hf * LANES:(hf + 1) * LANES] for g in range(sub)], masks)
        for tt in range(sub):
            y_ref[0, rows_of(hf * sub + tt), :] = outs[tt]


def _s5(p3, col0, h_re, h_im, ops):
    bsz, seq, n_in = p3.shape
    groups, state = h_re.shape[1:]
    ch, tc, sub = S5_GROUP, S5_CHUNK, S5_SUB
    assert seq % tc == 0 and groups % sub == 0 and col0 % LANES == 0 and tc % sub == 0
    n_chunks = seq // tc
    nct = min(S5_TILE_CHUNKS, n_chunks)
    assert n_chunks % nct == 0 and nct % min(8, nct) == 0
    gbs = groups // sub
    lane0 = col0 // LANES
    pack = lambda x, y: jnp.concatenate([x, y], axis=-1).astype(F32).reshape(bsz, gbs, sub, 2 * state)
    per_g = lambda last: pl.BlockSpec((sub, tc * ch, last), lambda gb, b, ci: (gb, 0, 0))
    vec = pl.BlockSpec((1, sub, 2 * state), lambda gb, b, ci: (gb, 0, 0))
    st = pl.BlockSpec((1, 1, sub, 2 * state), lambda gb, b, ci: (b, gb, 0, 0))
    y, h_out = pl.pallas_call(
        functools.partial(_s5_kernel, nct=nct),
        grid=(gbs, bsz, n_chunks // nct),
        in_specs=[pl.BlockSpec((1, nct * tc, LANES), lambda gb, b, ci: (b, ci, lane0 + gb)),
                  per_g(tc * ch), per_g(4 * state), per_g(2 * state), vec, vec, st, st],
        out_specs=[pl.BlockSpec((1, nct * tc, LANES), lambda gb, b, ci: (b, ci, gb)), st],
        out_shape=[jax.ShapeDtypeStruct((bsz, seq, groups * ch), F32),
                   jax.ShapeDtypeStruct((bsz, gbs, sub, 2 * state), F32)],
        scratch_shapes=[pltpu.VMEM((sub, 2 * state), F32), pltpu.VMEM((sub, 2 * state), F32),
                        pltpu.VMEM((nct * sub, 2 * state), F32), pltpu.VMEM((nct * sub, 2 * state), F32),
                        pltpu.VMEM((nct * sub, 2 * state), F32)],
        compiler_params=_params("parallel", "parallel", "arbitrary"),
        name="s5",
    )(p3, ops['toep'], ops['w'], ops['vt'],
      ops['a16'].reshape(gbs, sub, 2 * state), ops['b16'].reshape(gbs, sub, 2 * state),
      pack(h_re, h_im), pack(h_im, h_re))
    h_out = h_out.reshape(bsz, groups, 2 * state)
    return y, h_out[..., :state], h_out[..., state:]


def _gelu_tanh(x):
    return 0.5 * x * (1.0 + jnp.tanh(math.sqrt(2.0 / math.pi) * (x + 0.044715 * (x * x * x))))


def _mix_kernel(x_ref, rw_ref, y_ref, gw_ref, gb_ref, wo1_ref, wo2_ref, g_ref, b_ref, o_ref, *, alpha, rb):
    for r in range(0, x_ref.shape[0], rb):
        rows = slice(r, r + rb)
        y = _gelu_tanh(y_ref[rows, :])
        s5o = y * _sigmoid(_mm(y, gw_ref[...]) + gb_ref[...])
        mix = _mm(rw_ref[rows, :], wo1_ref[...]) + _mm(s5o, wo2_ref[...])
        o_ref[rows, :] = _ln(alpha * x_ref[rows, :] + mix, g_ref[...], b_ref[...])


def _mix(x, rw, y, glu_w, glu_b, w_out, layer, g, b, alpha):
    n, d = x.shape
    r_w = rw.shape[1]
    s_w = y.shape[1]
    tm = _token_tile(n)
    tile = lambda w: pl.BlockSpec((tm, w), lambda i: (i, 0))
    full = lambda a, bb: pl.BlockSpec((a, bb), lambda i: (0, 0), pipeline_mode=pl.Buffered(1))
    slab = lambda a, bb, k: pl.BlockSpec((None, a, bb), lambda i: (layer, k, 0), pipeline_mode=pl.Buffered(1))
    assert r_w == s_w
    return pl.pallas_call(
        functools.partial(_mix_kernel, alpha=alpha, rb=min(tm, ROW_BLOCK)),
        grid=(n // tm,),
        in_specs=[tile(d), tile(r_w), tile(s_w), slab(s_w, s_w, 0), full(1, s_w), slab(r_w, d, 0), slab(s_w, d, 1),
                  full(1, d), full(1, d)],
        out_specs=tile(d),
        out_shape=jax.ShapeDtypeStruct((n, d), F32),
        compiler_params=_params("parallel"),
        name="mix",
    )(x, rw, y, glu_w, glu_b.reshape(1, -1).astype(F32), w_out, w_out, g, b)


def _attn_kernel(x_ref, mk_ref, mv_ref, wq_ref, wo_ref, g_ref, b_ref, o_ref, *, alpha, heads, rb):
    d = x_ref.shape[-1]
    hd = d // heads
    mk = mk_ref[0]
    mv = mv_ref[0]
    for r in range(0, x_ref.shape[0], rb):
        x = x_ref[r:r + rb, :]
        q = _mm(x, wq_ref[...]) * (hd ** -0.5)
        outs = []
        for h in range(heads):
            sl = slice(h * hd, (h + 1) * hd)
            s = _mm(q[:, sl], mk[:, sl], NT)
            e = jnp.exp(s - jnp.max(s, axis=-1, keepdims=True))
            pr = e / jnp.sum(e, axis=-1, keepdims=True)
            outs.append(_mm(pr, mv[:, sl]))
        o = jnp.concatenate(outs, axis=-1)
        o_ref[r:r + rb, :] = _ln(alpha * x + _mm(o, wo_ref[...]), g_ref[...], b_ref[...])


def _attn(x, mk, mv, wq, wo, layer, g, b, alpha, seq):
    n, d = x.shape
    n_mem = mk.shape[1]
    tm = min(_token_tile(n), _token_tile(seq))
    per_batch = seq // tm
    tile = pl.BlockSpec((tm, d), lambda i: (i, 0))
    mem = pl.BlockSpec((1, n_mem, d), lambda i: (i // per_batch, 0, 0))
    full = lambda a, bb: pl.BlockSpec((a, bb), lambda i: (0, 0), pipeline_mode=pl.Buffered(1))
    slab = pl.BlockSpec((None, d, d), lambda i: (layer, 0, 0), pipeline_mode=pl.Buffered(1))
    return pl.pallas_call(
        functools.partial(_attn_kernel, alpha=alpha, heads=XATTN_HEADS, rb=min(tm, ROW_BLOCK)),
        grid=(n // tm,),
        in_specs=[tile, mem, mem, slab, slab, full(1, d), full(1, d)],
        out_specs=tile,
        out_shape=jax.ShapeDtypeStruct((n, d), F32),
        compiler_params=_params("parallel"),
        name="attn",
    )(x, mk.astype(BF16), mv.astype(BF16), wq, wo, g, b)


def _layer(x, mk, mv, shift_prev, s_rwkv, h_re, h_im, lw, wb, layer, s5_ops, alpha):
    bsz, seq, d = x.shape
    n = bsz * seq
    ln = lambda i: (lw['ln_g'][i].reshape(1, d).astype(F32), lw['ln_b'][i].reshape(1, d).astype(F32))
    cols = shift_prev.shape[-1]
    x1 = _ffn_ln(x.reshape(n, d), wb['f1g'], wb['f1u'], wb['f1d'], layer, *ln(0), alpha)
    p = _proj(x1, wb['w_in'], layer).reshape(bsz, seq, -1)
    rw, s_new = _rwkv(p, shift_prev, s_rwkv, lw)
    y, hr, hi = _s5(p, cols, h_re, h_im, s5_ops)
    x2 = _mix(x1, rw.reshape(n, -1), y.reshape(n, -1), wb['glu_w'], lw['glu_b'], wb['w_out'], layer, *ln(1), alpha)
    x3 = _attn(x2, mk.reshape(bsz, -1, d), mv.reshape(bsz, -1, d), wb['xq'], wb['xo'], layer, *ln(2), alpha, seq)
    x4 = _ffn_ln(x3, wb['f2g'], wb['f2u'], wb['f2d'], layer, *ln(3), alpha)
    return x4.reshape(bsz, seq, d), p[:, -1:, :cols], s_new, hr, hi


def kernel(x_prompt, x_sample, mem_prompt, cache_mem_k, cache_mem_v, state_rwkv, cache_shift, state_s5_re, state_s5_im, ln_g, ln_b, ffn1_gate, ffn1_up, ffn1_down, w_in, shift_mu, rwkv_w0, rwkv_w_up, rwkv_a0, rwkv_a_up, rwkv_g_up, rwkv_k_k, rwkv_k_a, rwkv_r_k, rwkv_gn_w, rwkv_gn_b, s5_a_re, s5_a_im, s5_log_dt, s5_b_re, s5_b_im, s5_c_re, s5_c_im, s5_d, s5_glu_w, s5_glu_b, w_mix_out, xattn_q, xattn_k, xattn_v, xattn_o, ffn2_gate, ffn2_up, ffn2_down):
    depth = ln_g.shape[0]
    bp, _, d = x_prompt.shape
    n_mem = mem_prompt.shape[1]
    heads, hd = state_rwkv.shape[2], state_rwkv.shape[3]
    cols = cache_shift.shape[-1]
    groups, state = state_s5_re.shape[2:]
    alpha = (2.0 * depth) ** 0.25
    bf = _to_bf16
    wb = dict(f1g=bf(ffn1_gate), f1u=bf(ffn1_up), f1d=bf(ffn1_down), w_in=bf(w_in), glu_w=bf(s5_glu_w),
              w_out=bf(w_mix_out), xq=bf(xattn_q), xk=bf(xattn_k), xv=bf(xattn_v), xo=bf(xattn_o),
              f2g=bf(ffn2_gate), f2u=bf(ffn2_up), f2d=bf(ffn2_down))
    xp, xs = x_prompt, x_sample
    outs = [[] for _ in range(10)]
    for l in range(depth):
        lw = dict(ln_g=ln_g[l], ln_b=ln_b[l], shift_mu=shift_mu[l], w0=rwkv_w0[l], w_up=rwkv_w_up[l], a0=rwkv_a0[l],
                  a_up=rwkv_a_up[l], g_up=rwkv_g_up[l], k_k=rwkv_k_k[l], k_a=rwkv_k_a[l], r_k=rwkv_r_k[l],
                  gn_w=rwkv_gn_w[l], gn_b=rwkv_gn_b[l], a_re=s5_a_re[l], a_im=s5_a_im[l], log_dt=s5_log_dt[l],
                  b_re=s5_b_re[l], b_im=s5_b_im[l], c_re=s5_c_re[l], c_im=s5_c_im[l], d=s5_d[l],
                  glu_b=s5_glu_b[l])
        s5_ops = _s5_prep(lw)
        mem2 = mem_prompt.reshape(bp * n_mem, d)
        mk_p = _proj(mem2, wb['xk'], l).reshape(bp, n_mem, XATTN_HEADS, d // XATTN_HEADS)
        mv_p = _proj(mem2, wb['xv'], l).reshape(bp, n_mem, XATTN_HEADS, d // XATTN_HEADS)
        xp, sh_p, rw_p, hr_p, hi_p = _layer(
            xp, mk_p, mv_p, jnp.zeros((bp, 1, cols), F32), jnp.zeros((bp, heads, hd, hd), F32),
            jnp.zeros((bp, groups, state), F32), jnp.zeros((bp, groups, state), F32), lw, wb, l, s5_ops, alpha)
        xs, sh_s, rw_s, hr_s, hi_s = _layer(
            xs, cache_mem_k[l], cache_mem_v[l], cache_shift[l], state_rwkv[l],
            state_s5_re[l], state_s5_im[l], lw, wb, l, s5_ops, alpha)
        for acc, val in zip(outs, (mk_p, mv_p, rw_p, sh_p, hr_p, hi_p, rw_s, sh_s, hr_s, hi_s)):
            acc.append(val)
    return (xp, xs) + tuple(jnp.stack(o) for o in outs)
```

```python
import functools
import math

import jax
import jax.numpy as jnp
from jax import lax
from jax.experimental import pallas as pl
from jax.experimental.pallas import tpu as pltpu

F32 = jnp.float32
BF16 = jnp.bfloat16

LN_EPS = 1e-5
GN_EPS = 64e-5
RWKV_HEAD = 64
RANK_W = 64
RANK_A = 64
RANK_G = 128
S5_GROUP = 16
S5_STATE = 64
S5_CHUNK = 16
RWKV_CHUNK = 64
RWKV_SUBCHUNKS = 4
XATTN_HEADS = 4
LANES = 128
MXU_TILE = 256
S5_SUB = LANES // S5_GROUP
S5_TILE_CHUNKS = 256
VMEM_LIMIT = 56 * 1024 * 1024
ROW_BLOCK = 256
CAST_BLOCK_BYTES = 6 * 1024 * 1024

NT = (((1,), (1,)), ((), ()))
TN = (((0,), (0,)), ((), ()))


def _mm(a, b, dims=None):
    a = a.astype(BF16)
    b = b.astype(BF16)
    if dims is None:
        return jnp.dot(a, b, preferred_element_type=F32)
    return lax.dot_general(a, b, dims, preferred_element_type=F32)


def _mm_f32(a, b, dims=None):
    if dims is None:
        return jnp.dot(a, b, preferred_element_type=F32, precision=lax.Precision.HIGHEST)
    return lax.dot_general(a, b, dims, preferred_element_type=F32, precision=lax.Precision.HIGHEST)


def _split_bf16(x, parts):
    pieces = []
    for _ in range(parts):
        piece = x.astype(BF16)
        pieces.append(piece)
        x = x - piece.astype(F32)
    return pieces


def _ln(z, g, b):
    mu = jnp.mean(z, axis=-1, keepdims=True)
    d = z - mu
    var = jnp.mean(d * d, axis=-1, keepdims=True)
    return d * lax.rsqrt(var + LN_EPS) * g + b


def _sigmoid(x):
    return 1.0 / (1.0 + jnp.exp(-x))


def _params(*sem):
    return pltpu.CompilerParams(dimension_semantics=sem, vmem_limit_bytes=VMEM_LIMIT)


def _token_tile(n):
    for t in (512, 256, 128, 64, 32, 16, 8):
        if n % t == 0:
            return t
    raise ValueError(f"token count {n} is not a multiple of 8")


def _cast_kernel(x_ref, o_ref):
    o_ref[...] = x_ref[...].astype(BF16)


def _to_bf16(w):
    depth, a, b = w.shape
    rows = depth * a
    tr = 8
    while rows % (2 * tr) == 0 and 2 * tr * b * 4 <= CAST_BLOCK_BYTES:
        tr *= 2
    out = pl.pallas_call(
        _cast_kernel,
        grid=(rows // tr,),
        in_specs=[pl.BlockSpec((tr, b), lambda i: (i, 0))],
        out_specs=pl.BlockSpec((tr, b), lambda i: (i, 0)),
        out_shape=jax.ShapeDtypeStruct((rows, b), BF16),
        compiler_params=_params("parallel"),
        name="to_bf16",
    )(w.reshape(rows, b))
    return out.reshape(depth, a, b)


def _ffn_ln_kernel(x_ref, wg_ref, wu_ref, wd_ref, g_ref, b_ref, o_ref, xb_ref, acc_ref, *,
                   alpha, n_tiles, n_blocks):
    i = pl.program_id(0)
    j = pl.program_id(1)
    slot = i % 2
    rb = o_ref.shape[0] // n_blocks

    def norm_previous_rows():
        r0 = pl.multiple_of((j % n_blocks) * rb, rb)
        z = 0.5 * acc_ref[1 - slot, pl.ds(r0, rb), :]
        o_ref[pl.ds(r0, rb), :] = _ln(z, g_ref[...], b_ref[...])

    @pl.when((i == 0) & (j == 0))
    def _():
        acc_ref[1] = jnp.zeros(acc_ref.shape[1:], F32)

    @pl.when((i < n_tiles) & (j == 0))
    def _():
        x = x_ref[...]
        xb_ref[...] = x.astype(BF16)
        acc_ref[slot] = (2.0 * alpha) * x

    @pl.when(i < n_tiles)
    def _():
        norm_previous_rows()
        xb = xb_ref[...]
        gate = jnp.dot(xb, wg_ref[...], preferred_element_type=F32)
        up = jnp.dot(xb, wu_ref[...], preferred_element_type=F32)
        h = gate * _sigmoid(gate) * up
        acc_ref[slot] += jnp.dot(h.astype(BF16), wd_ref[...], preferred_element_type=F32)

    @pl.when(i == n_tiles)
    def _():
        norm_previous_rows()


def _ffn_ln(x, wg, wu, wd, layer, g, b, alpha):
    n, d = x.shape
    dff = wg.shape[-1]
    tm = _token_tile(n)
    tf = 512 if dff % 512 == 0 else dff
    n_tiles = n // tm
    steps = dff // tf
    n_blocks = 1
    while 2 * n_blocks <= min(steps, 8) and tm % (16 * n_blocks) == 0:
        n_blocks *= 2
    tile_in = lambda i, j: (jnp.minimum(i, n_tiles - 1), 0)
    chunk = lambda i, j: jnp.where(i < n_tiles, j, steps - 1)
    return pl.pallas_call(
        functools.partial(_ffn_ln_kernel, alpha=alpha, n_tiles=n_tiles, n_blocks=n_blocks),
        grid=(n_tiles + 1, steps),
        in_specs=[
            pl.BlockSpec((tm, d), tile_in),
            pl.BlockSpec((None, d, tf), lambda i, j: (layer, 0, chunk(i, j))),
            pl.BlockSpec((None, d, tf), lambda i, j: (layer, 0, chunk(i, j))),
            pl.BlockSpec((None, tf, d), lambda i, j: (layer, chunk(i, j), 0)),
            pl.BlockSpec((1, d), lambda i, j: (0, 0)),
            pl.BlockSpec((1, d), lambda i, j: (0, 0)),
        ],
        out_specs=pl.BlockSpec((tm, d), lambda i, j: (jnp.maximum(i - 1, 0), 0)),
        out_shape=jax.ShapeDtypeStruct((n, d), F32),
        scratch_shapes=[pltpu.VMEM((tm, d), BF16), pltpu.VMEM((2, tm, d), F32)],
        compiler_params=_params("arbitrary", "arbitrary"),
        name="ffn_ln",
    )(x, wg, wu, wd, g, b)


def _proj_kernel(x_ref, w_ref, o_ref):
    o_ref[...] = _mm(x_ref[...], w_ref[...])


def _column_tile(nout, cap=2304):
    best = None
    for t in range(LANES, min(nout, cap) + 1, LANES):
        if nout % t == 0:
            best = t
    return best if best is not None else nout


def _proj(x, w, layer):
    n, d = x.shape
    nout = w.shape[-1]
    tm = _token_tile(n)
    tn = _column_tile(nout)
    return pl.pallas_call(
        _proj_kernel,
        grid=(nout // tn, n // tm),
        in_specs=[
            pl.BlockSpec((tm, d), lambda j, i: (i, 0)),
            pl.BlockSpec((None, d, tn), lambda j, i: (layer, 0, j)),
        ],
        out_specs=pl.BlockSpec((tm, tn), lambda j, i: (i, j)),
        out_shape=jax.ShapeDtypeStruct((n, nout), F32),
        compiler_params=_params("parallel", "parallel"),
        name="proj",
    )(x, w)


def _rwkv_kernel(p_ref, shift_ref, s0_ref, mu_ref, w0_ref, wup_ref, a0_ref, aup_ref, gup_ref,
                 kk_ref, ka_ref, rk_ref, gnw_ref, gnb_ref,
                 o_ref, sout_ref, prev_ref, state_ref, *, t, n_sub, heads):
    c = pl.program_id(1)
    r_w = heads * RWKV_HEAD
    gw = min(MXU_TILE, r_w)
    per = gw // RWKV_HEAD
    mw = per * t
    head_shift = RWKV_HEAD.bit_length() - 1
    bf = lambda x: x.astype(BF16)

    @pl.when(c == 0)
    def _():
        prev_ref[...] = shift_ref[0]
        state_ref[...] = s0_ref[0]

    same_head = ((lax.broadcasted_iota(jnp.int32, (gw, gw), 0) >> head_shift)
                 == (lax.broadcasted_iota(jnp.int32, (gw, gw), 1) >> head_shift))
    ones_bd = jnp.where(same_head, 1.0, 0.0).astype(BF16)

    def head_sum(x):
        return jnp.concatenate(
            [sum(jnp.dot(piece, ones_bd, preferred_element_type=F32) for piece in _split_bf16(x[:, q:q + gw], 2))
             for q in range(0, r_w, gw)], axis=1)

    tri = jnp.where(lax.broadcasted_iota(jnp.int32, (t, t), 0) >= lax.broadcasted_iota(jnp.int32, (t, t), 1),
                    1.0, 0.0).astype(BF16)

    def prologue(p, prev_row):
        row = lax.broadcasted_iota(jnp.int32, p.shape, 0)
        prev = jnp.where(row == 0, prev_row, pltpu.roll(p, 1, axis=0))
        ps = p + mu_ref[...] * (prev - p)
        r = ps[:, 0:r_w]
        k = ps[:, r_w:2 * r_w]
        v = ps[:, 2 * r_w:3 * r_w]
        wa_in = ps[:, 3 * r_w:3 * r_w + RANK_W + RANK_A]
        lg = ps[:, 3 * r_w + RANK_W + RANK_A:]
        log_w = -math.exp(-0.5) * _sigmoid(w0_ref[...] + _mm(jnp.tanh(wa_in), wup_ref[...]))
        a = _sigmoid(a0_ref[...] + _mm(wa_in, aup_ref[...]))
        g = _mm(_sigmoid(lg), gup_ref[...])
        kk = k * kk_ref[...]
        kk = kk * lax.rsqrt(jnp.maximum(head_sum(kk * kk), 1e-24))
        k2 = k * (1.0 + (a - 1.0) * ka_ref[...])
        b = kk * a
        cum = sum(jnp.dot(tri, piece, preferred_element_type=F32) for piece in _split_bf16(log_w, 3))
        e_cum = jnp.exp(cum)
        e_neg = jnp.exp(-cum)
        e_tail = jnp.exp(cum[t - 1:t, :] - cum)
        return dict(kq=kk * jnp.exp(cum - log_w), rq=r * e_cum, bd=b * e_neg, kd=k2 * e_neg, bt=b * e_tail,
                    kt=k2 * e_tail, v=v, p_last=e_cum[t - 1:t, :], bonus=head_sum(r * k2 * rk_ref[...]) * v, g=g)

    pro = []
    prev_row = prev_ref[...]
    for j in range(n_sub):
        p = p_ref[0, j * t:(j + 1) * t, :]
        pro.append(prologue(p, prev_row))
        prev_row = p[t - 1:t, :]
    prev_ref[...] = prev_row

    def block_rows(x, bw):
        xb = bf(x)
        blk = lax.broadcasted_iota(jnp.int32, xb.shape, 1) >> (bw.bit_length() - 1)
        return jnp.concatenate([jnp.where(blk == h, xb, jnp.zeros_like(xb)) for h in range(per)], axis=0)

    lane_blk = lax.broadcasted_iota(jnp.int32, (RWKV_HEAD, gw), 1) >> head_shift

    def diag_blocks(z):
        return sum(jnp.where(lane_blk == h, z[h * RWKV_HEAD:(h + 1) * RWKV_HEAD, :], 0.0) for h in range(per))

    quads = range(0, r_w, gw)
    probs = [(j, q) for j in range(n_sub) for q in quads]
    ps_ = range(len(probs))
    part = lambda name: [pro[j][name][:, q:q + gw] for j, q in probs]
    kq, rq, bd, kd, bt, kt, v = map(part, ("kq", "rq", "bd", "kd", "bt", "kt", "v"))
    row_m = lax.broadcasted_iota(jnp.int32, (t, mw), 0)
    col_m = lax.broadcasted_iota(jnp.int32, (t, mw), 1) & (t - 1)
    strict = row_m > col_m
    incl = row_m >= col_m

    vbd = [block_rows(v[i], RWKV_HEAD) for i in ps_]
    ms = [_mm(jnp.concatenate([kq[i], rq[i]], axis=0),
              jnp.concatenate([block_rows(bd[i], RWKV_HEAD), block_rows(kd[i], RWKV_HEAD)], axis=0), NT) for i in ps_]
    m_b = [jnp.where(strict, ms[i][:t, :mw], 0.0) for i in ps_]
    mkv = [_mm(jnp.where(strict, ms[i][:t, mw:], 0.0), vbd[i]) for i in ps_]
    l_b = [bf(jnp.where(incl, ms[i][t:, :mw], 0.0)) for i in ps_]
    l_k = [bf(jnp.where(incl, ms[i][t:, mw:], 0.0)) for i in ps_]
    invs = None
    k = 1
    while k < t:
        shift = (2 * k).bit_length() - 1
        join = ((row_m >> shift) == (col_m >> shift)) & ((row_m & k) != 0) & ((col_m & k) == 0)
        cs = [jnp.where(join, m, 0.0) for m in m_b]
        if k == 1:
            invs = [jnp.where(row_m == col_m, 1.0, 0.0) - c for c in cs]
        else:
            xs = [_mm(cs[i], block_rows(invs[i], t)) for i in ps_]
            invs = [invs[i] - _mm(invs[i], block_rows(xs[i], t)) for i in ps_]
        k *= 2
    invs = [bf(x) for x in invs]
    gk = [_mm(invs[i], block_rows(kq[i], RWKV_HEAD)) for i in ps_]
    u0 = [-_mm(invs[i], block_rows(mkv[i], RWKV_HEAD)) for i in ps_]
    rp = [bf(rq[i] - _mm(l_b[i], block_rows(gk[i], RWKV_HEAD))) for i in ps_]
    y0 = [_mm(jnp.concatenate([l_b[i], l_k[i]], axis=1),
              jnp.concatenate([block_rows(u0[i], RWKV_HEAD), vbd[i]], axis=0)) for i in ps_]
    phi = [block_rows(diag_blocks(_mm(gk[i], bt[i], TN)), RWKV_HEAD) for i in ps_]
    psi = [diag_blocks(_mm(jnp.concatenate([u0[i], v[i]], axis=0), jnp.concatenate([bt[i], kt[i]], axis=0), TN))
           for i in ps_]

    states = [state_ref[:, q:q + gw] for q in quads]
    n_q = len(states)
    inv_hd = 1.0 / RWKV_HEAD
    for j in range(n_sub):
        ys = []
        for qi in range(n_q):
            i = j * n_q + qi
            s = states[qi]
            ys.append(y0[i] + _mm(rp[i], block_rows(s, RWKV_HEAD), NT))
            states[qi] = s * pro[j]["p_last"][:, quads[qi]:quads[qi] + gw] - _mm(s, phi[i]) + psi[i]
        y = jnp.concatenate(ys, axis=1)
        dy = y - head_sum(y) * inv_hd
        var = head_sum(dy * dy) * inv_hd
        yn = dy * lax.rsqrt(var + GN_EPS) * gnw_ref[...] + gnb_ref[...]
        o_ref[0, j * t:(j + 1) * t, :] = (yn + pro[j]["bonus"]) * pro[j]["g"]
    for qi, q in enumerate(quads):
        state_ref[:, q:q + gw] = states[qi]

    @pl.when(c == pl.num_programs(1) - 1)
    def _():
        sout_ref[0] = state_ref[...]


def _rwkv(p3, shift_prev, s0, lw):
    bsz, seq, _ = p3.shape
    heads = s0.shape[1]
    r_w = heads * RWKV_HEAD
    cols = 3 * r_w + RANK_W + RANK_A + RANK_G
    t = min(RWKV_CHUNK, seq)
    assert seq % t == 0 and t & (t - 1) == 0
    n_sub = RWKV_SUBCHUNKS if seq % (RWKV_SUBCHUNKS * t) == 0 else 1
    tt = n_sub * t
    row = lambda x: x.reshape(1, -1).astype(F32)
    zeros = jnp.zeros((RANK_W, r_w), F32)
    wup = jnp.concatenate([lw['w_up'], zeros], axis=0).astype(BF16)
    aup = jnp.concatenate([zeros, lw['a_up']], axis=0).astype(BF16)
    vec = lambda n: pl.BlockSpec((1, n), lambda b, c: (0, 0))
    full = lambda a, bb: pl.BlockSpec((a, bb), lambda b, c: (0, 0))
    lanes_hk = lambda s: jnp.swapaxes(s.astype(F32), 1, 2).reshape(bsz, RWKV_HEAD, r_w)
    out, s_new = pl.pallas_call(
        functools.partial(_rwkv_kernel, t=t, n_sub=n_sub, heads=heads),
        grid=(bsz, seq // tt),
        in_specs=[
            pl.BlockSpec((1, tt, cols), lambda b, c: (b, c, 0)),
            pl.BlockSpec((1, 1, cols), lambda b, c: (b, 0, 0)),
            pl.BlockSpec((1, RWKV_HEAD, r_w), lambda b, c: (b, 0, 0)),
            vec(cols), vec(r_w), full(RANK_W + RANK_A, r_w), vec(r_w), full(RANK_W + RANK_A, r_w),
            full(RANK_G, r_w), vec(r_w), vec(r_w), vec(r_w), vec(r_w), vec(r_w),
        ],
        out_specs=[
            pl.BlockSpec((1, tt, r_w), lambda b, c: (b, c, 0)),
            pl.BlockSpec((1, RWKV_HEAD, r_w), lambda b, c: (b, 0, 0)),
        ],
        out_shape=[
            jax.ShapeDtypeStruct((bsz, seq, r_w), F32),
            jax.ShapeDtypeStruct((bsz, RWKV_HEAD, r_w), F32),
        ],
        scratch_shapes=[pltpu.VMEM((1, cols), F32), pltpu.VMEM((RWKV_HEAD, r_w), F32)],
        compiler_params=_params("parallel", "arbitrary"),
        name="rwkv",
    )(p3, shift_prev.astype(F32), lanes_hk(s0), row(lw['shift_mu']), row(lw['w0']), wup, row(lw['a0']), aup,
      lw['g_up'].astype(BF16), row(lw['k_k']), row(lw['k_a']), row(lw['r_k']), row(lw['gn_w']),
      row(lw['gn_b']))
    return out, jnp.swapaxes(s_new.reshape(bsz, RWKV_HEAD, heads, RWKV_HEAD), 1, 2)


def _s5_prep_kernel(are_ref, aim_ref, ldt_ref, btre_ref, btim_ref, cre_ref, cim_ref, d_ref,
                    kst_ref, wre_ref, wim_ref, vre_ref, vimn_ref, lre_ref, lim_ref, *, tc):
    for g in range(are_ref.shape[0]):
        a_re = are_ref[g]
        a_im = aim_ref[g]
        dt = jnp.exp(ldt_ref[g])
        mag = jnp.exp(a_re * dt)
        l_re = mag * jnp.cos(a_im * dt)
        l_im = mag * jnp.sin(a_im * dt)
        den = a_re * a_re + a_im * a_im
        x_re = l_re - 1.0
        co_re = (x_re * a_re + l_im * a_im) / den
        co_im = (l_im * a_re - x_re * a_im) / den
        pw = [(jnp.ones_like(l_re), jnp.zeros_like(l_re))]
        for _ in range(tc):
            q_re, q_im = pw[-1]
            pw.append((q_re * l_re - q_im * l_im, q_re * l_im + q_im * l_re))

        c_re = cre_ref[g]
        c_im = cim_ref[g]
        bt_re = btre_ref[g]
        bt_im = btim_ref[g]
        cc_re = c_re * co_re - c_im * co_im
        cc_im = c_re * co_im + c_im * co_re
        cl_re = jnp.concatenate([cc_re * q[0] - cc_im * q[1] for q in pw[:tc]], axis=0)
        cl_im = jnp.concatenate([cc_re * q[1] + cc_im * q[0] for q in pw[:tc]], axis=0)
        kst = _mm_f32(cl_re, bt_re, NT) - _mm_f32(cl_im, bt_im, NT)
        rr = lax.broadcasted_iota(jnp.int32, kst.shape, 0)
        cc = lax.broadcasted_iota(jnp.int32, kst.shape, 1)
        kst_ref[g] = kst + jnp.where(rr == cc, d_ref[g], 0.0)

        w_re, w_im = [], []
        for s in range(tc):
            q_re, q_im = pw[tc - 1 - s]
            f_re = q_re * co_re - q_im * co_im
            f_im = q_re * co_im + q_im * co_re
            w_re.append(bt_re * f_re - bt_im * f_im)
            w_im.append(bt_re * f_im + bt_im * f_re)
        wre_ref[g] = jnp.concatenate(w_re, axis=0)
        wim_ref[g] = jnp.concatenate(w_im, axis=0)
        vre_ref[g] = jnp.concatenate([c_re * q[0] - c_im * q[1] for q in pw[1:]], axis=0)
        vimn_ref[g] = jnp.concatenate([-(c_re * q[1] + c_im * q[0]) for q in pw[1:]], axis=0)
        lre_ref[g] = pw[tc][0]
        lim_ref[g] = pw[tc][1]


def _s5_prep(lw):
    groups, state = lw['a_re'].shape
    ch = S5_GROUP
    tc = S5_CHUNK
    g3 = lambda x, a, b: x.astype(F32).reshape(groups, a, b)
    gpb = S5_SUB if groups % S5_SUB == 0 else 1
    spec = lambda a, b: pl.BlockSpec((gpb, a, b), lambda g: (g, 0, 0))
    outs = pl.pallas_call(
        functools.partial(_s5_prep_kernel, tc=tc),
        grid=(groups // gpb,),
        in_specs=[spec(1, state), spec(1, state), spec(1, 1), spec(ch, state), spec(ch, state),
                  spec(ch, state), spec(ch, state), spec(1, ch)],
        out_specs=[spec(tc * ch, ch), spec(tc * ch, state), spec(tc * ch, state), spec(tc * ch, state),
                   spec(tc * ch, state), spec(1, state), spec(1, state)],
        out_shape=[jax.ShapeDtypeStruct((groups, tc * ch, ch), F32)]
        + [jax.ShapeDtypeStruct((groups, tc * ch, state), F32)] * 4
        + [jax.ShapeDtypeStruct((groups, 1, state), F32)] * 2,
        compiler_params=_params("parallel"),
        name="s5_prep",
    )(g3(lw['a_re'], 1, state), g3(lw['a_im'], 1, state), g3(lw['log_dt'], 1, 1),
      jnp.swapaxes(lw['b_re'], 1, 2).astype(F32), jnp.swapaxes(lw['b_im'], 1, 2).astype(F32),
      lw['c_re'].astype(F32), lw['c_im'].astype(F32), g3(lw['d'], 1, ch))
    kst, w_re, w_im, v_re, v_imn, l_re, l_im = outs
    k4 = kst.reshape(groups, tc, ch, ch)
    tt = jnp.arange(tc)
    tau = tt[:, None] - tt[None, :]
    blocks = jnp.where((tau >= 0)[None, :, :, None, None], k4[:, jnp.clip(tau, 0, tc - 1)], 0.0)
    toep = jnp.transpose(blocks, (0, 1, 3, 2, 4)).reshape(groups, tc * ch, tc * ch)
    return dict(
        toep=toep.astype(BF16),
        w=jnp.concatenate([w_re, w_im, w_im, w_re], axis=-1).astype(BF16),
        vt=jnp.concatenate([v_re, v_imn], axis=-1).astype(BF16),
        a16=jnp.concatenate([l_re, l_re], axis=-1),
        b16=jnp.concatenate([-l_im, l_im], axis=-1),
    )


def _lane_block_transpose(arrs, masks):
    n = len(arrs)
    rolled = [[a if j == 0 else pltpu.roll(a, S5_GROUP * j, axis=1) for j in range(n)] for a in arrs]
    outs = []
    for j in range(n):
        out = rolled[0][(0 - j) % n]
        for i in range(1, n):
            out = jnp.where(masks[i], rolled[i][(i - j) % n], out)
        outs.append(out)
    return outs


def _s5_kernel(x_ref, toep_ref, w_ref, vt_ref, a_ref, b_ref, h0_ref, h0s_ref,
               y_ref, hout_ref, x_st, s_st, ex_ref, es_ref, hs_ref, *, nct):
    ci = pl.program_id(2)
    sub = S5_SUB
    halves = S5_CHUNK // sub
    width = a_ref.shape[-1]

    @pl.when(ci == 0)
    def _():
        x_st[...] = h0_ref[0, 0]
        s_st[...] = h0s_ref[0, 0]

    lane = lax.broadcasted_iota(jnp.int32, (nct, LANES), 1)
    masks = [(lane >= S5_GROUP * j) & (lane < S5_GROUP * (j + 1)) for j in range(sub)]

    rows_of = lambda t: pl.ds(t, nct, stride=S5_CHUNK)
    assert halves == 2
    hi_bits = lambda x: lax.bitcast_convert_type(x.astype(BF16).astype(F32), jnp.int32)
    words = _lane_block_transpose(
        [lax.shift_right_logical(hi_bits(x_ref[0, rows_of(tt), :]), 16) | hi_bits(x_ref[0, rows_of(sub + tt), :])
         for tt in range(sub)], masks)
    halves_of = lambda w: (lax.shift_left(w, 16), w & jnp.int32(-65536))
    us = [jnp.concatenate([lax.bitcast_convert_type(h, F32) for h in halves_of(words[g])], axis=1).astype(BF16)
          for g in range(sub)]

    for g in range(sub):
        e = _mm(us[g], w_ref[g])
        ex_ref[pl.ds(g, nct, stride=sub), :] = e[:, :width]
        es_ref[pl.ds(g, nct, stride=sub), :] = e[:, width:]

    a = a_ref[0]
    b = b_ref[0]
    steps = min(8, nct)

    def body(i, carry):
        x, s = carry
        for j in range(steps):
            rows = pl.ds(pl.multiple_of((i * steps + j) * sub, sub), sub)
            hs_ref[rows, :] = x
            x, s = a * x + b * s + ex_ref[rows, :], a * s - b * x + es_ref[rows, :]
        return x, s

    x, s = lax.fori_loop(0, nct // steps, body, (x_st[...], s_st[...]))
    x_st[...] = x
    s_st[...] = s
    hout_ref[0, 0] = x

    ys = [_mm(us[g], toep_ref[g], NT) + _mm(hs_ref[pl.ds(g, nct, stride=sub), :], vt_ref[g], NT)
          for g in range(sub)]
    for hf in range(halves):
        outs = _lane_block_transpose([ys[g][:, hf * LANES:(hf + 1) * LANES] for g in range(sub)], masks)
        for tt in range(sub):
            y_ref[0, rows_of(hf * sub + tt), :] = outs[tt]


def _s5(p3, col0, h_re, h_im, ops):
    bsz, seq, n_in = p3.shape
    groups, state = h_re.shape[1:]
    ch, tc, sub = S5_GROUP, S5_CHUNK, S5_SUB
    assert seq % tc == 0 and groups % sub == 0 and col0 % LANES == 0 and tc % sub == 0
    n_chunks = seq // tc
    nct = min(S5_TILE_CHUNKS, n_chunks)
    assert n_chunks % nct == 0 and nct % min(8, nct) == 0
    gbs = groups // sub
    lane0 = col0 // LANES
    pack = lambda x, y: jnp.concatenate([x, y], axis=-1).astype(F32).reshape(bsz, gbs, sub, 2 * state)
    per_g = lambda last: pl.BlockSpec((sub, tc * ch, last), lambda gb, b, ci: (gb, 0, 0))
    vec = pl.BlockSpec((1, sub, 2 * state), lambda gb, b, ci: (gb, 0, 0))
    st = pl.BlockSpec((1, 1, sub, 2 * state), lambda gb, b, ci: (b, gb, 0, 0))
    y, h_out = pl.pallas_call(
        functools.partial(_s5_kernel, nct=nct),
        grid=(gbs, bsz, n_chunks // nct),
        in_specs=[pl.BlockSpec((1, nct * tc, LANES), lambda gb, b, ci: (b, ci, lane0 + gb)),
                  per_g(tc * ch), per_g(4 * state), per_g(2 * state), vec, vec, st, st],
        out_specs=[pl.BlockSpec((1, nct * tc, LANES), lambda gb, b, ci: (b, ci, gb)), st],
        out_shape=[jax.ShapeDtypeStruct((bsz, seq, groups * ch), F32),
                   jax.ShapeDtypeStruct((bsz, gbs, sub, 2 * state), F32)],
        scratch_shapes=[pltpu.VMEM((sub, 2 * state), F32), pltpu.VMEM((sub, 2 * state), F32),
                        pltpu.VMEM((nct * sub, 2 * state), F32), pltpu.VMEM((nct * sub, 2 * state), F32),
                        pltpu.VMEM((nct * sub, 2 * state), F32)],
        compiler_params=_params("parallel", "parallel", "arbitrary"),
        name="s5",
    )(p3, ops['toep'], ops['w'], ops['vt'],
      ops['a16'].reshape(gbs, sub, 2 * state), ops['b16'].reshape(gbs, sub, 2 * state),
      pack(h_re, h_im), pack(h_im, h_re))
    h_out = h_out.reshape(bsz, groups, 2 * state)
    return y, h_out[..., :state], h_out[..., state:]


def _gelu_tanh(x):
    return 0.5 * x * (1.0 + jnp.tanh(math.sqrt(2.0 / math.pi) * (x + 0.044715 * (x * x * x))))


def _mix_kernel(x_ref, rw_ref, y_ref, gw_ref, gb_ref, wo1_ref, wo2_ref, g_ref, b_ref, o_ref, *, alpha, rb):
    for r in range(0, x_ref.shape[0], rb):
        rows = slice(r, r + rb)
        y = _gelu_tanh(y_ref[rows, :])
        s5o = y * _sigmoid(_mm(y, gw_ref[...]) + gb_ref[...])
        mix = _mm(rw_ref[rows, :], wo1_ref[...]) + _mm(s5o, wo2_ref[...])
        o_ref[rows, :] = _ln(alpha * x_ref[rows, :] + mix, g_ref[...], b_ref[...])


def _mix(x, rw, y, glu_w, glu_b, w_out, layer, g, b, alpha):
    n, d = x.shape
    r_w = rw.shape[1]
    s_w = y.shape[1]
    tm = _token_tile(n)
    tile = lambda w: pl.BlockSpec((tm, w), lambda i: (i, 0))
    full = lambda a, bb: pl.BlockSpec((a, bb), lambda i: (0, 0), pipeline_mode=pl.Buffered(1))
    slab = lambda a, bb, k: pl.BlockSpec((None, a, bb), lambda i: (layer, k, 0), pipeline_mode=pl.Buffered(1))
    assert r_w == s_w
    return pl.pallas_call(
        functools.partial(_mix_kernel, alpha=alpha, rb=min(tm, ROW_BLOCK)),
        grid=(n // tm,),
        in_specs=[tile(d), tile(r_w), tile(s_w), slab(s_w, s_w, 0), full(1, s_w), slab(r_w, d, 0), slab(s_w, d, 1),
                  full(1, d), full(1, d)],
        out_specs=tile(d),
        out_shape=jax.ShapeDtypeStruct((n, d), F32),
        compiler_params=_params("parallel"),
        name="mix",
    )(x, rw, y, glu_w, glu_b.reshape(1, -1).astype(F32), w_out, w_out, g, b)


def _attn_kernel(x_ref, mk_ref, mv_ref, wq_ref, wo_ref, g_ref, b_ref, o_ref, *, alpha, heads, rb):
    d = x_ref.shape[-1]
    hd = d // heads
    mk = mk_ref[0]
    mv = mv_ref[0]
    for r in range(0, x_ref.shape[0], rb):
        x = x_ref[r:r + rb, :]
        q = _mm(x, wq_ref[...]) * (hd ** -0.5)
        outs = []
        for h in range(heads):
            sl = slice(h * hd, (h + 1) * hd)
            s = _mm(q[:, sl], mk[:, sl], NT)
            e = jnp.exp(s - jnp.max(s, axis=-1, keepdims=True))
            pr = e / jnp.sum(e, axis=-1, keepdims=True)
            outs.append(_mm(pr, mv[:, sl]))
        o = jnp.concatenate(outs, axis=-1)
        o_ref[r:r + rb, :] = _ln(alpha * x + _mm(o, wo_ref[...]), g_ref[...], b_ref[...])


def _attn(x, mk, mv, wq, wo, layer, g, b, alpha, seq):
    n, d = x.shape
    n_mem = mk.shape[1]
    tm = min(_token_tile(n), _token_tile(seq))
    per_batch = seq // tm
    tile = pl.BlockSpec((tm, d), lambda i: (i, 0))
    mem = pl.BlockSpec((1, n_mem, d), lambda i: (i // per_batch, 0, 0))
    full = lambda a, bb: pl.BlockSpec((a, bb), lambda i: (0, 0), pipeline_mode=pl.Buffered(1))
    slab = pl.BlockSpec((None, d, d), lambda i: (layer, 0, 0), pipeline_mode=pl.Buffered(1))
    return pl.pallas_call(
        functools.partial(_attn_kernel, alpha=alpha, heads=XATTN_HEADS, rb=min(tm, ROW_BLOCK)),
        grid=(n // tm,),
        in_specs=[tile, mem, mem, slab, slab, full(1, d), full(1, d)],
        out_specs=tile,
        out_shape=jax.ShapeDtypeStruct((n, d), F32),
        compiler_params=_params("parallel"),
        name="attn",
    )(x, mk.astype(BF16), mv.astype(BF16), wq, wo, g, b)


def _layer(x, mk, mv, shift_prev, s_rwkv, h_re, h_im, lw, wb, layer, s5_ops, alpha):
    bsz, seq, d = x.shape
    n = bsz * seq
    ln = lambda i: (lw['ln_g'][i].reshape(1, d).astype(F32), lw['ln_b'][i].reshape(1, d).astype(F32))
    cols = shift_prev.shape[-1]
    x1 = _ffn_ln(x.reshape(n, d), wb['f1g'], wb['f1u'], wb['f1d'], layer, *ln(0), alpha)
    p = _proj(x1, wb['w_in'], layer).reshape(bsz, seq, -1)
    rw, s_new = _rwkv(p, shift_prev, s_rwkv, lw)
    y, hr, hi = _s5(p, cols, h_re, h_im, s5_ops)
    x2 = _mix(x1, rw.reshape(n, -1), y.reshape(n, -1), wb['glu_w'], lw['glu_b'], wb['w_out'], layer, *ln(1), alpha)
    x3 = _attn(x2, mk.reshape(bsz, -1, d), mv.reshape(bsz, -1, d), wb['xq'], wb['xo'], layer, *ln(2), alpha, seq)
    x4 = _ffn_ln(x3, wb['f2g'], wb['f2u'], wb['f2d'], layer, *ln(3), alpha)
    return x4.reshape(bsz, seq, d), p[:, -1:, :cols], s_new, hr, hi


def kernel(x_prompt, x_sample, mem_prompt, cache_mem_k, cache_mem_v, state_rwkv, cache_shift, state_s5_re, state_s5_im, ln_g, ln_b, ffn1_gate, ffn1_up, ffn1_down, w_in, shift_mu, rwkv_w0, rwkv_w_up, rwkv_a0, rwkv_a_up, rwkv_g_up, rwkv_k_k, rwkv_k_a, rwkv_r_k, rwkv_gn_w, rwkv_gn_b, s5_a_re, s5_a_im, s5_log_dt, s5_b_re, s5_b_im, s5_c_re, s5_c_im, s5_d, s5_glu_w, s5_glu_b, w_mix_out, xattn_q, xattn_k, xattn_v, xattn_o, ffn2_gate, ffn2_up, ffn2_down):
    depth = ln_g.shape[0]
    bp, _, d = x_prompt.shape
    n_mem = mem_prompt.shape[1]
    heads, hd = state_rwkv.shape[2], state_rwkv.shape[3]
    cols = cache_shift.shape[-1]
    groups, state = state_s5_re.shape[2:]
    alpha = (2.0 * depth) ** 0.25
    bf = _to_bf16
    wb = dict(f1g=bf(ffn1_gate), f1u=bf(ffn1_up), f1d=bf(ffn1_down), w_in=bf(w_in), glu_w=bf(s5_glu_w),
              w_out=bf(w_mix_out), xq=bf(xattn_q), xk=bf(xattn_k), xv=bf(xattn_v), xo=bf(xattn_o),
              f2g=bf(ffn2_gate), f2u=bf(ffn2_up), f2d=bf(ffn2_down))
    xp, xs = x_prompt, x_sample
    outs = [[] for _ in range(10)]
    for l in range(depth):
        lw = dict(ln_g=ln_g[l], ln_b=ln_b[l], shift_mu=shift_mu[l], w0=rwkv_w0[l], w_up=rwkv_w_up[l], a0=rwkv_a0[l],
                  a_up=rwkv_a_up[l], g_up=rwkv_g_up[l], k_k=rwkv_k_k[l], k_a=rwkv_k_a[l], r_k=rwkv_r_k[l],
                  gn_w=rwkv_gn_w[l], gn_b=rwkv_gn_b[l], a_re=s5_a_re[l], a_im=s5_a_im[l], log_dt=s5_log_dt[l],
                  b_re=s5_b_re[l], b_im=s5_b_im[l], c_re=s5_c_re[l], c_im=s5_c_im[l], d=s5_d[l],
                  glu_b=s5_glu_b[l])
        s5_ops = _s5_prep(lw)
        mem2 = mem_prompt.reshape(bp * n_mem, d)
        mk_p = _proj(mem2, wb['xk'], l).reshape(bp, n_mem, XATTN_HEADS, d // XATTN_HEADS)
        mv_p = _proj(mem2, wb['xv'], l).reshape(bp, n_mem, XATTN_HEADS, d // XATTN_HEADS)
        xp, sh_p, rw_p, hr_p, hi_p = _layer(
            xp, mk_p, mv_p, jnp.zeros((bp, 1, cols), F32), jnp.zeros((bp, heads, hd, hd), F32),
            jnp.zeros((bp, groups, state), F32), jnp.zeros((bp, groups, state), F32), lw, wb, l, s5_ops, alpha)
        xs, sh_s, rw_s, hr_s, hi_s = _layer(
            xs, cache_mem_k[l], cache_mem_v[l], cache_shift[l], state_rwkv[l],
            state_s5_re[l], state_s5_im[l], lw, wb, l, s5_ops, alpha)
        for acc, val in zip(outs, (mk_p, mv_p, rw_p, sh_p, hr_p, hi_p, rw_s, sh_s, hr_s, hi_s)):
            acc.append(val)
    return (xp, xs) + tuple(jnp.stack(o) for o in outs)
```

```python
import functools
import math

import jax
import jax.numpy as jnp
from jax import lax
from jax.experimental import pallas as pl
from jax.experimental.pallas import tpu as pltpu

F32 = jnp.float32
BF16 = jnp.bfloat16

LN_EPS = 1e-5
GN_EPS = 64e-5
RWKV_HEAD = 64
RANK_W = 64
RANK_A = 64
RANK_G = 128
S5_GROUP = 16
S5_CHUNK = 16
RWKV_CHUNK = 64
RWKV_SUBCHUNKS = 4
XATTN_HEADS = 4
LANES = 128
MXU_TILE = 256
S5_SUB = LANES // S5_GROUP
S5_TILE_CHUNKS = 256
VMEM_LIMIT = 56 * 1024 * 1024
ROW_BLOCK = 256
CAST_BLOCK_BYTES = 6 * 1024 * 1024

NT = (((1,), (1,)), ((), ()))
TN = (((0,), (0,)), ((), ()))


def _mm(a, b, dims=None):
    a = a.astype(BF16)
    b = b.astype(BF16)
    if dims is None:
        return jnp.dot(a, b, preferred_element_type=F32)
    return lax.dot_general(a, b, dims, preferred_element_type=F32)


def _mm_f32(a, b, dims=None):
    if dims is None:
        return jnp.dot(a, b, preferred_element_type=F32, precision=lax.Precision.HIGHEST)
    return lax.dot_general(a, b, dims, preferred_element_type=F32, precision=lax.Precision.HIGHEST)


def _split_bf16(x, parts):
    pieces = []
    for _ in range(parts):
        piece = x.astype(BF16)
        pieces.append(piece)
        x = x - piece.astype(F32)
    return pieces


def _ln(z, g, b):
    mu = jnp.mean(z, axis=-1, keepdims=True)
    d = z - mu
    var = jnp.mean(d * d, axis=-1, keepdims=True)
    return d * lax.rsqrt(var + LN_EPS) * g + b


def _sigmoid(x):
    return 1.0 / (1.0 + jnp.exp(-x))


def _params(*sem):
    return pltpu.CompilerParams(dimension_semantics=sem, vmem_limit_bytes=VMEM_LIMIT)


def _token_tile(n):
    for t in (512, 256, 128, 64, 32, 16, 8):
        if n % t == 0:
            return t
    raise ValueError(f"token count {n} is not a multiple of 8")


def _cast_kernel(x_ref, o_ref):
    o_ref[...] = x_ref[...].astype(BF16)


def _to_bf16(w):
    depth, a, b = w.shape
    rows = depth * a
    tr = 8
    while rows % (2 * tr) == 0 and 2 * tr * b * 4 <= CAST_BLOCK_BYTES:
        tr *= 2
    out = pl.pallas_call(
        _cast_kernel,
        grid=(rows // tr,),
        in_specs=[pl.BlockSpec((tr, b), lambda i: (i, 0))],
        out_specs=pl.BlockSpec((tr, b), lambda i: (i, 0)),
        out_shape=jax.ShapeDtypeStruct((rows, b), BF16),
        compiler_params=_params("parallel"),
        name="to_bf16",
    )(w.reshape(rows, b))
    return out.reshape(depth, a, b)


def _ffn_ln_kernel(x_ref, wg_ref, wu_ref, wd_ref, g_ref, b_ref, o_ref, xb_ref, acc_ref, *,
                   alpha, n_tiles, n_blocks):
    i = pl.program_id(0)
    j = pl.program_id(1)
    slot = i % 2
    rb = o_ref.shape[0] // n_blocks

    def norm_previous_rows():
        r0 = pl.multiple_of((j % n_blocks) * rb, rb)
        z = 0.5 * acc_ref[1 - slot, pl.ds(r0, rb), :]
        o_ref[pl.ds(r0, rb), :] = _ln(z, g_ref[...], b_ref[...])

    @pl.when((i == 0) & (j == 0))
    def _():
        acc_ref[1] = jnp.zeros(acc_ref.shape[1:], F32)

    @pl.when((i < n_tiles) & (j == 0))
    def _():
        x = x_ref[...]
        xb_ref[...] = x.astype(BF16)
        acc_ref[slot] = (2.0 * alpha) * x

    @pl.when(i < n_tiles)
    def _():
        norm_previous_rows()
        xb = xb_ref[...]
        gate = jnp.dot(xb, wg_ref[...], preferred_element_type=F32)
        up = jnp.dot(xb, wu_ref[...], preferred_element_type=F32)
        h = gate * _sigmoid(gate) * up
        acc_ref[slot] += jnp.dot(h.astype(BF16), wd_ref[...], preferred_element_type=F32)

    @pl.when(i == n_tiles)
    def _():
        norm_previous_rows()


def _ffn_ln(x, wg, wu, wd, layer, g, b, alpha):
    n, d = x.shape
    dff = wg.shape[-1]
    tm = _token_tile(n)
    tf = 512 if dff % 512 == 0 else dff
    n_tiles = n // tm
    steps = dff // tf
    n_blocks = 1
    while 2 * n_blocks <= min(steps, 8) and tm % (16 * n_blocks) == 0:
        n_blocks *= 2
    tile_in = lambda i, j: (jnp.minimum(i, n_tiles - 1), 0)
    chunk = lambda i, j: jnp.where(i < n_tiles, j, steps - 1)
    return pl.pallas_call(
        functools.partial(_ffn_ln_kernel, alpha=alpha, n_tiles=n_tiles, n_blocks=n_blocks),
        grid=(n_tiles + 1, steps),
        in_specs=[
            pl.BlockSpec((tm, d), tile_in),
            pl.BlockSpec((None, d, tf), lambda i, j: (layer, 0, chunk(i, j))),
            pl.BlockSpec((None, d, tf), lambda i, j: (layer, 0, chunk(i, j))),
            pl.BlockSpec((None, tf, d), lambda i, j: (layer, chunk(i, j), 0)),
            pl.BlockSpec((1, d), lambda i, j: (0, 0)),
            pl.BlockSpec((1, d), lambda i, j: (0, 0)),
        ],
        out_specs=pl.BlockSpec((tm, d), lambda i, j: (jnp.maximum(i - 1, 0), 0)),
        out_shape=jax.ShapeDtypeStruct((n, d), F32),
        scratch_shapes=[pltpu.VMEM((tm, d), BF16), pltpu.VMEM((2, tm, d), F32)],
        compiler_params=_params("arbitrary", "arbitrary"),
        name="ffn_ln",
    )(x, wg, wu, wd, g, b)


def _proj_kernel(x_ref, w_ref, o_ref):
    o_ref[...] = _mm(x_ref[...], w_ref[...])


def _column_tile(nout, cap=2304):
    best = None
    for t in range(LANES, min(nout, cap) + 1, LANES):
        if nout % t == 0:
            best = t
    return best if best is not None else nout


def _proj(x, w, layer):
    n, d = x.shape
    nout = w.shape[-1]
    tm = _token_tile(n)
    tn = _column_tile(nout)
    return pl.pallas_call(
        _proj_kernel,
        grid=(nout // tn, n // tm),
        in_specs=[
            pl.BlockSpec((tm, d), lambda j, i: (i, 0)),
            pl.BlockSpec((None, d, tn), lambda j, i: (layer, 0, j)),
        ],
        out_specs=pl.BlockSpec((tm, tn), lambda j, i: (i, j)),
        out_shape=jax.ShapeDtypeStruct((n, nout), F32),
        compiler_params=_params("parallel", "parallel"),
        name="proj",
    )(x, w)


def _rwkv_kernel(p_ref, shift_ref, s0_ref, mu_ref, w0_ref, wup_ref, a0_ref, aup_ref, gup_ref,
                 kk_ref, ka_ref, rk_ref, gnw_ref, gnb_ref,
                 o_ref, sout_ref, prev_ref, state_ref, *, t, n_sub, heads):
    c = pl.program_id(1)
    r_w = heads * RWKV_HEAD
    gw = min(MXU_TILE, r_w)
    per = gw // RWKV_HEAD
    mw = per * t
    head_shift = RWKV_HEAD.bit_length() - 1
    bf = lambda x: x.astype(BF16)

    @pl.when(c == 0)
    def _():
        prev_ref[...] = shift_ref[0]
        state_ref[...] = s0_ref[0]

    same_head = ((lax.broadcasted_iota(jnp.int32, (gw, gw), 0) >> head_shift)
                 == (lax.broadcasted_iota(jnp.int32, (gw, gw), 1) >> head_shift))
    ones_bd = jnp.where(same_head, 1.0, 0.0).astype(BF16)

    def head_sum(x):
        return jnp.concatenate(
            [sum(jnp.dot(piece, ones_bd, preferred_element_type=F32) for piece in _split_bf16(x[:, q:q + gw], 2))
             for q in range(0, r_w, gw)], axis=1)

    tri = jnp.where(lax.broadcasted_iota(jnp.int32, (t, t), 0) >= lax.broadcasted_iota(jnp.int32, (t, t), 1),
                    1.0, 0.0).astype(BF16)

    def prologue(p, prev_row):
        row = lax.broadcasted_iota(jnp.int32, p.shape, 0)
        prev = jnp.where(row == 0, prev_row, pltpu.roll(p, 1, axis=0))
        ps = p + mu_ref[...] * (prev - p)
        r = ps[:, 0:r_w]
        k = ps[:, r_w:2 * r_w]
        v = ps[:, 2 * r_w:3 * r_w]
        wa_in = ps[:, 3 * r_w:3 * r_w + RANK_W + RANK_A]
        lg = ps[:, 3 * r_w + RANK_W + RANK_A:]
        log_w = -math.exp(-0.5) * _sigmoid(w0_ref[...] + _mm(jnp.tanh(wa_in), wup_ref[...]))
        a = _sigmoid(a0_ref[...] + _mm(wa_in, aup_ref[...]))
        g = _mm(_sigmoid(lg), gup_ref[...])
        kk = k * kk_ref[...]
        kk = kk * lax.rsqrt(jnp.maximum(head_sum(kk * kk), 1e-24))
        k2 = k * (1.0 + (a - 1.0) * ka_ref[...])
        b = kk * a
        cum = sum(jnp.dot(tri, piece, preferred_element_type=F32) for piece in _split_bf16(log_w, 3))
        e_cum = jnp.exp(cum)
        e_neg = jnp.exp(-cum)
        e_tail = jnp.exp(cum[t - 1:t, :] - cum)
        return dict(kq=kk * jnp.exp(cum - log_w), rq=r * e_cum, bd=b * e_neg, kd=k2 * e_neg, bt=b * e_tail,
                    kt=k2 * e_tail, v=v, p_last=e_cum[t - 1:t, :], bonus=head_sum(r * k2 * rk_ref[...]) * v, g=g)

    pro = []
    prev_row = prev_ref[...]
    for j in range(n_sub):
        p = p_ref[0, j * t:(j + 1) * t, :]
        pro.append(prologue(p, prev_row))
        prev_row = p[t - 1:t, :]
    prev_ref[...] = prev_row

    def block_rows(x, bw):
        xb = bf(x)
        blk = lax.broadcasted_iota(jnp.int32, xb.shape, 1) >> (bw.bit_length() - 1)
        return jnp.concatenate([jnp.where(blk == h, xb, jnp.zeros_like(xb)) for h in range(per)], axis=0)

    lane_blk = lax.broadcasted_iota(jnp.int32, (RWKV_HEAD, gw), 1) >> head_shift

    def diag_blocks(z):
        return sum(jnp.where(lane_blk == h, z[h * RWKV_HEAD:(h + 1) * RWKV_HEAD, :], 0.0) for h in range(per))

    quads = range(0, r_w, gw)
    probs = [(j, q) for j in range(n_sub) for q in quads]
    ps_ = range(len(probs))
    part = lambda name: [pro[j][name][:, q:q + gw] for j, q in probs]
    kq, rq, bd, kd, bt, kt, v = map(part, ("kq", "rq", "bd", "kd", "bt", "kt", "v"))
    row_m = lax.broadcasted_iota(jnp.int32, (t, mw), 0)
    col_m = lax.broadcasted_iota(jnp.int32, (t, mw), 1) & (t - 1)
    strict = row_m > col_m
    incl = row_m >= col_m

    vbd = [block_rows(v[i], RWKV_HEAD) for i in ps_]
    ms = [_mm(jnp.concatenate([kq[i], rq[i]], axis=0),
              jnp.concatenate([block_rows(bd[i], RWKV_HEAD), block_rows(kd[i], RWKV_HEAD)], axis=0), NT) for i in ps_]
    m_b = [jnp.where(strict, ms[i][:t, :mw], 0.0) for i in ps_]
    mkv = [_mm(jnp.where(strict, ms[i][:t, mw:], 0.0), vbd[i]) for i in ps_]
    l_b = [bf(jnp.where(incl, ms[i][t:, :mw], 0.0)) for i in ps_]
    l_k = [bf(jnp.where(incl, ms[i][t:, mw:], 0.0)) for i in ps_]
    invs = None
    k = 1
    while k < t:
        shift = (2 * k).bit_length() - 1
        join = ((row_m >> shift) == (col_m >> shift)) & ((row_m & k) != 0) & ((col_m & k) == 0)
        cs = [jnp.where(join, m, 0.0) for m in m_b]
        if k == 1:
            invs = [jnp.where(row_m == col_m, 1.0, 0.0) - c for c in cs]
        else:
            xs = [_mm(cs[i], block_rows(invs[i], t)) for i in ps_]
            invs = [invs[i] - _mm(invs[i], block_rows(xs[i], t)) for i in ps_]
        k *= 2
    invs = [bf(x) for x in invs]
    gk = [_mm(invs[i], block_rows(kq[i], RWKV_HEAD)) for i in ps_]
    u0 = [-_mm(invs[i], block_rows(mkv[i], RWKV_HEAD)) for i in ps_]
    rp = [bf(rq[i] - _mm(l_b[i], block_rows(gk[i], RWKV_HEAD))) for i in ps_]
    y0 = [_mm(jnp.concatenate([l_b[i], l_k[i]], axis=1),
              jnp.concatenate([block_rows(u0[i], RWKV_HEAD), vbd[i]], axis=0)) for i in ps_]
    phi = [block_rows(diag_blocks(_mm(gk[i], bt[i], TN)), RWKV_HEAD) for i in ps_]
    psi = [diag_blocks(_mm(jnp.concatenate([u0[i], v[i]], axis=0), jnp.concatenate([bt[i], kt[i]], axis=0), TN))
           for i in ps_]

    states = [state_ref[:, q:q + gw] for q in quads]
    n_q = len(states)
    inv_hd = 1.0 / RWKV_HEAD
    for j in range(n_sub):
        ys = []
        for qi in range(n_q):
            i = j * n_q + qi
            s = states[qi]
            ys.append(y0[i] + _mm(rp[i], block_rows(s, RWKV_HEAD), NT))
            states[qi] = s * pro[j]["p_last"][:, quads[qi]:quads[qi] + gw] - _mm(s, phi[i]) + psi[i]
        y = jnp.concatenate(ys, axis=1)
        dy = y - head_sum(y) * inv_hd
        var = head_sum(dy * dy) * inv_hd
        yn = dy * lax.rsqrt(var + GN_EPS) * gnw_ref[...] + gnb_ref[...]
        o_ref[0, j * t:(j + 1) * t, :] = (yn + pro[j]["bonus"]) * pro[j]["g"]
    for qi, q in enumerate(quads):
        state_ref[:, q:q + gw] = states[qi]

    @pl.when(c == pl.num_programs(1) - 1)
    def _():
        sout_ref[0] = state_ref[...]


def _rwkv(p3, shift_prev, s0, lw):
    bsz, seq, _ = p3.shape
    heads = s0.shape[1]
    r_w = heads * RWKV_HEAD
    cols = 3 * r_w + RANK_W + RANK_A + RANK_G
    t = min(RWKV_CHUNK, seq)
    assert seq % t == 0 and t & (t - 1) == 0
    n_sub = RWKV_SUBCHUNKS if seq % (RWKV_SUBCHUNKS * t) == 0 else 1
    tt = n_sub * t
    row = lambda x: x.reshape(1, -1).astype(F32)
    zeros = jnp.zeros((RANK_W, r_w), F32)
    wup = jnp.concatenate([lw['w_up'], zeros], axis=0).astype(BF16)
    aup = jnp.concatenate([zeros, lw['a_up']], axis=0).astype(BF16)
    vec = lambda n: pl.BlockSpec((1, n), lambda b, c: (0, 0))
    full = lambda a, bb: pl.BlockSpec((a, bb), lambda b, c: (0, 0))
    lanes_hk = lambda s: jnp.swapaxes(s.astype(F32), 1, 2).reshape(bsz, RWKV_HEAD, r_w)
    out, s_new = pl.pallas_call(
        functools.partial(_rwkv_kernel, t=t, n_sub=n_sub, heads=heads),
        grid=(bsz, seq // tt),
        in_specs=[
            pl.BlockSpec((1, tt, cols), lambda b, c: (b, c, 0)),
            pl.BlockSpec((1, 1, cols), lambda b, c: (b, 0, 0)),
            pl.BlockSpec((1, RWKV_HEAD, r_w), lambda b, c: (b, 0, 0)),
            vec(cols), vec(r_w), full(RANK_W + RANK_A, r_w), vec(r_w), full(RANK_W + RANK_A, r_w),
            full(RANK_G, r_w), vec(r_w), vec(r_w), vec(r_w), vec(r_w), vec(r_w),
        ],
        out_specs=[
            pl.BlockSpec((1, tt, r_w), lambda b, c: (b, c, 0)),
            pl.BlockSpec((1, RWKV_HEAD, r_w), lambda b, c: (b, 0, 0)),
        ],
        out_shape=[
            jax.ShapeDtypeStruct((bsz, seq, r_w), F32),
            jax.ShapeDtypeStruct((bsz, RWKV_HEAD, r_w), F32),
        ],
        scratch_shapes=[pltpu.VMEM((1, cols), F32), pltpu.VMEM((RWKV_HEAD, r_w), F32)],
        compiler_params=_params("parallel", "arbitrary"),
        name="rwkv",
    )(p3, shift_prev.astype(F32), lanes_hk(s0), row(lw['shift_mu']), row(lw['w0']), wup, row(lw['a0']), aup,
      lw['g_up'].astype(BF16), row(lw['k_k']), row(lw['k_a']), row(lw['r_k']), row(lw['gn_w']),
      row(lw['gn_b']))
    return out, jnp.swapaxes(s_new.reshape(bsz, RWKV_HEAD, heads, RWKV_HEAD), 1, 2)


def _s5_prep_kernel(are_ref, aim_ref, ldt_ref, btre_ref, btim_ref, cre_ref, cim_ref, d_ref,
                    kst_ref, wre_ref, wim_ref, vre_ref, vimn_ref, lre_ref, lim_ref, *, tc):
    for g in range(are_ref.shape[0]):
        a_re = are_ref[g]
        a_im = aim_ref[g]
        dt = jnp.exp(ldt_ref[g])
        mag = jnp.exp(a_re * dt)
        l_re = mag * jnp.cos(a_im * dt)
        l_im = mag * jnp.sin(a_im * dt)
        den = a_re * a_re + a_im * a_im
        x_re = l_re - 1.0
        co_re = (x_re * a_re + l_im * a_im) / den
        co_im = (l_im * a_re - x_re * a_im) / den
        pw = [(jnp.ones_like(l_re), jnp.zeros_like(l_re))]
        for _ in range(tc):
            q_re, q_im = pw[-1]
            pw.append((q_re * l_re - q_im * l_im, q_re * l_im + q_im * l_re))

        c_re = cre_ref[g]
        c_im = cim_ref[g]
        bt_re = btre_ref[g]
        bt_im = btim_ref[g]
        cc_re = c_re * co_re - c_im * co_im
        cc_im = c_re * co_im + c_im * co_re
        cl_re = jnp.concatenate([cc_re * q[0] - cc_im * q[1] for q in pw[:tc]], axis=0)
        cl_im = jnp.concatenate([cc_re * q[1] + cc_im * q[0] for q in pw[:tc]], axis=0)
        kst = _mm_f32(cl_re, bt_re, NT) - _mm_f32(cl_im, bt_im, NT)
        rr = lax.broadcasted_iota(jnp.int32, kst.shape, 0)
        cc = lax.broadcasted_iota(jnp.int32, kst.shape, 1)
        kst_ref[g] = kst + jnp.where(rr == cc, d_ref[g], 0.0)

        w_re, w_im = [], []
        for s in range(tc):
            q_re, q_im = pw[tc - 1 - s]
            f_re = q_re * co_re - q_im * co_im
            f_im = q_re * co_im + q_im * co_re
            w_re.append(bt_re * f_re - bt_im * f_im)
            w_im.append(bt_re * f_im + bt_im * f_re)
        wre_ref[g] = jnp.concatenate(w_re, axis=0)
        wim_ref[g] = jnp.concatenate(w_im, axis=0)
        vre_ref[g] = jnp.concatenate([c_re * q[0] - c_im * q[1] for q in pw[1:]], axis=0)
        vimn_ref[g] = jnp.concatenate([-(c_re * q[1] + c_im * q[0]) for q in pw[1:]], axis=0)
        lre_ref[g] = pw[tc][0]
        lim_ref[g] = pw[tc][1]


def _s5_prep(lw):
    groups, state = lw['a_re'].shape
    ch = S5_GROUP
    tc = S5_CHUNK
    g3 = lambda x, a, b: x.astype(F32).reshape(groups, a, b)
    gpb = S5_SUB if groups % S5_SUB == 0 else 1
    spec = lambda a, b: pl.BlockSpec((gpb, a, b), lambda g: (g, 0, 0))
    outs = pl.pallas_call(
        functools.partial(_s5_prep_kernel, tc=tc),
        grid=(groups // gpb,),
        in_specs=[spec(1, state), spec(1, state), spec(1, 1), spec(ch, state), spec(ch, state),
                  spec(ch, state), spec(ch, state), spec(1, ch)],
        out_specs=[spec(tc * ch, ch), spec(tc * ch, state), spec(tc * ch, state), spec(tc * ch, state),
                   spec(tc * ch, state), spec(1, state), spec(1, state)],
        out_shape=[jax.ShapeDtypeStruct((groups, tc * ch, ch), F32)]
        + [jax.ShapeDtypeStruct((groups, tc * ch, state), F32)] * 4
        + [jax.ShapeDtypeStruct((groups, 1, state), F32)] * 2,
        compiler_params=_params("parallel"),
        name="s5_prep",
    )(g3(lw['a_re'], 1, state), g3(lw['a_im'], 1, state), g3(lw['log_dt'], 1, 1),
      jnp.swapaxes(lw['b_re'], 1, 2).astype(F32), jnp.swapaxes(lw['b_im'], 1, 2).astype(F32),
      lw['c_re'].astype(F32), lw['c_im'].astype(F32), g3(lw['d'], 1, ch))
    kst, w_re, w_im, v_re, v_imn, l_re, l_im = outs
    k4 = kst.reshape(groups, tc, ch, ch)
    tt = jnp.arange(tc)
    tau = tt[:, None] - tt[None, :]
    blocks = jnp.where((tau >= 0)[None, :, :, None, None], k4[:, jnp.clip(tau, 0, tc - 1)], 0.0)
    toep = jnp.transpose(blocks, (0, 1, 3, 2, 4)).reshape(groups, tc * ch, tc * ch)
    return dict(
        toep=toep.astype(BF16),
        w=jnp.concatenate([w_re, w_im, w_im, w_re], axis=-1).astype(BF16),
        vt=jnp.concatenate([v_re, v_imn], axis=-1).astype(BF16),
        a16=jnp.concatenate([l_re, l_re], axis=-1),
        b16=jnp.concatenate([-l_im, l_im], axis=-1),
    )


def _lane_block_transpose(arrs):
    n = len(arrs)
    assert n * S5_GROUP == LANES and n & (n - 1) == 0
    blk = lax.broadcasted_iota(jnp.int32, arrs[0].shape, 1) >> (S5_GROUP.bit_length() - 1)
    cur = list(arrs)
    b = n // 2
    while b:
        upper = (blk & b) != 0
        nxt = list(cur)
        for i in range(n):
            if i & b:
                continue
            lo, hi = cur[i], cur[i + b]
            nxt[i] = jnp.where(upper, pltpu.roll(hi, S5_GROUP * b, axis=1), lo)
            nxt[i + b] = jnp.where(upper, hi, pltpu.roll(lo, LANES - S5_GROUP * b, axis=1))
        cur = nxt
        b //= 2
    return cur


def _s5_kernel(x_ref, toep_ref, w_ref, vt_ref, a_ref, b_ref, h0_ref, h0s_ref,
               y_ref, hout_ref, x_st, s_st, ex_ref, es_ref, hs_ref, *, nct):
    ci = pl.program_id(2)
    sub = S5_SUB
    halves = S5_CHUNK // sub
    width = a_ref.shape[-1]

    @pl.when(ci == 0)
    def _():
        x_st[...] = h0_ref[0, 0]
        s_st[...] = h0s_ref[0, 0]

    rows_of = lambda t: pl.ds(t, nct, stride=S5_CHUNK)
    assert halves == 2
    hi_bits = lambda x: lax.bitcast_convert_type(x.astype(BF16).astype(F32), jnp.int32)
    words = _lane_block_transpose(
        [lax.shift_right_logical(hi_bits(x_ref[0, rows_of(tt), :]), 16) | hi_bits(x_ref[0, rows_of(sub + tt), :])
         for tt in range(sub)])
    halves_of = lambda w: (lax.shift_left(w, 16), w & jnp.int32(-65536))
    us = [jnp.concatenate([lax.bitcast_convert_type(h, F32) for h in halves_of(words[g])], axis=1).astype(BF16)
          for g in range(sub)]

    for g in range(sub):
        e = _mm(us[g], w_ref[g])
        ex_ref[pl.ds(g, nct, stride=sub), :] = e[:, :width]
        es_ref[pl.ds(g, nct, stride=sub), :] = e[:, width:]

    a = a_ref[0]
    b = b_ref[0]
    steps = min(8, nct)

    def body(i, carry):
        x, s = carry
        for j in range(steps):
            rows = pl.ds(pl.multiple_of((i * steps + j) * sub, sub), sub)
            hs_ref[rows, :] = x
            x, s = a * x + b * s + ex_ref[rows, :], a * s - b * x + es_ref[rows, :]
        return x, s

    x, s = lax.fori_loop(0, nct // steps, body, (x_st[...], s_st[...]))
    x_st[...] = x
    s_st[...] = s
    hout_ref[0, 0] = x

    ys = [_mm(us[g], toep_ref[g], NT) + _mm(hs_ref[pl.ds(g, nct, stride=sub), :], vt_ref[g], NT)
          for g in range(sub)]
    for hf in range(halves):
        outs = _lane_block_transpose([ys[g][:, hf * LANES:(hf + 1) * LANES] for g in range(sub)])
        for tt in range(sub):
            y_ref[0, rows_of(hf * sub + tt), :] = outs[tt]


def _s5(p3, col0, h_re, h_im, ops):
    bsz, seq, n_in = p3.shape
    groups, state = h_re.shape[1:]
    ch, tc, sub = S5_GROUP, S5_CHUNK, S5_SUB
    assert seq % tc == 0 and groups % sub == 0 and col0 % LANES == 0 and tc % sub == 0
    n_chunks = seq // tc
    nct = min(S5_TILE_CHUNKS, n_chunks)
    assert n_chunks % nct == 0 and nct % min(8, nct) == 0
    gbs = groups // sub
    lane0 = col0 // LANES
    pack = lambda x, y: jnp.concatenate([x, y], axis=-1).astype(F32).reshape(bsz, gbs, sub, 2 * state)
    per_g = lambda last: pl.BlockSpec((sub, tc * ch, last), lambda gb, b, ci: (gb, 0, 0))
    vec = pl.BlockSpec((1, sub, 2 * state), lambda gb, b, ci: (gb, 0, 0))
    st = pl.BlockSpec((1, 1, sub, 2 * state), lambda gb, b, ci: (b, gb, 0, 0))
    y, h_out = pl.pallas_call(
        functools.partial(_s5_kernel, nct=nct),
        grid=(gbs, bsz, n_chunks // nct),
        in_specs=[pl.BlockSpec((1, nct * tc, LANES), lambda gb, b, ci: (b, ci, lane0 + gb)),
                  per_g(tc * ch), per_g(4 * state), per_g(2 * state), vec, vec, st, st],
        out_specs=[pl.BlockSpec((1, nct * tc, LANES), lambda gb, b, ci: (b, ci, gb)), st],
        out_shape=[jax.ShapeDtypeStruct((bsz, seq, groups * ch), F32),
                   jax.ShapeDtypeStruct((bsz, gbs, sub, 2 * state), F32)],
        scratch_shapes=[pltpu.VMEM((sub, 2 * state), F32), pltpu.VMEM((sub, 2 * state), F32),
                        pltpu.VMEM((nct * sub, 2 * state), F32), pltpu.VMEM((nct * sub, 2 * state), F32),
                        pltpu.VMEM((nct * sub, 2 * state), F32)],
        compiler_params=_params("parallel", "parallel", "arbitrary"),
        name="s5",
    )(p3, ops['toep'], ops['w'], ops['vt'],
      ops['a16'].reshape(gbs, sub, 2 * state), ops['b16'].reshape(gbs, sub, 2 * state),
      pack(h_re, h_im), pack(h_im, h_re))
    h_out = h_out.reshape(bsz, groups, 2 * state)
    return y, h_out[..., :state], h_out[..., state:]


def _gelu_tanh(x):
    return 0.5 * x * (1.0 + jnp.tanh(math.sqrt(2.0 / math.pi) * (x + 0.044715 * (x * x * x))))


def _mix_kernel(x_ref, rw_ref, y_ref, gw_ref, gb_ref, wo1_ref, wo2_ref, g_ref, b_ref, o_ref, *, alpha, rb):
    for r in range(0, x_ref.shape[0], rb):
        rows = slice(r, r + rb)
        y = _gelu_tanh(y_ref[rows, :])
        s5o = y * _sigmoid(_mm(y, gw_ref[...]) + gb_ref[...])
        mix = _mm(rw_ref[rows, :], wo1_ref[...]) + _mm(s5o, wo2_ref[...])
        o_ref[rows, :] = _ln(alpha * x_ref[rows, :] + mix, g_ref[...], b_ref[...])


def _mix(x, rw, y, glu_w, glu_b, w_out, layer, g, b, alpha):
    n, d = x.shape
    r_w = rw.shape[1]
    s_w = y.shape[1]
    tm = _token_tile(n)
    tile = lambda w: pl.BlockSpec((tm, w), lambda i: (i, 0))
    full = lambda a, bb: pl.BlockSpec((a, bb), lambda i: (0, 0), pipeline_mode=pl.Buffered(1))
    slab = lambda a, bb, k: pl.BlockSpec((None, a, bb), lambda i: (layer, k, 0), pipeline_mode=pl.Buffered(1))
    assert r_w == s_w
    return pl.pallas_call(
        functools.partial(_mix_kernel, alpha=alpha, rb=min(tm, ROW_BLOCK)),
        grid=(n // tm,),
        in_specs=[tile(d), tile(r_w), tile(s_w), slab(s_w, s_w, 0), full(1, s_w), slab(r_w, d, 0), slab(s_w, d, 1),
                  full(1, d), full(1, d)],
        out_specs=tile(d),
        out_shape=jax.ShapeDtypeStruct((n, d), F32),
        compiler_params=_params("parallel"),
        name="mix",
    )(x, rw, y, glu_w, glu_b.reshape(1, -1).astype(F32), w_out, w_out, g, b)


def _attn_kernel(x_ref, mk_ref, mv_ref, wq_ref, wo_ref, g_ref, b_ref, o_ref, *, alpha, heads, rb):
    d = x_ref.shape[-1]
    hd = d // heads
    mk = mk_ref[0]
    mv = mv_ref[0]
    for r in range(0, x_ref.shape[0], rb):
        x = x_ref[r:r + rb, :]
        q = _mm(x, wq_ref[...]) * (hd ** -0.5)
        outs = []
        for h in range(heads):
            sl = slice(h * hd, (h + 1) * hd)
            s = _mm(q[:, sl], mk[:, sl], NT)
            e = jnp.exp(s - jnp.max(s, axis=-1, keepdims=True))
            pr = e / jnp.sum(e, axis=-1, keepdims=True)
            outs.append(_mm(pr, mv[:, sl]))
        o = jnp.concatenate(outs, axis=-1)
        o_ref[r:r + rb, :] = _ln(alpha * x + _mm(o, wo_ref[...]), g_ref[...], b_ref[...])


def _attn(x, mk, mv, wq, wo, layer, g, b, alpha, seq):
    n, d = x.shape
    n_mem = mk.shape[1]
    tm = min(_token_tile(n), _token_tile(seq))
    per_batch = seq // tm
    tile = pl.BlockSpec((tm, d), lambda i: (i, 0))
    mem = pl.BlockSpec((1, n_mem, d), lambda i: (i // per_batch, 0, 0))
    full = lambda a, bb: pl.BlockSpec((a, bb), lambda i: (0, 0), pipeline_mode=pl.Buffered(1))
    slab = pl.BlockSpec((None, d, d), lambda i: (layer, 0, 0), pipeline_mode=pl.Buffered(1))
    return pl.pallas_call(
        functools.partial(_attn_kernel, alpha=alpha, heads=XATTN_HEADS, rb=min(tm, ROW_BLOCK)),
        grid=(n // tm,),
        in_specs=[tile, mem, mem, slab, slab, full(1, d), full(1, d)],
        out_specs=tile,
        out_shape=jax.ShapeDtypeStruct((n, d), F32),
        compiler_params=_params("parallel"),
        name="attn",
    )(x, mk.astype(BF16), mv.astype(BF16), wq, wo, g, b)


def _layer(x, mk, mv, shift_prev, s_rwkv, h_re, h_im, lw, wb, layer, s5_ops, alpha):
    bsz, seq, d = x.shape
    n = bsz * seq
    ln = lambda i: (lw['ln_g'][i].reshape(1, d).astype(F32), lw['ln_b'][i].reshape(1, d).astype(F32))
    cols = shift_prev.shape[-1]
    x1 = _ffn_ln(x.reshape(n, d), wb['f1g'], wb['f1u'], wb['f1d'], layer, *ln(0), alpha)
    p = _proj(x1, wb['w_in'], layer).reshape(bsz, seq, -1)
    rw, s_new = _rwkv(p, shift_prev, s_rwkv, lw)
    y, hr, hi = _s5(p, cols, h_re, h_im, s5_ops)
    x2 = _mix(x1, rw.reshape(n, -1), y.reshape(n, -1), wb['glu_w'], lw['glu_b'], wb['w_out'], layer, *ln(1), alpha)
    x3 = _attn(x2, mk.reshape(bsz, -1, d), mv.reshape(bsz, -1, d), wb['xq'], wb['xo'], layer, *ln(2), alpha, seq)
    x4 = _ffn_ln(x3, wb['f2g'], wb['f2u'], wb['f2d'], layer, *ln(3), alpha)
    return x4.reshape(bsz, seq, d), p[:, -1:, :cols], s_new, hr, hi


def kernel(x_prompt, x_sample, mem_prompt, cache_mem_k, cache_mem_v, state_rwkv, cache_shift, state_s5_re, state_s5_im, ln_g, ln_b, ffn1_gate, ffn1_up, ffn1_down, w_in, shift_mu, rwkv_w0, rwkv_w_up, rwkv_a0, rwkv_a_up, rwkv_g_up, rwkv_k_k, rwkv_k_a, rwkv_r_k, rwkv_gn_w, rwkv_gn_b, s5_a_re, s5_a_im, s5_log_dt, s5_b_re, s5_b_im, s5_c_re, s5_c_im, s5_d, s5_glu_w, s5_glu_b, w_mix_out, xattn_q, xattn_k, xattn_v, xattn_o, ffn2_gate, ffn2_up, ffn2_down):
    depth = ln_g.shape[0]
    bp, _, d = x_prompt.shape
    n_mem = mem_prompt.shape[1]
    heads, hd = state_rwkv.shape[2], state_rwkv.shape[3]
    cols = cache_shift.shape[-1]
    groups, state = state_s5_re.shape[2:]
    alpha = (2.0 * depth) ** 0.25
    bf = _to_bf16
    wb = dict(f1g=bf(ffn1_gate), f1u=bf(ffn1_up), f1d=bf(ffn1_down), w_in=bf(w_in), glu_w=bf(s5_glu_w),
              w_out=bf(w_mix_out), xq=bf(xattn_q), xk=bf(xattn_k), xv=bf(xattn_v), xo=bf(xattn_o),
              f2g=bf(ffn2_gate), f2u=bf(ffn2_up), f2d=bf(ffn2_down))
    xp, xs = x_prompt, x_sample
    outs = [[] for _ in range(10)]
    for l in range(depth):
        lw = dict(ln_g=ln_g[l], ln_b=ln_b[l], shift_mu=shift_mu[l], w0=rwkv_w0[l], w_up=rwkv_w_up[l], a0=rwkv_a0[l],
                  a_up=rwkv_a_up[l], g_up=rwkv_g_up[l], k_k=rwkv_k_k[l], k_a=rwkv_k_a[l], r_k=rwkv_r_k[l],
                  gn_w=rwkv_gn_w[l], gn_b=rwkv_gn_b[l], a_re=s5_a_re[l], a_im=s5_a_im[l], log_dt=s5_log_dt[l],
                  b_re=s5_b_re[l], b_im=s5_b_im[l], c_re=s5_c_re[l], c_im=s5_c_im[l], d=s5_d[l],
                  glu_b=s5_glu_b[l])
        s5_ops = _s5_prep(lw)
        mem2 = mem_prompt.reshape(bp * n_mem, d)
        mk_p = _proj(mem2, wb['xk'], l).reshape(bp, n_mem, XATTN_HEADS, d // XATTN_HEADS)
        mv_p = _proj(mem2, wb['xv'], l).reshape(bp, n_mem, XATTN_HEADS, d // XATTN_HEADS)
        xp, sh_p, rw_p, hr_p, hi_p = _layer(
            xp, mk_p, mv_p, jnp.zeros((bp, 1, cols), F32), jnp.zeros((bp, heads, hd, hd), F32),
            jnp.zeros((bp, groups, state), F32), jnp.zeros((bp, groups, state), F32), lw, wb, l, s5_ops, alpha)
        xs, sh_s, rw_s, hr_s, hi_s = _layer(
            xs, cache_mem_k[l], cache_mem_v[l], cache_shift[l], state_rwkv[l],
            state_s5_re[l], state_s5_im[l], lw, wb, l, s5_ops, alpha)
        for acc, val in zip(outs, (mk_p, mv_p, rw_p, sh_p, hr_p, hi_p, rw_s, sh_s, hr_s, hi_s)):
            acc.append(val)
    return (xp, xs) + tuple(jnp.stack(o) for o in outs)
```

```python
import functools
import math

import jax
import jax.numpy as jnp
from jax import lax
from jax.experimental import pallas as pl
from jax.experimental.pallas import tpu as pltpu

F32 = jnp.float32
BF16 = jnp.bfloat16

LN_EPS = 1e-5
GN_EPS = 64e-5
RWKV_HEAD = 64
RANK_W = 64
RANK_A = 64
RANK_G = 128
S5_GROUP = 16
S5_CHUNK = 16
RWKV_CHUNK = 64
RWKV_SUBCHUNKS = 4
XATTN_HEADS = 4
LANES = 128
RWKV_GROUP_LANES = 128
S5_SUB = LANES // S5_GROUP
S5_TILE_CHUNKS = 256
VMEM_LIMIT = 56 * 1024 * 1024
ROW_BLOCK = 256
CAST_BLOCK_BYTES = 6 * 1024 * 1024

NT = (((1,), (1,)), ((), ()))
TN = (((0,), (0,)), ((), ()))


def _mm(a, b, dims=None):
    a = a.astype(BF16)
    b = b.astype(BF16)
    if dims is None:
        return jnp.dot(a, b, preferred_element_type=F32)
    return lax.dot_general(a, b, dims, preferred_element_type=F32)


def _mm_f32(a, b, dims=None):
    if dims is None:
        return jnp.dot(a, b, preferred_element_type=F32, precision=lax.Precision.HIGHEST)
    return lax.dot_general(a, b, dims, preferred_element_type=F32, precision=lax.Precision.HIGHEST)


def _split_bf16(x, parts):
    pieces = []
    for _ in range(parts):
        piece = x.astype(BF16)
        pieces.append(piece)
        x = x - piece.astype(F32)
    return pieces


def _ln(z, g, b):
    mu = jnp.mean(z, axis=-1, keepdims=True)
    d = z - mu
    var = jnp.mean(d * d, axis=-1, keepdims=True)
    return d * lax.rsqrt(var + LN_EPS) * g + b


def _sigmoid(x):
    return 1.0 / (1.0 + jnp.exp(-x))


def _params(*sem):
    return pltpu.CompilerParams(dimension_semantics=sem, vmem_limit_bytes=VMEM_LIMIT)


def _token_tile(n):
    for t in (512, 256, 128, 64, 32, 16, 8):
        if n % t == 0:
            return t
    raise ValueError(f"token count {n} is not a multiple of 8")


def _cast_kernel(x_ref, o_ref):
    o_ref[...] = x_ref[...].astype(BF16)


def _to_bf16(w):
    depth, a, b = w.shape
    rows = depth * a
    tr = 8
    while rows % (2 * tr) == 0 and 2 * tr * b * 4 <= CAST_BLOCK_BYTES:
        tr *= 2
    out = pl.pallas_call(
        _cast_kernel,
        grid=(rows // tr,),
        in_specs=[pl.BlockSpec((tr, b), lambda i: (i, 0))],
        out_specs=pl.BlockSpec((tr, b), lambda i: (i, 0)),
        out_shape=jax.ShapeDtypeStruct((rows, b), BF16),
        compiler_params=_params("parallel"),
        name="to_bf16",
    )(w.reshape(rows, b))
    return out.reshape(depth, a, b)


def _ffn_ln_kernel(x_ref, wg_ref, wu_ref, wd_ref, g_ref, b_ref, o_ref, xb_ref, acc_ref, *,
                   alpha, n_tiles, n_blocks):
    i = pl.program_id(0)
    j = pl.program_id(1)
    slot = i % 2
    rb = o_ref.shape[0] // n_blocks

    def norm_previous_rows():
        r0 = pl.multiple_of((j % n_blocks) * rb, rb)
        z = 0.5 * acc_ref[1 - slot, pl.ds(r0, rb), :]
        o_ref[pl.ds(r0, rb), :] = _ln(z, g_ref[...], b_ref[...])

    @pl.when((i == 0) & (j == 0))
    def _():
        acc_ref[1] = jnp.zeros(acc_ref.shape[1:], F32)

    @pl.when((i < n_tiles) & (j == 0))
    def _():
        x = x_ref[...]
        xb_ref[...] = x.astype(BF16)
        acc_ref[slot] = (2.0 * alpha) * x

    @pl.when(i < n_tiles)
    def _():
        norm_previous_rows()
        xb = xb_ref[...]
        gate = jnp.dot(xb, wg_ref[...], preferred_element_type=F32)
        up = jnp.dot(xb, wu_ref[...], preferred_element_type=F32)
        h = gate * _sigmoid(gate) * up
        acc_ref[slot] += jnp.dot(h.astype(BF16), wd_ref[...], preferred_element_type=F32)

    @pl.when(i == n_tiles)
    def _():
        norm_previous_rows()


def _ffn_ln(x, wg, wu, wd, layer, g, b, alpha):
    n, d = x.shape
    dff = wg.shape[-1]
    tm = _token_tile(n)
    tf = 512 if dff % 512 == 0 else dff
    n_tiles = n // tm
    steps = dff // tf
    n_blocks = 1
    while 2 * n_blocks <= min(steps, 8) and tm % (16 * n_blocks) == 0:
        n_blocks *= 2
    tile_in = lambda i, j: (jnp.minimum(i, n_tiles - 1), 0)
    chunk = lambda i, j: jnp.where(i < n_tiles, j, steps - 1)
    return pl.pallas_call(
        functools.partial(_ffn_ln_kernel, alpha=alpha, n_tiles=n_tiles, n_blocks=n_blocks),
        grid=(n_tiles + 1, steps),
        in_specs=[
            pl.BlockSpec((tm, d), tile_in),
            pl.BlockSpec((None, d, tf), lambda i, j: (layer, 0, chunk(i, j))),
            pl.BlockSpec((None, d, tf), lambda i, j: (layer, 0, chunk(i, j))),
            pl.BlockSpec((None, tf, d), lambda i, j: (layer, chunk(i, j), 0)),
            pl.BlockSpec((1, d), lambda i, j: (0, 0)),
            pl.BlockSpec((1, d), lambda i, j: (0, 0)),
        ],
        out_specs=pl.BlockSpec((tm, d), lambda i, j: (jnp.maximum(i - 1, 0), 0)),
        out_shape=jax.ShapeDtypeStruct((n, d), F32),
        scratch_shapes=[pltpu.VMEM((tm, d), BF16), pltpu.VMEM((2, tm, d), F32)],
        compiler_params=_params("arbitrary", "arbitrary"),
        name="ffn_ln",
    )(x, wg, wu, wd, g, b)


def _proj_kernel(x_ref, w_ref, o_ref):
    o_ref[...] = _mm(x_ref[...], w_ref[...])


def _column_tile(nout, cap=2304):
    best = None
    for t in range(LANES, min(nout, cap) + 1, LANES):
        if nout % t == 0:
            best = t
    return best if best is not None else nout


def _proj(x, w, layer):
    n, d = x.shape
    nout = w.shape[-1]
    tm = _token_tile(n)
    tn = _column_tile(nout)
    return pl.pallas_call(
        _proj_kernel,
        grid=(nout // tn, n // tm),
        in_specs=[
            pl.BlockSpec((tm, d), lambda j, i: (i, 0)),
            pl.BlockSpec((None, d, tn), lambda j, i: (layer, 0, j)),
        ],
        out_specs=pl.BlockSpec((tm, tn), lambda j, i: (i, j)),
        out_shape=jax.ShapeDtypeStruct((n, nout), F32),
        compiler_params=_params("parallel", "parallel"),
        name="proj",
    )(x, w)


def _rwkv_kernel(p_ref, shift_ref, s0_ref, mu_ref, w0_ref, wup_ref, a0_ref, aup_ref, gup_ref,
                 kk_ref, ka_ref, rk_ref, gnw_ref, gnb_ref,
                 o_ref, sout_ref, prev_ref, state_ref, *, t, n_sub, heads):
    c = pl.program_id(1)
    r_w = heads * RWKV_HEAD
    gw = min(RWKV_GROUP_LANES, r_w)
    per = gw // RWKV_HEAD
    mw = per * t
    head_shift = RWKV_HEAD.bit_length() - 1
    bf = lambda x: x.astype(BF16)

    @pl.when(c == 0)
    def _():
        prev_ref[...] = shift_ref[0]
        state_ref[...] = s0_ref[0]

    same_head = ((lax.broadcasted_iota(jnp.int32, (gw, gw), 0) >> head_shift)
                 == (lax.broadcasted_iota(jnp.int32, (gw, gw), 1) >> head_shift))
    ones_bd = jnp.where(same_head, 1.0, 0.0).astype(BF16)

    quads = range(0, r_w, gw)

    def head_sums(xs):
        parts = [piece[:, q:q + gw] for x in xs for piece in _split_bf16(x, 2) for q in quads]
        sums = jnp.dot(jnp.concatenate(parts, axis=0), ones_bd, preferred_element_type=F32)
        outs, r0 = [], 0
        for x in xs:
            rows = x.shape[0]
            hi = [sums[r0 + i * rows:r0 + (i + 1) * rows] for i in range(len(quads))]
            r0 += len(quads) * rows
            lo = [sums[r0 + i * rows:r0 + (i + 1) * rows] for i in range(len(quads))]
            r0 += len(quads) * rows
            outs.append(jnp.concatenate([h + l for h, l in zip(hi, lo)], axis=1))
        return outs

    tri = jnp.where(lax.broadcasted_iota(jnp.int32, (t, t), 0) >= lax.broadcasted_iota(jnp.int32, (t, t), 1),
                    1.0, 0.0).astype(BF16)

    def mix_in(p, prev_row):
        row = lax.broadcasted_iota(jnp.int32, p.shape, 0)
        prev = jnp.where(row == 0, prev_row, pltpu.roll(p, 1, axis=0))
        ps = p + mu_ref[...] * (prev - p)
        r = ps[:, 0:r_w]
        k = ps[:, r_w:2 * r_w]
        v = ps[:, 2 * r_w:3 * r_w]
        wa_in = ps[:, 3 * r_w:3 * r_w + RANK_W + RANK_A]
        lg = ps[:, 3 * r_w + RANK_W + RANK_A:]
        log_w = -math.exp(-0.5) * _sigmoid(w0_ref[...] + _mm(jnp.tanh(wa_in), wup_ref[...]))
        a = _sigmoid(a0_ref[...] + _mm(wa_in, aup_ref[...]))
        g = _mm(_sigmoid(lg), gup_ref[...])
        return dict(r=r, k=k, v=v, log_w=log_w, a=a, g=g, kk=k * kk_ref[...], k2=k * (1.0 + (a - 1.0) * ka_ref[...]))

    def decays(m, kk_sq, rk_sum):
        kk = m["kk"] * lax.rsqrt(jnp.maximum(kk_sq, 1e-24))
        b = kk * m["a"]
        log_w, r, k2 = m["log_w"], m["r"], m["k2"]
        cum = sum(jnp.dot(tri, piece, preferred_element_type=F32) for piece in _split_bf16(log_w, 3))
        e_cum = jnp.exp(cum)
        e_neg = jnp.exp(-cum)
        e_tail = jnp.exp(cum[t - 1:t, :] - cum)
        return dict(kq=kk * jnp.exp(cum - log_w), rq=r * e_cum, bd=b * e_neg, kd=k2 * e_neg, bt=b * e_tail,
                    kt=k2 * e_tail, v=m["v"], p_last=e_cum[t - 1:t, :], bonus=rk_sum * m["v"], g=m["g"])

    mixed = []
    prev_row = prev_ref[...]
    for j in range(n_sub):
        p = p_ref[0, j * t:(j + 1) * t, :]
        mixed.append(mix_in(p, prev_row))
        prev_row = p[t - 1:t, :]
    prev_ref[...] = prev_row
    sums = head_sums([m["kk"] * m["kk"] for m in mixed] + [m["r"] * m["k2"] * rk_ref[...] for m in mixed])
    pro = [decays(mixed[j], sums[j], sums[n_sub + j]) for j in range(n_sub)]

    def block_rows(x, bw):
        xb = bf(x)
        blk = lax.broadcasted_iota(jnp.int32, xb.shape, 1) >> (bw.bit_length() - 1)
        return jnp.concatenate([jnp.where(blk == h, xb, jnp.zeros_like(xb)) for h in range(per)], axis=0)

    lane_blk = lax.broadcasted_iota(jnp.int32, (RWKV_HEAD, gw), 1) >> head_shift

    def diag_blocks(z):
        return sum(jnp.where(lane_blk == h, z[h * RWKV_HEAD:(h + 1) * RWKV_HEAD, :], 0.0) for h in range(per))

    probs = [(j, q) for j in range(n_sub) for q in quads]
    ps_ = range(len(probs))
    part = lambda name: [pro[j][name][:, q:q + gw] for j, q in probs]
    kq, rq, bd, kd, bt, kt, v = map(part, ("kq", "rq", "bd", "kd", "bt", "kt", "v"))
    row_m = lax.broadcasted_iota(jnp.int32, (t, mw), 0)
    col_m = lax.broadcasted_iota(jnp.int32, (t, mw), 1) & (t - 1)
    strict = row_m > col_m
    incl = row_m >= col_m

    vbd = [block_rows(v[i], RWKV_HEAD) for i in ps_]
    ms = [_mm(jnp.concatenate([kq[i], rq[i]], axis=0),
              jnp.concatenate([block_rows(bd[i], RWKV_HEAD), block_rows(kd[i], RWKV_HEAD)], axis=0), NT) for i in ps_]
    m_b = [jnp.where(strict, ms[i][:t, :mw], 0.0) for i in ps_]
    mkv = [_mm(jnp.where(strict, ms[i][:t, mw:], 0.0), vbd[i]) for i in ps_]
    l_b = [bf(jnp.where(incl, ms[i][t:, :mw], 0.0)) for i in ps_]
    l_k = [bf(jnp.where(incl, ms[i][t:, mw:], 0.0)) for i in ps_]
    invs = None
    k = 1
    while k < t:
        shift = (2 * k).bit_length() - 1
        join = ((row_m >> shift) == (col_m >> shift)) & ((row_m & k) != 0) & ((col_m & k) == 0)
        cs = [jnp.where(join, m, 0.0) for m in m_b]
        if k == 1:
            invs = [jnp.where(row_m == col_m, 1.0, 0.0) - c for c in cs]
        else:
            xs = [_mm(cs[i], block_rows(invs[i], t)) for i in ps_]
            invs = [invs[i] - _mm(invs[i], block_rows(xs[i], t)) for i in ps_]
        k *= 2
    invs = [bf(x) for x in invs]
    gk = [_mm(invs[i], block_rows(kq[i], RWKV_HEAD)) for i in ps_]
    u0 = [-_mm(invs[i], block_rows(mkv[i], RWKV_HEAD)) for i in ps_]
    rp = [bf(rq[i] - _mm(l_b[i], block_rows(gk[i], RWKV_HEAD))) for i in ps_]
    y0 = [_mm(jnp.concatenate([l_b[i], l_k[i]], axis=1),
              jnp.concatenate([block_rows(u0[i], RWKV_HEAD), vbd[i]], axis=0)) for i in ps_]
    phi = [block_rows(diag_blocks(_mm(gk[i], bt[i], TN)), RWKV_HEAD) for i in ps_]
    psi = [diag_blocks(_mm(jnp.concatenate([u0[i], v[i]], axis=0), jnp.concatenate([bt[i], kt[i]], axis=0), TN))
           for i in ps_]

    states = [state_ref[:, q:q + gw] for q in quads]
    n_q = len(states)
    inv_hd = 1.0 / RWKV_HEAD
    for j in range(n_sub):
        ys = []
        for qi in range(n_q):
            i = j * n_q + qi
            s = states[qi]
            ys.append(y0[i] + _mm(rp[i], block_rows(s, RWKV_HEAD), NT))
            states[qi] = s * pro[j]["p_last"][:, quads[qi]:quads[qi] + gw] - _mm(s, phi[i]) + psi[i]
        y = jnp.concatenate(ys, axis=1)
        dy = y - head_sums([y])[0] * inv_hd
        var = head_sums([dy * dy])[0] * inv_hd
        yn = dy * lax.rsqrt(var + GN_EPS) * gnw_ref[...] + gnb_ref[...]
        o_ref[0, j * t:(j + 1) * t, :] = (yn + pro[j]["bonus"]) * pro[j]["g"]
    for qi, q in enumerate(quads):
        state_ref[:, q:q + gw] = states[qi]

    @pl.when(c == pl.num_programs(1) - 1)
    def _():
        sout_ref[0] = state_ref[...]


def _rwkv(p3, shift_prev, s0, lw):
    bsz, seq, _ = p3.shape
    heads = s0.shape[1]
    r_w = heads * RWKV_HEAD
    cols = 3 * r_w + RANK_W + RANK_A + RANK_G
    t = min(RWKV_CHUNK, seq)
    assert seq % t == 0 and t & (t - 1) == 0
    n_sub = RWKV_SUBCHUNKS if seq % (RWKV_SUBCHUNKS * t) == 0 else 1
    tt = n_sub * t
    row = lambda x: x.reshape(1, -1).astype(F32)
    zeros = jnp.zeros((RANK_W, r_w), F32)
    wup = jnp.concatenate([lw['w_up'], zeros], axis=0).astype(BF16)
    aup = jnp.concatenate([zeros, lw['a_up']], axis=0).astype(BF16)
    vec = lambda n: pl.BlockSpec((1, n), lambda b, c: (0, 0))
    full = lambda a, bb: pl.BlockSpec((a, bb), lambda b, c: (0, 0))
    lanes_hk = lambda s: jnp.swapaxes(s.astype(F32), 1, 2).reshape(bsz, RWKV_HEAD, r_w)
    out, s_new = pl.pallas_call(
        functools.partial(_rwkv_kernel, t=t, n_sub=n_sub, heads=heads),
        grid=(bsz, seq // tt),
        in_specs=[
            pl.BlockSpec((1, tt, cols), lambda b, c: (b, c, 0)),
            pl.BlockSpec((1, 1, cols), lambda b, c: (b, 0, 0)),
            pl.BlockSpec((1, RWKV_HEAD, r_w), lambda b, c: (b, 0, 0)),
            vec(cols), vec(r_w), full(RANK_W + RANK_A, r_w), vec(r_w), full(RANK_W + RANK_A, r_w),
            full(RANK_G, r_w), vec(r_w), vec(r_w), vec(r_w), vec(r_w), vec(r_w),
        ],
        out_specs=[
            pl.BlockSpec((1, tt, r_w), lambda b, c: (b, c, 0)),
            pl.BlockSpec((1, RWKV_HEAD, r_w), lambda b, c: (b, 0, 0)),
        ],
        out_shape=[
            jax.ShapeDtypeStruct((bsz, seq, r_w), F32),
            jax.ShapeDtypeStruct((bsz, RWKV_HEAD, r_w), F32),
        ],
        scratch_shapes=[pltpu.VMEM((1, cols), F32), pltpu.VMEM((RWKV_HEAD, r_w), F32)],
        compiler_params=_params("parallel", "arbitrary"),
        name="rwkv",
    )(p3, shift_prev.astype(F32), lanes_hk(s0), row(lw['shift_mu']), row(lw['w0']), wup, row(lw['a0']), aup,
      lw['g_up'].astype(BF16), row(lw['k_k']), row(lw['k_a']), row(lw['r_k']), row(lw['gn_w']),
      row(lw['gn_b']))
    return out, jnp.swapaxes(s_new.reshape(bsz, RWKV_HEAD, heads, RWKV_HEAD), 1, 2)


def _s5_prep_kernel(are_ref, aim_ref, ldt_ref, btre_ref, btim_ref, cre_ref, cim_ref, d_ref,
                    kst_ref, wre_ref, wim_ref, vre_ref, vimn_ref, lre_ref, lim_ref, *, tc):
    for g in range(are_ref.shape[0]):
        a_re = are_ref[g]
        a_im = aim_ref[g]
        dt = jnp.exp(ldt_ref[g])
        mag = jnp.exp(a_re * dt)
        l_re = mag * jnp.cos(a_im * dt)
        l_im = mag * jnp.sin(a_im * dt)
        den = a_re * a_re + a_im * a_im
        x_re = l_re - 1.0
        co_re = (x_re * a_re + l_im * a_im) / den
        co_im = (l_im * a_re - x_re * a_im) / den
        pw = [(jnp.ones_like(l_re), jnp.zeros_like(l_re))]
        for _ in range(tc):
            q_re, q_im = pw[-1]
            pw.append((q_re * l_re - q_im * l_im, q_re * l_im + q_im * l_re))

        c_re = cre_ref[g]
        c_im = cim_ref[g]
        bt_re = btre_ref[g]
        bt_im = btim_ref[g]
        cc_re = c_re * co_re - c_im * co_im
        cc_im = c_re * co_im + c_im * co_re
        cl_re = jnp.concatenate([cc_re * q[0] - cc_im * q[1] for q in pw[:tc]], axis=0)
        cl_im = jnp.concatenate([cc_re * q[1] + cc_im * q[0] for q in pw[:tc]], axis=0)
        kst = _mm_f32(cl_re, bt_re, NT) - _mm_f32(cl_im, bt_im, NT)
        rr = lax.broadcasted_iota(jnp.int32, kst.shape, 0)
        cc = lax.broadcasted_iota(jnp.int32, kst.shape, 1)
        kst_ref[g] = kst + jnp.where(rr == cc, d_ref[g], 0.0)

        w_re, w_im = [], []
        for s in range(tc):
            q_re, q_im = pw[tc - 1 - s]
            f_re = q_re * co_re - q_im * co_im
            f_im = q_re * co_im + q_im * co_re
            w_re.append(bt_re * f_re - bt_im * f_im)
            w_im.append(bt_re * f_im + bt_im * f_re)
        wre_ref[g] = jnp.concatenate(w_re, axis=0)
        wim_ref[g] = jnp.concatenate(w_im, axis=0)
        vre_ref[g] = jnp.concatenate([c_re * q[0] - c_im * q[1] for q in pw[1:]], axis=0)
        vimn_ref[g] = jnp.concatenate([-(c_re * q[1] + c_im * q[0]) for q in pw[1:]], axis=0)
        lre_ref[g] = pw[tc][0]
        lim_ref[g] = pw[tc][1]


def _s5_prep(lw):
    groups, state = lw['a_re'].shape
    ch = S5_GROUP
    tc = S5_CHUNK
    g3 = lambda x, a, b: x.astype(F32).reshape(groups, a, b)
    gpb = S5_SUB if groups % S5_SUB == 0 else 1
    spec = lambda a, b: pl.BlockSpec((gpb, a, b), lambda g: (g, 0, 0))
    outs = pl.pallas_call(
        functools.partial(_s5_prep_kernel, tc=tc),
        grid=(groups // gpb,),
        in_specs=[spec(1, state), spec(1, state), spec(1, 1), spec(ch, state), spec(ch, state),
                  spec(ch, state), spec(ch, state), spec(1, ch)],
        out_specs=[spec(tc * ch, ch), spec(tc * ch, state), spec(tc * ch, state), spec(tc * ch, state),
                   spec(tc * ch, state), spec(1, state), spec(1, state)],
        out_shape=[jax.ShapeDtypeStruct((groups, tc * ch, ch), F32)]
        + [jax.ShapeDtypeStruct((groups, tc * ch, state), F32)] * 4
        + [jax.ShapeDtypeStruct((groups, 1, state), F32)] * 2,
        compiler_params=_params("parallel"),
        name="s5_prep",
    )(g3(lw['a_re'], 1, state), g3(lw['a_im'], 1, state), g3(lw['log_dt'], 1, 1),
      jnp.swapaxes(lw['b_re'], 1, 2).astype(F32), jnp.swapaxes(lw['b_im'], 1, 2).astype(F32),
      lw['c_re'].astype(F32), lw['c_im'].astype(F32), g3(lw['d'], 1, ch))
    kst, w_re, w_im, v_re, v_imn, l_re, l_im = outs
    k4 = kst.reshape(groups, tc, ch, ch)
    tt = jnp.arange(tc)
    tau = tt[:, None] - tt[None, :]
    blocks = jnp.where((tau >= 0)[None, :, :, None, None], k4[:, jnp.clip(tau, 0, tc - 1)], 0.0)
    toep = jnp.transpose(blocks, (0, 1, 3, 2, 4)).reshape(groups, tc * ch, tc * ch)
    return dict(
        toep=toep.astype(BF16),
        w=jnp.concatenate([w_re, w_im, w_im, w_re], axis=-1).astype(BF16),
        vt=jnp.concatenate([v_re, v_imn], axis=-1).astype(BF16),
        a16=jnp.concatenate([l_re, l_re], axis=-1),
        b16=jnp.concatenate([-l_im, l_im], axis=-1),
    )


def _lane_block_transpose(arrs):
    n = len(arrs)
    assert n * S5_GROUP == LANES and n & (n - 1) == 0
    blk = lax.broadcasted_iota(jnp.int32, arrs[0].shape, 1) >> (S5_GROUP.bit_length() - 1)
    cur = list(arrs)
    b = n // 2
    while b:
        upper = (blk & b) != 0
        nxt = list(cur)
        for i in range(n):
            if i & b:
                continue
            lo, hi = cur[i], cur[i + b]
            nxt[i] = jnp.where(upper, pltpu.roll(hi, S5_GROUP * b, axis=1), lo)
            nxt[i + b] = jnp.where(upper, hi, pltpu.roll(lo, LANES - S5_GROUP * b, axis=1))
        cur = nxt
        b //= 2
    return cur


def _s5_kernel(x_ref, toep_ref, w_ref, vt_ref, a_ref, b_ref, h0_ref, h0s_ref,
               y_ref, hout_ref, x_st, s_st, ex_ref, es_ref, hs_ref, *, nct):
    ci = pl.program_id(2)
    sub = S5_SUB
    halves = S5_CHUNK // sub
    width = a_ref.shape[-1]

    @pl.when(ci == 0)
    def _():
        x_st[...] = h0_ref[0, 0]
        s_st[...] = h0s_ref[0, 0]

    rows_of = lambda t: pl.ds(t, nct, stride=S5_CHUNK)
    assert halves == 2
    hi_bits = lambda x: lax.bitcast_convert_type(x.astype(BF16).astype(F32), jnp.int32)
    words = _lane_block_transpose(
        [lax.shift_right_logical(hi_bits(x_ref[0, rows_of(tt), :]), 16) | hi_bits(x_ref[0, rows_of(sub + tt), :])
         for tt in range(sub)])
    halves_of = lambda w: (lax.shift_left(w, 16), w & jnp.int32(-65536))
    us = [jnp.concatenate([lax.bitcast_convert_type(h, F32) for h in halves_of(words[g])], axis=1).astype(BF16)
          for g in range(sub)]

    for g in range(sub):
        e = _mm(us[g], w_ref[g])
        ex_ref[pl.ds(g, nct, stride=sub), :] = e[:, :width]
        es_ref[pl.ds(g, nct, stride=sub), :] = e[:, width:]

    a = a_ref[0]
    b = b_ref[0]
    steps = min(8, nct)

    def body(i, carry):
        x, s = carry
        for j in range(steps):
            rows = pl.ds(pl.multiple_of((i * steps + j) * sub, sub), sub)
            hs_ref[rows, :] = x
            x, s = a * x + b * s + ex_ref[rows, :], a * s - b * x + es_ref[rows, :]
        return x, s

    x, s = lax.fori_loop(0, nct // steps, body, (x_st[...], s_st[...]))
    x_st[...] = x
    s_st[...] = s
    hout_ref[0, 0] = x

    ys = [_mm(us[g], toep_ref[g], NT) + _mm(hs_ref[pl.ds(g, nct, stride=sub), :], vt_ref[g], NT)
          for g in range(sub)]
    for hf in range(halves):
        outs = _lane_block_transpose([ys[g][:, hf * LANES:(hf + 1) * LANES] for g in range(sub)])
        for tt in range(sub):
            y_ref[0, rows_of(hf * sub + tt), :] = outs[tt]


def _s5(p3, col0, h_re, h_im, ops):
    bsz, seq, n_in = p3.shape
    groups, state = h_re.shape[1:]
    ch, tc, sub = S5_GROUP, S5_CHUNK, S5_SUB
    assert seq % tc == 0 and groups % sub == 0 and col0 % LANES == 0 and tc % sub == 0
    n_chunks = seq // tc
    nct = min(S5_TILE_CHUNKS, n_chunks)
    assert n_chunks % nct == 0 and nct % min(8, nct) == 0
    gbs = groups // sub
    lane0 = col0 // LANES
    pack = lambda x, y: jnp.concatenate([x, y], axis=-1).astype(F32).reshape(bsz, gbs, sub, 2 * state)
    per_g = lambda last: pl.BlockSpec((sub, tc * ch, last), lambda gb, b, ci: (gb, 0, 0))
    vec = pl.BlockSpec((1, sub, 2 * state), lambda gb, b, ci: (gb, 0, 0))
    st = pl.BlockSpec((1, 1, sub, 2 * state), lambda gb, b, ci: (b, gb, 0, 0))
    y, h_out = pl.pallas_call(
        functools.partial(_s5_kernel, nct=nct),
        grid=(gbs, bsz, n_chunks // nct),
        in_specs=[pl.BlockSpec((1, nct * tc, LANES), lambda gb, b, ci: (b, ci, lane0 + gb)),
                  per_g(tc * ch), per_g(4 * state), per_g(2 * state), vec, vec, st, st],
        out_specs=[pl.BlockSpec((1, nct * tc, LANES), lambda gb, b, ci: (b, ci, gb)), st],
        out_shape=[jax.ShapeDtypeStruct((bsz, seq, groups * ch), F32),
                   jax.ShapeDtypeStruct((bsz, gbs, sub, 2 * state), F32)],
        scratch_shapes=[pltpu.VMEM((sub, 2 * state), F32), pltpu.VMEM((sub, 2 * state), F32),
                        pltpu.VMEM((nct * sub, 2 * state), F32), pltpu.VMEM((nct * sub, 2 * state), F32),
                        pltpu.VMEM((nct * sub, 2 * state), F32)],
        compiler_params=_params("parallel", "parallel", "arbitrary"),
        name="s5",
    )(p3, ops['toep'], ops['w'], ops['vt'],
      ops['a16'].reshape(gbs, sub, 2 * state), ops['b16'].reshape(gbs, sub, 2 * state),
      pack(h_re, h_im), pack(h_im, h_re))
    h_out = h_out.reshape(bsz, groups, 2 * state)
    return y, h_out[..., :state], h_out[..., state:]


def _gelu_tanh(x):
    return 0.5 * x * (1.0 + jnp.tanh(math.sqrt(2.0 / math.pi) * (x + 0.044715 * (x * x * x))))


def _mix_kernel(x_ref, rw_ref, y_ref, gw_ref, gb_ref, wo1_ref, wo2_ref, g_ref, b_ref, o_ref, *, alpha, rb):
    for r in range(0, x_ref.shape[0], rb):
        rows = slice(r, r + rb)
        y = _gelu_tanh(y_ref[rows, :])
        s5o = y * _sigmoid(_mm(y, gw_ref[...]) + gb_ref[...])
        mix = _mm(rw_ref[rows, :], wo1_ref[...]) + _mm(s5o, wo2_ref[...])
        o_ref[rows, :] = _ln(alpha * x_ref[rows, :] + mix, g_ref[...], b_ref[...])


def _mix(x, rw, y, glu_w, glu_b, w_out, layer, g, b, alpha):
    n, d = x.shape
    r_w = rw.shape[1]
    s_w = y.shape[1]
    tm = _token_tile(n)
    tile = lambda w: pl.BlockSpec((tm, w), lambda i: (i, 0))
    full = lambda a, bb: pl.BlockSpec((a, bb), lambda i: (0, 0), pipeline_mode=pl.Buffered(1))
    slab = lambda a, bb, k: pl.BlockSpec((None, a, bb), lambda i: (layer, k, 0), pipeline_mode=pl.Buffered(1))
    assert r_w == s_w
    return pl.pallas_call(
        functools.partial(_mix_kernel, alpha=alpha, rb=min(tm, ROW_BLOCK)),
        grid=(n // tm,),
        in_specs=[tile(d), tile(r_w), tile(s_w), slab(s_w, s_w, 0), full(1, s_w), slab(r_w, d, 0), slab(s_w, d, 1),
                  full(1, d), full(1, d)],
        out_specs=tile(d),
        out_shape=jax.ShapeDtypeStruct((n, d), F32),
        compiler_params=_params("parallel"),
        name="mix",
    )(x, rw, y, glu_w, glu_b.reshape(1, -1).astype(F32), w_out, w_out, g, b)


def _attn_kernel(x_ref, mk_ref, mv_ref, wq_ref, wo_ref, g_ref, b_ref, o_ref, *, alpha, heads, rb):
    d = x_ref.shape[-1]
    hd = d // heads
    mk = mk_ref[0]
    mv = mv_ref[0]
    for r in range(0, x_ref.shape[0], rb):
        x = x_ref[r:r + rb, :]
        q = _mm(x, wq_ref[...]) * (hd ** -0.5)
        outs = []
        for h in range(heads):
            sl = slice(h * hd, (h + 1) * hd)
            s = _mm(q[:, sl], mk[:, sl], NT)
            e = jnp.exp(s - jnp.max(s, axis=-1, keepdims=True))
            pr = e / jnp.sum(e, axis=-1, keepdims=True)
            outs.append(_mm(pr, mv[:, sl]))
        o = jnp.concatenate(outs, axis=-1)
        o_ref[r:r + rb, :] = _ln(alpha * x + _mm(o, wo_ref[...]), g_ref[...], b_ref[...])


def _attn(x, mk, mv, wq, wo, layer, g, b, alpha, seq):
    n, d = x.shape
    n_mem = mk.shape[1]
    tm = min(_token_tile(n), _token_tile(seq))
    per_batch = seq // tm
    tile = pl.BlockSpec((tm, d), lambda i: (i, 0))
    mem = pl.BlockSpec((1, n_mem, d), lambda i: (i // per_batch, 0, 0))
    full = lambda a, bb: pl.BlockSpec((a, bb), lambda i: (0, 0), pipeline_mode=pl.Buffered(1))
    slab = pl.BlockSpec((None, d, d), lambda i: (layer, 0, 0), pipeline_mode=pl.Buffered(1))
    return pl.pallas_call(
        functools.partial(_attn_kernel, alpha=alpha, heads=XATTN_HEADS, rb=min(tm, ROW_BLOCK)),
        grid=(n // tm,),
        in_specs=[tile, mem, mem, slab, slab, full(1, d), full(1, d)],
        out_specs=tile,
        out_shape=jax.ShapeDtypeStruct((n, d), F32),
        compiler_params=_params("parallel"),
        name="attn",
    )(x, mk.astype(BF16), mv.astype(BF16), wq, wo, g, b)


def _layer(x, mk, mv, shift_prev, s_rwkv, h_re, h_im, lw, wb, layer, s5_ops, alpha):
    bsz, seq, d = x.shape
    n = bsz * seq
    ln = lambda i: (lw['ln_g'][i].reshape(1, d).astype(F32), lw['ln_b'][i].reshape(1, d).astype(F32))
    cols = shift_prev.shape[-1]
    x1 = _ffn_ln(x.reshape(n, d), wb['f1g'], wb['f1u'], wb['f1d'], layer, *ln(0), alpha)
    p = _proj(x1, wb['w_in'], layer).reshape(bsz, seq, -1)
    rw, s_new = _rwkv(p, shift_prev, s_rwkv, lw)
    y, hr, hi = _s5(p, cols, h_re, h_im, s5_ops)
    x2 = _mix(x1, rw.reshape(n, -1), y.reshape(n, -1), wb['glu_w'], lw['glu_b'], wb['w_out'], layer, *ln(1), alpha)
    x3 = _attn(x2, mk.reshape(bsz, -1, d), mv.reshape(bsz, -1, d), wb['xq'], wb['xo'], layer, *ln(2), alpha, seq)
    x4 = _ffn_ln(x3, wb['f2g'], wb['f2u'], wb['f2d'], layer, *ln(3), alpha)
    return x4.reshape(bsz, seq, d), p[:, -1:, :cols], s_new, hr, hi


def kernel(x_prompt, x_sample, mem_prompt, cache_mem_k, cache_mem_v, state_rwkv, cache_shift, state_s5_re, state_s5_im, ln_g, ln_b, ffn1_gate, ffn1_up, ffn1_down, w_in, shift_mu, rwkv_w0, rwkv_w_up, rwkv_a0, rwkv_a_up, rwkv_g_up, rwkv_k_k, rwkv_k_a, rwkv_r_k, rwkv_gn_w, rwkv_gn_b, s5_a_re, s5_a_im, s5_log_dt, s5_b_re, s5_b_im, s5_c_re, s5_c_im, s5_d, s5_glu_w, s5_glu_b, w_mix_out, xattn_q, xattn_k, xattn_v, xattn_o, ffn2_gate, ffn2_up, ffn2_down):
    depth = ln_g.shape[0]
    bp, _, d = x_prompt.shape
    n_mem = mem_prompt.shape[1]
    heads, hd = state_rwkv.shape[2], state_rwkv.shape[3]
    cols = cache_shift.shape[-1]
    groups, state = state_s5_re.shape[2:]
    alpha = (2.0 * depth) ** 0.25
    bf = _to_bf16
    wb = dict(f1g=bf(ffn1_gate), f1u=bf(ffn1_up), f1d=bf(ffn1_down), w_in=bf(w_in), glu_w=bf(s5_glu_w),
              w_out=bf(w_mix_out), xq=bf(xattn_q), xk=bf(xattn_k), xv=bf(xattn_v), xo=bf(xattn_o),
              f2g=bf(ffn2_gate), f2u=bf(ffn2_up), f2d=bf(ffn2_down))
    xp, xs = x_prompt, x_sample
    outs = [[] for _ in range(10)]
    for l in range(depth):
        lw = dict(ln_g=ln_g[l], ln_b=ln_b[l], shift_mu=shift_mu[l], w0=rwkv_w0[l], w_up=rwkv_w_up[l], a0=rwkv_a0[l],
                  a_up=rwkv_a_up[l], g_up=rwkv_g_up[l], k_k=rwkv_k_k[l], k_a=rwkv_k_a[l], r_k=rwkv_r_k[l],
                  gn_w=rwkv_gn_w[l], gn_b=rwkv_gn_b[l], a_re=s5_a_re[l], a_im=s5_a_im[l], log_dt=s5_log_dt[l],
                  b_re=s5_b_re[l], b_im=s5_b_im[l], c_re=s5_c_re[l], c_im=s5_c_im[l], d=s5_d[l],
                  glu_b=s5_glu_b[l])
        s5_ops = _s5_prep(lw)
        mem2 = mem_prompt.reshape(bp * n_mem, d)
        mk_p = _proj(mem2, wb['xk'], l).reshape(bp, n_mem, XATTN_HEADS, d // XATTN_HEADS)
        mv_p = _proj(mem2, wb['xv'], l).reshape(bp, n_mem, XATTN_HEADS, d // XATTN_HEADS)
        xp, sh_p, rw_p, hr_p, hi_p = _layer(
            xp, mk_p, mv_p, jnp.zeros((bp, 1, cols), F32), jnp.zeros((bp, heads, hd, hd), F32),
            jnp.zeros((bp, groups, state), F32), jnp.zeros((bp, groups, state), F32), lw, wb, l, s5_ops, alpha)
        xs, sh_s, rw_s, hr_s, hi_s = _layer(
            xs, cache_mem_k[l], cache_mem_v[l], cache_shift[l], state_rwkv[l],
            state_s5_re[l], state_s5_im[l], lw, wb, l, s5_ops, alpha)
        for acc, val in zip(outs, (mk_p, mv_p, rw_p, sh_p, hr_p, hi_p, rw_s, sh_s, hr_s, hi_s)):
            acc.append(val)
    return (xp, xs) + tuple(jnp.stack(o) for o in outs)
```

```python
import functools
import math

import jax
import jax.numpy as jnp
from jax import lax
from jax.experimental import pallas as pl
from jax.experimental.pallas import tpu as pltpu

F32 = jnp.float32
BF16 = jnp.bfloat16

LN_EPS = 1e-5
GN_EPS = 64e-5
RWKV_HEAD = 64
RANK_W = 64
RANK_A = 64
RANK_G = 128
S5_GROUP = 16
S5_CHUNK = 16
RWKV_CHUNK = 64
RWKV_SUBCHUNKS = 4
XATTN_HEADS = 4
LANES = 128
RWKV_GROUP_LANES = 128
S5_SUB = LANES // S5_GROUP
S5_TILE_CHUNKS = 256
S5_SMALL_CHUNKS = 64
VMEM_LIMIT = 56 * 1024 * 1024
ROW_BLOCK = 256
CAST_BLOCK_BYTES = 6 * 1024 * 1024

NT = (((1,), (1,)), ((), ()))
TN = (((0,), (0,)), ((), ()))


def _mm(a, b, dims=None):
    a = a.astype(BF16)
    b = b.astype(BF16)
    if dims is None:
        return jnp.dot(a, b, preferred_element_type=F32)
    return lax.dot_general(a, b, dims, preferred_element_type=F32)


def _mm_f32(a, b, dims=None):
    if dims is None:
        return jnp.dot(a, b, preferred_element_type=F32, precision=lax.Precision.HIGHEST)
    return lax.dot_general(a, b, dims, preferred_element_type=F32, precision=lax.Precision.HIGHEST)


def _split_bf16(x, parts):
    pieces = []
    for _ in range(parts):
        piece = x.astype(BF16)
        pieces.append(piece)
        x = x - piece.astype(F32)
    return pieces


def _ln(z, g, b):
    mu = jnp.mean(z, axis=-1, keepdims=True)
    d = z - mu
    var = jnp.mean(d * d, axis=-1, keepdims=True)
    return d * lax.rsqrt(var + LN_EPS) * g + b


def _sigmoid(x):
    return 1.0 / (1.0 + jnp.exp(-x))


def _params(*sem):
    return pltpu.CompilerParams(dimension_semantics=sem, vmem_limit_bytes=VMEM_LIMIT)


def _token_tile(n):
    for t in (512, 256, 128, 64, 32, 16, 8):
        if n % t == 0:
            return t
    raise ValueError(f"token count {n} is not a multiple of 8")


def _cast_kernel(x_ref, o_ref):
    o_ref[...] = x_ref[...].astype(BF16)


def _to_bf16(w):
    depth, a, b = w.shape
    rows = depth * a
    tr = 8
    while rows % (2 * tr) == 0 and 2 * tr * b * 4 <= CAST_BLOCK_BYTES:
        tr *= 2
    out = pl.pallas_call(
        _cast_kernel,
        grid=(rows // tr,),
        in_specs=[pl.BlockSpec((tr, b), lambda i: (i, 0))],
        out_specs=pl.BlockSpec((tr, b), lambda i: (i, 0)),
        out_shape=jax.ShapeDtypeStruct((rows, b), BF16),
        compiler_params=_params("parallel"),
        name="to_bf16",
    )(w.reshape(rows, b))
    return out.reshape(depth, a, b)


def _ffn_ln_kernel(x_ref, wg_ref, wu_ref, wd_ref, g_ref, b_ref, o_ref, xb_ref, acc_ref, *,
                   alpha, n_tiles, n_blocks):
    i = pl.program_id(0)
    j = pl.program_id(1)
    slot = i % 2
    rb = o_ref.shape[0] // n_blocks

    def norm_previous_rows():
        r0 = pl.multiple_of((j % n_blocks) * rb, rb)
        z = 0.5 * acc_ref[1 - slot, pl.ds(r0, rb), :]
        o_ref[pl.ds(r0, rb), :] = _ln(z, g_ref[...], b_ref[...])

    @pl.when((i == 0) & (j == 0))
    def _():
        acc_ref[1] = jnp.zeros(acc_ref.shape[1:], F32)

    @pl.when((i < n_tiles) & (j == 0))
    def _():
        x = x_ref[...]
        xb_ref[...] = x.astype(BF16)
        acc_ref[slot] = (2.0 * alpha) * x

    @pl.when(i < n_tiles)
    def _():
        norm_previous_rows()
        xb = xb_ref[...]
        gate = jnp.dot(xb, wg_ref[...], preferred_element_type=F32)
        up = jnp.dot(xb, wu_ref[...], preferred_element_type=F32)
        h = gate * _sigmoid(gate) * up
        acc_ref[slot] += jnp.dot(h.astype(BF16), wd_ref[...], preferred_element_type=F32)

    @pl.when(i == n_tiles)
    def _():
        norm_previous_rows()


def _ffn_ln(x, wg, wu, wd, layer, g, b, alpha):
    n, d = x.shape
    dff = wg.shape[-1]
    tm = _token_tile(n)
    tf = 512 if dff % 512 == 0 else dff
    n_tiles = n // tm
    steps = dff // tf
    n_blocks = 1
    while 2 * n_blocks <= min(steps, 8) and tm % (16 * n_blocks) == 0:
        n_blocks *= 2
    tile_in = lambda i, j: (jnp.minimum(i, n_tiles - 1), 0)
    chunk = lambda i, j: jnp.where(i < n_tiles, j, steps - 1)
    return pl.pallas_call(
        functools.partial(_ffn_ln_kernel, alpha=alpha, n_tiles=n_tiles, n_blocks=n_blocks),
        grid=(n_tiles + 1, steps),
        in_specs=[
            pl.BlockSpec((tm, d), tile_in),
            pl.BlockSpec((None, d, tf), lambda i, j: (layer, 0, chunk(i, j))),
            pl.BlockSpec((None, d, tf), lambda i, j: (layer, 0, chunk(i, j))),
            pl.BlockSpec((None, tf, d), lambda i, j: (layer, chunk(i, j), 0)),
            pl.BlockSpec((1, d), lambda i, j: (0, 0)),
            pl.BlockSpec((1, d), lambda i, j: (0, 0)),
        ],
        out_specs=pl.BlockSpec((tm, d), lambda i, j: (jnp.maximum(i - 1, 0), 0)),
        out_shape=jax.ShapeDtypeStruct((n, d), F32),
        scratch_shapes=[pltpu.VMEM((tm, d), BF16), pltpu.VMEM((2, tm, d), F32)],
        compiler_params=_params("arbitrary", "arbitrary"),
        name="ffn_ln",
    )(x, wg, wu, wd, g, b)


def _proj_kernel(x_ref, w_ref, o_ref):
    o_ref[...] = _mm(x_ref[...], w_ref[...])


def _column_tile(nout, cap=2304):
    best = None
    for t in range(LANES, min(nout, cap) + 1, LANES):
        if nout % t == 0:
            best = t
    return best if best is not None else nout


def _proj(x, w, layer):
    n, d = x.shape
    nout = w.shape[-1]
    tm = _token_tile(n)
    tn = _column_tile(nout)
    return pl.pallas_call(
        _proj_kernel,
        grid=(nout // tn, n // tm),
        in_specs=[
            pl.BlockSpec((tm, d), lambda j, i: (i, 0)),
            pl.BlockSpec((None, d, tn), lambda j, i: (layer, 0, j)),
        ],
        out_specs=pl.BlockSpec((tm, tn), lambda j, i: (i, j)),
        out_shape=jax.ShapeDtypeStruct((n, nout), F32),
        compiler_params=_params("parallel", "parallel"),
        name="proj",
    )(x, w)


def _rwkv_kernel(p_ref, shift_ref, s0_ref, mu_ref, w0_ref, wup_ref, a0_ref, aup_ref, gup_ref,
                 kk_ref, ka_ref, rk_ref, gnw_ref, gnb_ref,
                 o_ref, sout_ref, prev_ref, state_ref, *, t, n_sub, heads):
    c = pl.program_id(1)
    r_w = heads * RWKV_HEAD
    gw = min(RWKV_GROUP_LANES, r_w)
    per = gw // RWKV_HEAD
    mw = per * t
    head_shift = RWKV_HEAD.bit_length() - 1
    bf = lambda x: x.astype(BF16)

    @pl.when(c == 0)
    def _():
        prev_ref[...] = shift_ref[0]
        state_ref[...] = s0_ref[0]

    same_head = ((lax.broadcasted_iota(jnp.int32, (gw, gw), 0) >> head_shift)
                 == (lax.broadcasted_iota(jnp.int32, (gw, gw), 1) >> head_shift))
    ones_bd = jnp.where(same_head, 1.0, 0.0).astype(BF16)

    quads = range(0, r_w, gw)

    def head_sums(xs):
        parts = [piece[:, q:q + gw] for x in xs for piece in _split_bf16(x, 2) for q in quads]
        sums = jnp.dot(jnp.concatenate(parts, axis=0), ones_bd, preferred_element_type=F32)
        outs, r0 = [], 0
        for x in xs:
            rows = x.shape[0]
            hi = [sums[r0 + i * rows:r0 + (i + 1) * rows] for i in range(len(quads))]
            r0 += len(quads) * rows
            lo = [sums[r0 + i * rows:r0 + (i + 1) * rows] for i in range(len(quads))]
            r0 += len(quads) * rows
            outs.append(jnp.concatenate([h + l for h, l in zip(hi, lo)], axis=1))
        return outs

    tri = jnp.where(lax.broadcasted_iota(jnp.int32, (t, t), 0) >= lax.broadcasted_iota(jnp.int32, (t, t), 1),
                    1.0, 0.0).astype(BF16)

    p = p_ref[0]
    row = lax.broadcasted_iota(jnp.int32, p.shape, 0)
    prev = jnp.where(row == 0, prev_ref[...], pltpu.roll(p, 1, axis=0))
    prev_ref[...] = p[n_sub * t - 1:n_sub * t, :]
    ps = p + mu_ref[...] * (prev - p)
    r_all = ps[:, 0:r_w]
    k_all = ps[:, r_w:2 * r_w]
    v_all = ps[:, 2 * r_w:3 * r_w]
    wa_in = ps[:, 3 * r_w:3 * r_w + RANK_W + RANK_A]
    lg = ps[:, 3 * r_w + RANK_W + RANK_A:]
    log_w_all = -math.exp(-0.5) * _sigmoid(w0_ref[...] + _mm(jnp.tanh(wa_in), wup_ref[...]))
    a_all = _sigmoid(a0_ref[...] + _mm(wa_in, aup_ref[...]))
    g_all = _mm(_sigmoid(lg), gup_ref[...])
    kk_raw = k_all * kk_ref[...]
    k2_all = k_all * (1.0 + (a_all - 1.0) * ka_ref[...])
    kk_sq, rk_sum = head_sums([kk_raw * kk_raw, r_all * k2_all * rk_ref[...]])
    kk_all = kk_raw * lax.rsqrt(jnp.maximum(kk_sq, 1e-24))
    b_all = kk_all * a_all
    bonus_all = rk_sum * v_all

    def decays(rows):
        kk, b, log_w, r, k2 = kk_all[rows], b_all[rows], log_w_all[rows], r_all[rows], k2_all[rows]
        cum = sum(jnp.dot(tri, piece, preferred_element_type=F32) for piece in _split_bf16(log_w, 3))
        e_cum = jnp.exp(cum)
        e_neg = jnp.exp(-cum)
        e_tail = jnp.exp(cum[t - 1:t, :] - cum)
        return dict(kq=kk * jnp.exp(cum - log_w), rq=r * e_cum, bd=b * e_neg, kd=k2 * e_neg, bt=b * e_tail,
                    kt=k2 * e_tail, v=v_all[rows], p_last=e_cum[t - 1:t, :], bonus=bonus_all[rows], g=g_all[rows])

    pro = [decays(slice(j * t, (j + 1) * t)) for j in range(n_sub)]

    def block_rows(x, bw):
        xb = bf(x)
        blk = lax.broadcasted_iota(jnp.int32, xb.shape, 1) >> (bw.bit_length() - 1)
        return jnp.concatenate([jnp.where(blk == h, xb, jnp.zeros_like(xb)) for h in range(per)], axis=0)

    lane_blk = lax.broadcasted_iota(jnp.int32, (RWKV_HEAD, gw), 1) >> head_shift

    def diag_blocks(z):
        return sum(jnp.where(lane_blk == h, z[h * RWKV_HEAD:(h + 1) * RWKV_HEAD, :], 0.0) for h in range(per))

    probs = [(j, q) for j in range(n_sub) for q in quads]
    ps_ = range(len(probs))
    part = lambda name: [pro[j][name][:, q:q + gw] for j, q in probs]
    kq, rq, bd, kd, bt, kt, v = map(part, ("kq", "rq", "bd", "kd", "bt", "kt", "v"))
    row_m = lax.broadcasted_iota(jnp.int32, (t, mw), 0)
    col_m = lax.broadcasted_iota(jnp.int32, (t, mw), 1) & (t - 1)
    strict = row_m > col_m
    incl = row_m >= col_m

    vbd = [block_rows(v[i], RWKV_HEAD) for i in ps_]
    ms = [_mm(jnp.concatenate([kq[i], rq[i]], axis=0),
              jnp.concatenate([block_rows(bd[i], RWKV_HEAD), block_rows(kd[i], RWKV_HEAD)], axis=0), NT) for i in ps_]
    m_b = [jnp.where(strict, ms[i][:t, :mw], 0.0) for i in ps_]
    mkv = [_mm(jnp.where(strict, ms[i][:t, mw:], 0.0), vbd[i]) for i in ps_]
    l_b = [bf(jnp.where(incl, ms[i][t:, :mw], 0.0)) for i in ps_]
    l_k = [bf(jnp.where(incl, ms[i][t:, mw:], 0.0)) for i in ps_]
    invs = None
    k = 1
    while k < t:
        shift = (2 * k).bit_length() - 1
        join = ((row_m >> shift) == (col_m >> shift)) & ((row_m & k) != 0) & ((col_m & k) == 0)
        cs = [jnp.where(join, m, 0.0) for m in m_b]
        if k == 1:
            invs = [jnp.where(row_m == col_m, 1.0, 0.0) - c for c in cs]
        else:
            xs = [_mm(cs[i], block_rows(invs[i], t)) for i in ps_]
            invs = [invs[i] - _mm(invs[i], block_rows(xs[i], t)) for i in ps_]
        k *= 2
    invs = [bf(x) for x in invs]
    gk = [_mm(invs[i], block_rows(kq[i], RWKV_HEAD)) for i in ps_]
    u0 = [-_mm(invs[i], block_rows(mkv[i], RWKV_HEAD)) for i in ps_]
    rp = [bf(rq[i] - _mm(l_b[i], block_rows(gk[i], RWKV_HEAD))) for i in ps_]
    y0 = [_mm(jnp.concatenate([l_b[i], l_k[i]], axis=1),
              jnp.concatenate([block_rows(u0[i], RWKV_HEAD), vbd[i]], axis=0)) for i in ps_]
    phi = [block_rows(diag_blocks(_mm(gk[i], bt[i], TN)), RWKV_HEAD) for i in ps_]
    psi = [diag_blocks(_mm(jnp.concatenate([u0[i], v[i]], axis=0), jnp.concatenate([bt[i], kt[i]], axis=0), TN))
           for i in ps_]

    states = [state_ref[:, q:q + gw] for q in quads]
    n_q = len(states)
    inv_hd = 1.0 / RWKV_HEAD
    for j in range(n_sub):
        ys = []
        for qi in range(n_q):
            i = j * n_q + qi
            s = states[qi]
            ys.append(y0[i] + _mm(rp[i], block_rows(s, RWKV_HEAD), NT))
            states[qi] = s * pro[j]["p_last"][:, quads[qi]:quads[qi] + gw] - _mm(s, phi[i]) + psi[i]
        y = jnp.concatenate(ys, axis=1)
        dy = y - head_sums([y])[0] * inv_hd
        var = head_sums([dy * dy])[0] * inv_hd
        yn = dy * lax.rsqrt(var + GN_EPS) * gnw_ref[...] + gnb_ref[...]
        o_ref[0, j * t:(j + 1) * t, :] = (yn + pro[j]["bonus"]) * pro[j]["g"]
    for qi, q in enumerate(quads):
        state_ref[:, q:q + gw] = states[qi]

    @pl.when(c == pl.num_programs(1) - 1)
    def _():
        sout_ref[0] = state_ref[...]


def _rwkv(p3, shift_prev, s0, lw):
    bsz, seq, _ = p3.shape
    heads = s0.shape[1]
    r_w = heads * RWKV_HEAD
    cols = 3 * r_w + RANK_W + RANK_A + RANK_G
    t = min(RWKV_CHUNK, seq)
    assert seq % t == 0 and t & (t - 1) == 0
    n_sub = RWKV_SUBCHUNKS if seq % (RWKV_SUBCHUNKS * t) == 0 else 1
    tt = n_sub * t
    row = lambda x: x.reshape(1, -1).astype(F32)
    zeros = jnp.zeros((RANK_W, r_w), F32)
    wup = jnp.concatenate([lw['w_up'], zeros], axis=0).astype(BF16)
    aup = jnp.concatenate([zeros, lw['a_up']], axis=0).astype(BF16)
    vec = lambda n: pl.BlockSpec((1, n), lambda b, c: (0, 0))
    full = lambda a, bb: pl.BlockSpec((a, bb), lambda b, c: (0, 0))
    lanes_hk = lambda s: jnp.swapaxes(s.astype(F32), 1, 2).reshape(bsz, RWKV_HEAD, r_w)
    out, s_new = pl.pallas_call(
        functools.partial(_rwkv_kernel, t=t, n_sub=n_sub, heads=heads),
        grid=(bsz, seq // tt),
        in_specs=[
            pl.BlockSpec((1, tt, cols), lambda b, c: (b, c, 0)),
            pl.BlockSpec((1, 1, cols), lambda b, c: (b, 0, 0)),
            pl.BlockSpec((1, RWKV_HEAD, r_w), lambda b, c: (b, 0, 0)),
            vec(cols), vec(r_w), full(RANK_W + RANK_A, r_w), vec(r_w), full(RANK_W + RANK_A, r_w),
            full(RANK_G, r_w), vec(r_w), vec(r_w), vec(r_w), vec(r_w), vec(r_w),
        ],
        out_specs=[
            pl.BlockSpec((1, tt, r_w), lambda b, c: (b, c, 0)),
            pl.BlockSpec((1, RWKV_HEAD, r_w), lambda b, c: (b, 0, 0)),
        ],
        out_shape=[
            jax.ShapeDtypeStruct((bsz, seq, r_w), F32),
            jax.ShapeDtypeStruct((bsz, RWKV_HEAD, r_w), F32),
        ],
        scratch_shapes=[pltpu.VMEM((1, cols), F32), pltpu.VMEM((RWKV_HEAD, r_w), F32)],
        compiler_params=_params("parallel", "arbitrary"),
        name="rwkv",
    )(p3, shift_prev.astype(F32), lanes_hk(s0), row(lw['shift_mu']), row(lw['w0']), wup, row(lw['a0']), aup,
      lw['g_up'].astype(BF16), row(lw['k_k']), row(lw['k_a']), row(lw['r_k']), row(lw['gn_w']),
      row(lw['gn_b']))
    return out, jnp.swapaxes(s_new.reshape(bsz, RWKV_HEAD, heads, RWKV_HEAD), 1, 2)


def _s5_prep_kernel(are_ref, aim_ref, ldt_ref, btre_ref, btim_ref, cre_ref, cim_ref, d_ref,
                    kst_ref, wre_ref, wim_ref, vre_ref, vimn_ref, lre_ref, lim_ref, *, tc):
    for g in range(are_ref.shape[0]):
        a_re = are_ref[g]
        a_im = aim_ref[g]
        dt = jnp.exp(ldt_ref[g])
        mag = jnp.exp(a_re * dt)
        l_re = mag * jnp.cos(a_im * dt)
        l_im = mag * jnp.sin(a_im * dt)
        den = a_re * a_re + a_im * a_im
        x_re = l_re - 1.0
        co_re = (x_re * a_re + l_im * a_im) / den
        co_im = (l_im * a_re - x_re * a_im) / den
        pw = [(jnp.ones_like(l_re), jnp.zeros_like(l_re))]
        for _ in range(tc):
            q_re, q_im = pw[-1]
            pw.append((q_re * l_re - q_im * l_im, q_re * l_im + q_im * l_re))

        c_re = cre_ref[g]
        c_im = cim_ref[g]
        bt_re = btre_ref[g]
        bt_im = btim_ref[g]
        cc_re = c_re * co_re - c_im * co_im
        cc_im = c_re * co_im + c_im * co_re
        cl_re = jnp.concatenate([cc_re * q[0] - cc_im * q[1] for q in pw[:tc]], axis=0)
        cl_im = jnp.concatenate([cc_re * q[1] + cc_im * q[0] for q in pw[:tc]], axis=0)
        kst = _mm_f32(cl_re, bt_re, NT) - _mm_f32(cl_im, bt_im, NT)
        rr = lax.broadcasted_iota(jnp.int32, kst.shape, 0)
        cc = lax.broadcasted_iota(jnp.int32, kst.shape, 1)
        kst_ref[g] = kst + jnp.where(rr == cc, d_ref[g], 0.0)

        w_re, w_im = [], []
        for s in range(tc):
            q_re, q_im = pw[tc - 1 - s]
            f_re = q_re * co_re - q_im * co_im
            f_im = q_re * co_im + q_im * co_re
            w_re.append(bt_re * f_re - bt_im * f_im)
            w_im.append(bt_re * f_im + bt_im * f_re)
        wre_ref[g] = jnp.concatenate(w_re, axis=0)
        wim_ref[g] = jnp.concatenate(w_im, axis=0)
        vre_ref[g] = jnp.concatenate([c_re * q[0] - c_im * q[1] for q in pw[1:]], axis=0)
        vimn_ref[g] = jnp.concatenate([-(c_re * q[1] + c_im * q[0]) for q in pw[1:]], axis=0)
        lre_ref[g] = pw[tc][0]
        lim_ref[g] = pw[tc][1]


def _s5_prep(lw):
    groups, state = lw['a_re'].shape
    ch = S5_GROUP
    tc = S5_CHUNK
    g3 = lambda x, a, b: x.astype(F32).reshape(groups, a, b)
    gpb = S5_SUB if groups % S5_SUB == 0 else 1
    spec = lambda a, b: pl.BlockSpec((gpb, a, b), lambda g: (g, 0, 0))
    outs = pl.pallas_call(
        functools.partial(_s5_prep_kernel, tc=tc),
        grid=(groups // gpb,),
        in_specs=[spec(1, state), spec(1, state), spec(1, 1), spec(ch, state), spec(ch, state),
                  spec(ch, state), spec(ch, state), spec(1, ch)],
        out_specs=[spec(tc * ch, ch), spec(tc * ch, state), spec(tc * ch, state), spec(tc * ch, state),
                   spec(tc * ch, state), spec(1, state), spec(1, state)],
        out_shape=[jax.ShapeDtypeStruct((groups, tc * ch, ch), F32)]
        + [jax.ShapeDtypeStruct((groups, tc * ch, state), F32)] * 4
        + [jax.ShapeDtypeStruct((groups, 1, state), F32)] * 2,
        compiler_params=_params("parallel"),
        name="s5_prep",
    )(g3(lw['a_re'], 1, state), g3(lw['a_im'], 1, state), g3(lw['log_dt'], 1, 1),
      jnp.swapaxes(lw['b_re'], 1, 2).astype(F32), jnp.swapaxes(lw['b_im'], 1, 2).astype(F32),
      lw['c_re'].astype(F32), lw['c_im'].astype(F32), g3(lw['d'], 1, ch))
    kst, w_re, w_im, v_re, v_imn, l_re, l_im = outs
    k4 = kst.reshape(groups, tc, ch, ch)
    tt = jnp.arange(tc)
    tau = tt[:, None] - tt[None, :]
    blocks = jnp.where((tau >= 0)[None, :, :, None, None], k4[:, jnp.clip(tau, 0, tc - 1)], 0.0)
    toep = jnp.transpose(blocks, (0, 1, 3, 2, 4)).reshape(groups, tc * ch, tc * ch)
    return dict(
        toep=toep.astype(BF16),
        w=jnp.concatenate([w_re, w_im, w_im, w_re], axis=-1).astype(BF16),
        vt=jnp.concatenate([v_re, v_imn], axis=-1).astype(BF16),
        a16=jnp.concatenate([l_re, l_re], axis=-1),
        b16=jnp.concatenate([-l_im, l_im], axis=-1),
    )


def _lane_block_transpose(arrs):
    n = len(arrs)
    assert n * S5_GROUP == LANES and n & (n - 1) == 0
    blk = lax.broadcasted_iota(jnp.int32, arrs[0].shape, 1) >> (S5_GROUP.bit_length() - 1)
    cur = list(arrs)
    b = n // 2
    while b:
        upper = (blk & b) != 0
        nxt = list(cur)
        for i in range(n):
            if i & b:
                continue
            lo, hi = cur[i], cur[i + b]
            nxt[i] = jnp.where(upper, pltpu.roll(hi, S5_GROUP * b, axis=1), lo)
            nxt[i + b] = jnp.where(upper, hi, pltpu.roll(lo, LANES - S5_GROUP * b, axis=1))
        cur = nxt
        b //= 2
    return cur


def _s5_kernel(x_ref, toep_ref, w_ref, vt_ref, a_ref, b_ref, h0_ref, h0s_ref,
               y_ref, hout_ref, x_st, s_st, ex_ref, es_ref, hs_ref, *, nct, bpb):
    ci = pl.program_id(2)
    sub = S5_SUB
    halves = S5_CHUNK // sub
    width = a_ref.shape[-1]
    assert halves == 2
    a = a_ref[0]
    b = b_ref[0]
    steps = min(8, nct)
    rows_of = lambda t: pl.ds(t, nct, stride=S5_CHUNK)
    hi_bits = lambda x: lax.bitcast_convert_type(x.astype(BF16).astype(F32), jnp.int32)
    halves_of = lambda w: (lax.shift_left(w, 16), w & jnp.int32(-65536))

    def scan_step(i, carry):
        x, s = carry
        for j in range(steps):
            rows = pl.ds(pl.multiple_of((i * steps + j) * sub, sub), sub)
            hs_ref[rows, :] = x
            x, s = a * x + b * s + ex_ref[rows, :], a * s - b * x + es_ref[rows, :]
        return x, s

    for bi in range(bpb):
        @pl.when(ci == 0)
        def _():
            x_st[bi] = h0_ref[bi, 0]
            s_st[bi] = h0s_ref[bi, 0]

        words = _lane_block_transpose(
            [lax.shift_right_logical(hi_bits(x_ref[bi, rows_of(tt), :]), 16) | hi_bits(x_ref[bi, rows_of(sub + tt), :])
             for tt in range(sub)])
        us = [jnp.concatenate([lax.bitcast_convert_type(h, F32) for h in halves_of(words[g])], axis=1).astype(BF16)
              for g in range(sub)]

        for g in range(sub):
            e = _mm(us[g], w_ref[g])
            ex_ref[pl.ds(g, nct, stride=sub), :] = e[:, :width]
            es_ref[pl.ds(g, nct, stride=sub), :] = e[:, width:]

        x, s = lax.fori_loop(0, nct // steps, scan_step, (x_st[bi], s_st[bi]))
        x_st[bi] = x
        s_st[bi] = s
        hout_ref[bi, 0] = x

        ys = [_mm(us[g], toep_ref[g], NT) + _mm(hs_ref[pl.ds(g, nct, stride=sub), :], vt_ref[g], NT)
              for g in range(sub)]
        for hf in range(halves):
            outs = _lane_block_transpose([ys[g][:, hf * LANES:(hf + 1) * LANES] for g in range(sub)])
            for tt in range(sub):
                y_ref[bi, rows_of(hf * sub + tt), :] = outs[tt]


def _s5(p3, col0, h_re, h_im, ops):
    bsz, seq, n_in = p3.shape
    groups, state = h_re.shape[1:]
    ch, tc, sub = S5_GROUP, S5_CHUNK, S5_SUB
    assert seq % tc == 0 and groups % sub == 0 and col0 % LANES == 0 and tc % sub == 0
    n_chunks = seq // tc
    nct = min(S5_TILE_CHUNKS, n_chunks)
    assert n_chunks % nct == 0 and nct % min(8, nct) == 0
    gbs = groups // sub
    lane0 = col0 // LANES
    bpb = bsz if n_chunks == nct and bsz * n_chunks <= S5_SMALL_CHUNKS else 1
    pack = lambda x, y: jnp.concatenate([x, y], axis=-1).astype(F32).reshape(bsz, gbs, sub, 2 * state)
    per_g = lambda last: pl.BlockSpec((sub, tc * ch, last), lambda gb, b, ci: (gb, 0, 0))
    vec = pl.BlockSpec((1, sub, 2 * state), lambda gb, b, ci: (gb, 0, 0))
    st = pl.BlockSpec((bpb, 1, sub, 2 * state), lambda gb, b, ci: (b, gb, 0, 0))
    y, h_out = pl.pallas_call(
        functools.partial(_s5_kernel, nct=nct, bpb=bpb),
        grid=(gbs, bsz // bpb, n_chunks // nct),
        in_specs=[pl.BlockSpec((bpb, nct * tc, LANES), lambda gb, b, ci: (b, ci, lane0 + gb)),
                  per_g(tc * ch), per_g(4 * state), per_g(2 * state), vec, vec, st, st],
        out_specs=[pl.BlockSpec((bpb, nct * tc, LANES), lambda gb, b, ci: (b, ci, gb)), st],
        out_shape=[jax.ShapeDtypeStruct((bsz, seq, groups * ch), F32),
                   jax.ShapeDtypeStruct((bsz, gbs, sub, 2 * state), F32)],
        scratch_shapes=[pltpu.VMEM((bpb, sub, 2 * state), F32), pltpu.VMEM((bpb, sub, 2 * state), F32),
                        pltpu.VMEM((nct * sub, 2 * state), F32), pltpu.VMEM((nct * sub, 2 * state), F32),
                        pltpu.VMEM((nct * sub, 2 * state), F32)],
        compiler_params=_params("parallel", "parallel", "arbitrary"),
        name="s5",
    )(p3, ops['toep'], ops['w'], ops['vt'],
      ops['a16'].reshape(gbs, sub, 2 * state), ops['b16'].reshape(gbs, sub, 2 * state),
      pack(h_re, h_im), pack(h_im, h_re))
    h_out = h_out.reshape(bsz, groups, 2 * state)
    return y, h_out[..., :state], h_out[..., state:]


def _gelu_tanh(x):
    return 0.5 * x * (1.0 + jnp.tanh(math.sqrt(2.0 / math.pi) * (x + 0.044715 * (x * x * x))))


def _mix_kernel(x_ref, rw_ref, y_ref, gw_ref, gb_ref, wo1_ref, wo2_ref, g_ref, b_ref, o_ref, *, alpha, rb):
    for r in range(0, x_ref.shape[0], rb):
        rows = slice(r, r + rb)
        y = _gelu_tanh(y_ref[rows, :])
        s5o = y * _sigmoid(_mm(y, gw_ref[...]) + gb_ref[...])
        mix = _mm(rw_ref[rows, :], wo1_ref[...]) + _mm(s5o, wo2_ref[...])
        o_ref[rows, :] = _ln(alpha * x_ref[rows, :] + mix, g_ref[...], b_ref[...])


def _mix(x, rw, y, glu_w, glu_b, w_out, layer, g, b, alpha):
    n, d = x.shape
    r_w = rw.shape[1]
    s_w = y.shape[1]
    tm = _token_tile(n)
    tile = lambda w: pl.BlockSpec((tm, w), lambda i: (i, 0))
    full = lambda a, bb: pl.BlockSpec((a, bb), lambda i: (0, 0), pipeline_mode=pl.Buffered(1))
    slab = lambda a, bb, k: pl.BlockSpec((None, a, bb), lambda i: (layer, k, 0), pipeline_mode=pl.Buffered(1))
    assert r_w == s_w
    return pl.pallas_call(
        functools.partial(_mix_kernel, alpha=alpha, rb=min(tm, ROW_BLOCK)),
        grid=(n // tm,),
        in_specs=[tile(d), tile(r_w), tile(s_w), slab(s_w, s_w, 0), full(1, s_w), slab(r_w, d, 0), slab(s_w, d, 1),
                  full(1, d), full(1, d)],
        out_specs=tile(d),
        out_shape=jax.ShapeDtypeStruct((n, d), F32),
        compiler_params=_params("parallel"),
        name="mix",
    )(x, rw, y, glu_w, glu_b.reshape(1, -1).astype(F32), w_out, w_out, g, b)


def _attn_kernel(x_ref, mk_ref, mv_ref, wq_ref, wo_ref, g_ref, b_ref, o_ref, *, alpha, heads, rb):
    d = x_ref.shape[-1]
    hd = d // heads
    mk = mk_ref[0]
    mv = mv_ref[0]
    for r in range(0, x_ref.shape[0], rb):
        x = x_ref[r:r + rb, :]
        q = _mm(x, wq_ref[...]) * (hd ** -0.5)
        outs = []
        for h in range(heads):
            sl = slice(h * hd, (h + 1) * hd)
            s = _mm(q[:, sl], mk[:, sl], NT)
            e = jnp.exp(s - jnp.max(s, axis=-1, keepdims=True))
            pr = e / jnp.sum(e, axis=-1, keepdims=True)
            outs.append(_mm(pr, mv[:, sl]))
        o = jnp.concatenate(outs, axis=-1)
        o_ref[r:r + rb, :] = _ln(alpha * x + _mm(o, wo_ref[...]), g_ref[...], b_ref[...])


def _attn(x, mk, mv, wq, wo, layer, g, b, alpha, seq):
    n, d = x.shape
    n_mem = mk.shape[1]
    tm = min(_token_tile(n), _token_tile(seq))
    per_batch = seq // tm
    tile = pl.BlockSpec((tm, d), lambda i: (i, 0))
    mem = pl.BlockSpec((1, n_mem, d), lambda i: (i // per_batch, 0, 0))
    full = lambda a, bb: pl.BlockSpec((a, bb), lambda i: (0, 0), pipeline_mode=pl.Buffered(1))
    slab = pl.BlockSpec((None, d, d), lambda i: (layer, 0, 0), pipeline_mode=pl.Buffered(1))
    return pl.pallas_call(
        functools.partial(_attn_kernel, alpha=alpha, heads=XATTN_HEADS, rb=min(tm, ROW_BLOCK)),
        grid=(n // tm,),
        in_specs=[tile, mem, mem, slab, slab, full(1, d), full(1, d)],
        out_specs=tile,
        out_shape=jax.ShapeDtypeStruct((n, d), F32),
        compiler_params=_params("parallel"),
        name="attn",
    )(x, mk.astype(BF16), mv.astype(BF16), wq, wo, g, b)


def _layer(x, mk, mv, shift_prev, s_rwkv, h_re, h_im, lw, wb, layer, s5_ops, alpha):
    bsz, seq, d = x.shape
    n = bsz * seq
    ln = lambda i: (lw['ln_g'][i].reshape(1, d).astype(F32), lw['ln_b'][i].reshape(1, d).astype(F32))
    cols = shift_prev.shape[-1]
    x1 = _ffn_ln(x.reshape(n, d), wb['f1g'], wb['f1u'], wb['f1d'], layer, *ln(0), alpha)
    p = _proj(x1, wb['w_in'], layer).reshape(bsz, seq, -1)
    rw, s_new = _rwkv(p, shift_prev, s_rwkv, lw)
    y, hr, hi = _s5(p, cols, h_re, h_im, s5_ops)
    x2 = _mix(x1, rw.reshape(n, -1), y.reshape(n, -1), wb['glu_w'], lw['glu_b'], wb['w_out'], layer, *ln(1), alpha)
    x3 = _attn(x2, mk.reshape(bsz, -1, d), mv.reshape(bsz, -1, d), wb['xq'], wb['xo'], layer, *ln(2), alpha, seq)
    x4 = _ffn_ln(x3, wb['f2g'], wb['f2u'], wb['f2d'], layer, *ln(3), alpha)
    return x4.reshape(bsz, seq, d), p[:, -1:, :cols], s_new, hr, hi


def kernel(x_prompt, x_sample, mem_prompt, cache_mem_k, cache_mem_v, state_rwkv, cache_shift, state_s5_re, state_s5_im, ln_g, ln_b, ffn1_gate, ffn1_up, ffn1_down, w_in, shift_mu, rwkv_w0, rwkv_w_up, rwkv_a0, rwkv_a_up, rwkv_g_up, rwkv_k_k, rwkv_k_a, rwkv_r_k, rwkv_gn_w, rwkv_gn_b, s5_a_re, s5_a_im, s5_log_dt, s5_b_re, s5_b_im, s5_c_re, s5_c_im, s5_d, s5_glu_w, s5_glu_b, w_mix_out, xattn_q, xattn_k, xattn_v, xattn_o, ffn2_gate, ffn2_up, ffn2_down):
    depth = ln_g.shape[0]
    bp, _, d = x_prompt.shape
    n_mem = mem_prompt.shape[1]
    heads, hd = state_rwkv.shape[2], state_rwkv.shape[3]
    cols = cache_shift.shape[-1]
    groups, state = state_s5_re.shape[2:]
    alpha = (2.0 * depth) ** 0.25
    bf = _to_bf16
    wb = dict(f1g=bf(ffn1_gate), f1u=bf(ffn1_up), f1d=bf(ffn1_down), w_in=bf(w_in), glu_w=bf(s5_glu_w),
              w_out=bf(w_mix_out), xq=bf(xattn_q), xk=bf(xattn_k), xv=bf(xattn_v), xo=bf(xattn_o),
              f2g=bf(ffn2_gate), f2u=bf(ffn2_up), f2d=bf(ffn2_down))
    xp, xs = x_prompt, x_sample
    outs = [[] for _ in range(10)]
    for l in range(depth):
        lw = dict(ln_g=ln_g[l], ln_b=ln_b[l], shift_mu=shift_mu[l], w0=rwkv_w0[l], w_up=rwkv_w_up[l], a0=rwkv_a0[l],
                  a_up=rwkv_a_up[l], g_up=rwkv_g_up[l], k_k=rwkv_k_k[l], k_a=rwkv_k_a[l], r_k=rwkv_r_k[l],
                  gn_w=rwkv_gn_w[l], gn_b=rwkv_gn_b[l], a_re=s5_a_re[l], a_im=s5_a_im[l], log_dt=s5_log_dt[l],
                  b_re=s5_b_re[l], b_im=s5_b_im[l], c_re=s5_c_re[l], c_im=s5_c_im[l], d=s5_d[l],
                  glu_b=s5_glu_b[l])
        s5_ops = _s5_prep(lw)
        mem2 = mem_prompt.reshape(bp * n_mem, d)
        mk_p = _proj(mem2, wb['xk'], l).reshape(bp, n_mem, XATTN_HEADS, d // XATTN_HEADS)
        mv_p = _proj(mem2, wb['xv'], l).reshape(bp, n_mem, XATTN_HEADS, d // XATTN_HEADS)
        xp, sh_p, rw_p, hr_p, hi_p = _layer(
            xp, mk_p, mv_p, jnp.zeros((bp, 1, cols), F32), jnp.zeros((bp, heads, hd, hd), F32),
            jnp.zeros((bp, groups, state), F32), jnp.zeros((bp, groups, state), F32), lw, wb, l, s5_ops, alpha)
        xs, sh_s, rw_s, hr_s, hi_s = _layer(
            xs, cache_mem_k[l], cache_mem_v[l], cache_shift[l], state_rwkv[l],
            state_s5_re[l], state_s5_im[l], lw, wb, l, s5_ops, alpha)
        for acc, val in zip(outs, (mk_p, mv_p, rw_p, sh_p, hr_p, hi_p, rw_s, sh_s, hr_s, hi_s)):
            acc.append(val)
    return (xp, xs) + tuple(jnp.stack(o) for o in outs)
```

```python
import functools
import math

import jax
import jax.numpy as jnp
from jax import lax
from jax.experimental import pallas as pl
from jax.experimental.pallas import tpu as pltpu

F32 = jnp.float32
BF16 = jnp.bfloat16

LN_EPS = 1e-5
GN_EPS = 64e-5
RWKV_HEAD = 64
RANK_W = 64
RANK_A = 64
RANK_G = 128
S5_GROUP = 16
S5_CHUNK = 16
RWKV_CHUNK = 64
RWKV_SUBCHUNKS = 4
XATTN_HEADS = 4
LANES = 128
RWKV_GROUP_LANES = 128
S5_SUB = LANES // S5_GROUP
S5_TILE_CHUNKS = 256
S5_SMALL_CHUNKS = 64
VMEM_LIMIT = 56 * 1024 * 1024
ROW_BLOCK = 256
CAST_BLOCK_BYTES = 6 * 1024 * 1024

NT = (((1,), (1,)), ((), ()))
TN = (((0,), (0,)), ((), ()))


def _mm(a, b, dims=None):
    a = a.astype(BF16)
    b = b.astype(BF16)
    if dims is None:
        return jnp.dot(a, b, preferred_element_type=F32)
    return lax.dot_general(a, b, dims, preferred_element_type=F32)


def _mm_f32(a, b, dims=None):
    if dims is None:
        return jnp.dot(a, b, preferred_element_type=F32, precision=lax.Precision.HIGHEST)
    return lax.dot_general(a, b, dims, preferred_element_type=F32, precision=lax.Precision.HIGHEST)


def _split_bf16(x, parts):
    pieces = []
    for _ in range(parts):
        piece = x.astype(BF16)
        pieces.append(piece)
        x = x - piece.astype(F32)
    return pieces


def _ln(z, g, b):
    mu = jnp.mean(z, axis=-1, keepdims=True)
    d = z - mu
    var = jnp.mean(d * d, axis=-1, keepdims=True)
    return d * lax.rsqrt(var + LN_EPS) * g + b


def _sigmoid(x):
    return 1.0 / (1.0 + jnp.exp(-x))


def _params(*sem):
    return pltpu.CompilerParams(dimension_semantics=sem, vmem_limit_bytes=VMEM_LIMIT)


def _token_tile(n):
    for t in (512, 256, 128, 64, 32, 16, 8):
        if n % t == 0:
            return t
    raise ValueError(f"token count {n} is not a multiple of 8")


def _cast_kernel(x_ref, o_ref):
    o_ref[...] = x_ref[...].astype(BF16)


def _to_bf16(w):
    depth, a, b = w.shape
    rows = depth * a
    tr = 8
    while rows % (2 * tr) == 0 and 2 * tr * b * 4 <= CAST_BLOCK_BYTES:
        tr *= 2
    out = pl.pallas_call(
        _cast_kernel,
        grid=(rows // tr,),
        in_specs=[pl.BlockSpec((tr, b), lambda i: (i, 0))],
        out_specs=pl.BlockSpec((tr, b), lambda i: (i, 0)),
        out_shape=jax.ShapeDtypeStruct((rows, b), BF16),
        compiler_params=_params("parallel"),
        name="to_bf16",
    )(w.reshape(rows, b))
    return out.reshape(depth, a, b)


def _ffn_ln_kernel(x_ref, wg_ref, wu_ref, wd_ref, g_ref, b_ref, o_ref, xb_ref, acc_ref, *,
                   alpha, n_tiles, n_blocks):
    i = pl.program_id(0)
    j = pl.program_id(1)
    slot = i % 2
    rb = o_ref.shape[0] // n_blocks

    def norm_previous_rows():
        r0 = pl.multiple_of((j % n_blocks) * rb, rb)
        z = 0.5 * acc_ref[1 - slot, pl.ds(r0, rb), :]
        o_ref[pl.ds(r0, rb), :] = _ln(z, g_ref[...], b_ref[...])

    @pl.when((i == 0) & (j == 0))
    def _():
        acc_ref[1] = jnp.zeros(acc_ref.shape[1:], F32)

    @pl.when((i < n_tiles) & (j == 0))
    def _():
        x = x_ref[...]
        xb_ref[...] = x.astype(BF16)
        acc_ref[slot] = (2.0 * alpha) * x

    @pl.when(i < n_tiles)
    def _():
        norm_previous_rows()
        xb = xb_ref[...]
        gate = jnp.dot(xb, wg_ref[...], preferred_element_type=F32)
        up = jnp.dot(xb, wu_ref[...], preferred_element_type=F32)
        h = gate * _sigmoid(gate) * up
        acc_ref[slot] += jnp.dot(h.astype(BF16), wd_ref[...], preferred_element_type=F32)

    @pl.when(i == n_tiles)
    def _():
        norm_previous_rows()


def _ffn_ln(x, wg, wu, wd, layer, g, b, alpha):
    n, d = x.shape
    dff = wg.shape[-1]
    tm = _token_tile(n)
    tf = 512 if dff % 512 == 0 else dff
    n_tiles = n // tm
    steps = dff // tf
    n_blocks = 1
    while 2 * n_blocks <= min(steps, 8) and tm % (16 * n_blocks) == 0:
        n_blocks *= 2
    tile_in = lambda i, j: (jnp.minimum(i, n_tiles - 1), 0)
    chunk = lambda i, j: jnp.where(i < n_tiles, j, steps - 1)
    return pl.pallas_call(
        functools.partial(_ffn_ln_kernel, alpha=alpha, n_tiles=n_tiles, n_blocks=n_blocks),
        grid=(n_tiles + 1, steps),
        in_specs=[
            pl.BlockSpec((tm, d), tile_in),
            pl.BlockSpec((None, d, tf), lambda i, j: (layer, 0, chunk(i, j))),
            pl.BlockSpec((None, d, tf), lambda i, j: (layer, 0, chunk(i, j))),
            pl.BlockSpec((None, tf, d), lambda i, j: (layer, chunk(i, j), 0)),
            pl.BlockSpec((1, d), lambda i, j: (0, 0)),
            pl.BlockSpec((1, d), lambda i, j: (0, 0)),
        ],
        out_specs=pl.BlockSpec((tm, d), lambda i, j: (jnp.maximum(i - 1, 0), 0)),
        out_shape=jax.ShapeDtypeStruct((n, d), F32),
        scratch_shapes=[pltpu.VMEM((tm, d), BF16), pltpu.VMEM((2, tm, d), F32)],
        compiler_params=_params("arbitrary", "arbitrary"),
        name="ffn_ln",
    )(x, wg, wu, wd, g, b)


def _proj_kernel(x_ref, w_ref, o_ref):
    o_ref[...] = _mm(x_ref[...], w_ref[...])


def _column_tile(nout, cap=2304):
    best = None
    for t in range(LANES, min(nout, cap) + 1, LANES):
        if nout % t == 0:
            best = t
    return best if best is not None else nout


def _proj(x, w, layer):
    n, d = x.shape
    nout = w.shape[-1]
    tm = _token_tile(n)
    tn = _column_tile(nout)
    return pl.pallas_call(
        _proj_kernel,
        grid=(nout // tn, n // tm),
        in_specs=[
            pl.BlockSpec((tm, d), lambda j, i: (i, 0)),
            pl.BlockSpec((None, d, tn), lambda j, i: (layer, 0, j)),
        ],
        out_specs=pl.BlockSpec((tm, tn), lambda j, i: (i, j)),
        out_shape=jax.ShapeDtypeStruct((n, nout), F32),
        compiler_params=_params("parallel", "parallel"),
        name="proj",
    )(x, w)


def _rwkv_kernel(p_ref, shift_ref, s0_ref, mu_ref, w0_ref, wup_ref, a0_ref, aup_ref, gup_ref,
                 kk_ref, ka_ref, rk_ref, gnw_ref, gnb_ref,
                 o_ref, sout_ref, prev_ref, state_ref, *, t, n_sub, heads):
    c = pl.program_id(1)
    r_w = heads * RWKV_HEAD
    gw = min(RWKV_GROUP_LANES, r_w)
    per = gw // RWKV_HEAD
    mw = per * t
    head_shift = RWKV_HEAD.bit_length() - 1
    bf = lambda x: x.astype(BF16)

    @pl.when(c == 0)
    def _():
        prev_ref[...] = shift_ref[0]
        state_ref[...] = s0_ref[0]

    same_head = ((lax.broadcasted_iota(jnp.int32, (gw, gw), 0) >> head_shift)
                 == (lax.broadcasted_iota(jnp.int32, (gw, gw), 1) >> head_shift))
    ones_bd = jnp.where(same_head, 1.0, 0.0).astype(BF16)

    quads = range(0, r_w, gw)

    def head_sums(xs):
        parts = [piece[:, q:q + gw] for x in xs for piece in _split_bf16(x, 2) for q in quads]
        sums = jnp.dot(jnp.concatenate(parts, axis=0), ones_bd, preferred_element_type=F32)
        outs, r0 = [], 0
        for x in xs:
            rows = x.shape[0]
            hi = [sums[r0 + i * rows:r0 + (i + 1) * rows] for i in range(len(quads))]
            r0 += len(quads) * rows
            lo = [sums[r0 + i * rows:r0 + (i + 1) * rows] for i in range(len(quads))]
            r0 += len(quads) * rows
            outs.append(jnp.concatenate([h + l for h, l in zip(hi, lo)], axis=1))
        return outs

    p = p_ref[0]
    row = lax.broadcasted_iota(jnp.int32, p.shape, 0)
    prev = jnp.where(row == 0, prev_ref[...], pltpu.roll(p, 1, axis=0))
    prev_ref[...] = p[n_sub * t - 1:n_sub * t, :]
    ps = p + mu_ref[...] * (prev - p)
    r_all = ps[:, 0:r_w]
    k_all = ps[:, r_w:2 * r_w]
    v_all = ps[:, 2 * r_w:3 * r_w]
    wa_in = ps[:, 3 * r_w:3 * r_w + RANK_W + RANK_A]
    lg = ps[:, 3 * r_w + RANK_W + RANK_A:]
    log_w_all = -math.exp(-0.5) * _sigmoid(w0_ref[...] + _mm(jnp.tanh(wa_in), wup_ref[...]))
    a_all = _sigmoid(a0_ref[...] + _mm(wa_in, aup_ref[...]))
    g_all = _mm(_sigmoid(lg), gup_ref[...])
    kk_raw = k_all * kk_ref[...]
    k2_all = k_all * (1.0 + (a_all - 1.0) * ka_ref[...])
    kk_sq, rk_sum = head_sums([kk_raw * kk_raw, r_all * k2_all * rk_ref[...]])
    kk_all = kk_raw * lax.rsqrt(jnp.maximum(kk_sq, 1e-24))
    b_all = kk_all * a_all
    bonus_all = rk_sum * v_all
    in_chunk = lax.broadcasted_iota(jnp.int32, log_w_all.shape, 0) & (t - 1)
    cum_all = log_w_all
    sh = 1
    while sh < t:
        cum_all = cum_all + jnp.where(in_chunk >= sh, pltpu.roll(cum_all, sh, axis=0), 0.0)
        sh *= 2

    def decays(rows):
        kk, b, log_w, r, k2 = kk_all[rows], b_all[rows], log_w_all[rows], r_all[rows], k2_all[rows]
        cum = cum_all[rows]
        e_cum = jnp.exp(cum)
        e_neg = jnp.exp(-cum)
        e_tail = jnp.exp(cum[t - 1:t, :] - cum)
        return dict(kq=kk * jnp.exp(cum - log_w), rq=r * e_cum, bd=b * e_neg, kd=k2 * e_neg, bt=b * e_tail,
                    kt=k2 * e_tail, v=v_all[rows], p_last=e_cum[t - 1:t, :], bonus=bonus_all[rows], g=g_all[rows])

    pro = [decays(slice(j * t, (j + 1) * t)) for j in range(n_sub)]

    def block_rows(x, bw):
        xb = bf(x)
        blk = lax.broadcasted_iota(jnp.int32, xb.shape, 1) >> (bw.bit_length() - 1)
        return jnp.concatenate([jnp.where(blk == h, xb, jnp.zeros_like(xb)) for h in range(per)], axis=0)

    lane_blk = lax.broadcasted_iota(jnp.int32, (RWKV_HEAD, gw), 1) >> head_shift

    def diag_blocks(z):
        return sum(jnp.where(lane_blk == h, z[h * RWKV_HEAD:(h + 1) * RWKV_HEAD, :], 0.0) for h in range(per))

    probs = [(j, q) for j in range(n_sub) for q in quads]
    ps_ = range(len(probs))
    part = lambda name: [pro[j][name][:, q:q + gw] for j, q in probs]
    kq, rq, bd, kd, bt, kt, v = map(part, ("kq", "rq", "bd", "kd", "bt", "kt", "v"))
    row_m = lax.broadcasted_iota(jnp.int32, (t, mw), 0)
    col_m = lax.broadcasted_iota(jnp.int32, (t, mw), 1) & (t - 1)
    strict = row_m > col_m
    incl = row_m >= col_m

    vbd = [block_rows(v[i], RWKV_HEAD) for i in ps_]
    ms = [_mm(jnp.concatenate([kq[i], rq[i]], axis=0),
              jnp.concatenate([block_rows(bd[i], RWKV_HEAD), block_rows(kd[i], RWKV_HEAD)], axis=0), NT) for i in ps_]
    m_b = [jnp.where(strict, ms[i][:t, :mw], 0.0) for i in ps_]
    mkv = [_mm(jnp.where(strict, ms[i][:t, mw:], 0.0), vbd[i]) for i in ps_]
    l_b = [bf(jnp.where(incl, ms[i][t:, :mw], 0.0)) for i in ps_]
    l_k = [bf(jnp.where(incl, ms[i][t:, mw:], 0.0)) for i in ps_]
    invs = None
    k = 1
    while k < t:
        shift = (2 * k).bit_length() - 1
        join = ((row_m >> shift) == (col_m >> shift)) & ((row_m & k) != 0) & ((col_m & k) == 0)
        cs = [jnp.where(join, m, 0.0) for m in m_b]
        if k == 1:
            invs = [jnp.where(row_m == col_m, 1.0, 0.0) - c for c in cs]
        else:
            xs = [_mm(cs[i], block_rows(invs[i], t)) for i in ps_]
            invs = [invs[i] - _mm(invs[i], block_rows(xs[i], t)) for i in ps_]
        k *= 2
    invs = [bf(x) for x in invs]
    gk = [_mm(invs[i], block_rows(kq[i], RWKV_HEAD)) for i in ps_]
    u0 = [-_mm(invs[i], block_rows(mkv[i], RWKV_HEAD)) for i in ps_]
    rp = [bf(rq[i] - _mm(l_b[i], block_rows(gk[i], RWKV_HEAD))) for i in ps_]
    y0 = [_mm(jnp.concatenate([l_b[i], l_k[i]], axis=1),
              jnp.concatenate([block_rows(u0[i], RWKV_HEAD), vbd[i]], axis=0)) for i in ps_]
    phi = [block_rows(diag_blocks(_mm(gk[i], bt[i], TN)), RWKV_HEAD) for i in ps_]
    psi = [diag_blocks(_mm(jnp.concatenate([u0[i], v[i]], axis=0), jnp.concatenate([bt[i], kt[i]], axis=0), TN))
           for i in ps_]

    states = [state_ref[:, q:q + gw] for q in quads]
    n_q = len(states)
    inv_hd = 1.0 / RWKV_HEAD
    for j in range(n_sub):
        ys = []
        for qi in range(n_q):
            i = j * n_q + qi
            s = states[qi]
            ys.append(y0[i] + _mm(rp[i], block_rows(s, RWKV_HEAD), NT))
            states[qi] = s * pro[j]["p_last"][:, quads[qi]:quads[qi] + gw] - _mm(s, phi[i]) + psi[i]
        y = jnp.concatenate(ys, axis=1)
        dy = y - head_sums([y])[0] * inv_hd
        var = head_sums([dy * dy])[0] * inv_hd
        yn = dy * lax.rsqrt(var + GN_EPS) * gnw_ref[...] + gnb_ref[...]
        o_ref[0, j * t:(j + 1) * t, :] = (yn + pro[j]["bonus"]) * pro[j]["g"]
    for qi, q in enumerate(quads):
        state_ref[:, q:q + gw] = states[qi]

    @pl.when(c == pl.num_programs(1) - 1)
    def _():
        sout_ref[0] = state_ref[...]


def _rwkv(p3, shift_prev, s0, lw):
    bsz, seq, _ = p3.shape
    heads = s0.shape[1]
    r_w = heads * RWKV_HEAD
    cols = 3 * r_w + RANK_W + RANK_A + RANK_G
    t = min(RWKV_CHUNK, seq)
    assert seq % t == 0 and t & (t - 1) == 0
    n_sub = RWKV_SUBCHUNKS if seq % (RWKV_SUBCHUNKS * t) == 0 else 1
    tt = n_sub * t
    row = lambda x: x.reshape(1, -1).astype(F32)
    zeros = jnp.zeros((RANK_W, r_w), F32)
    wup = jnp.concatenate([lw['w_up'], zeros], axis=0).astype(BF16)
    aup = jnp.concatenate([zeros, lw['a_up']], axis=0).astype(BF16)
    vec = lambda n: pl.BlockSpec((1, n), lambda b, c: (0, 0))
    full = lambda a, bb: pl.BlockSpec((a, bb), lambda b, c: (0, 0))
    lanes_hk = lambda s: jnp.swapaxes(s.astype(F32), 1, 2).reshape(bsz, RWKV_HEAD, r_w)
    out, s_new = pl.pallas_call(
        functools.partial(_rwkv_kernel, t=t, n_sub=n_sub, heads=heads),
        grid=(bsz, seq // tt),
        in_specs=[
            pl.BlockSpec((1, tt, cols), lambda b, c: (b, c, 0)),
            pl.BlockSpec((1, 1, cols), lambda b, c: (b, 0, 0)),
            pl.BlockSpec((1, RWKV_HEAD, r_w), lambda b, c: (b, 0, 0)),
            vec(cols), vec(r_w), full(RANK_W + RANK_A, r_w), vec(r_w), full(RANK_W + RANK_A, r_w),
            full(RANK_G, r_w), vec(r_w), vec(r_w), vec(r_w), vec(r_w), vec(r_w),
        ],
        out_specs=[
            pl.BlockSpec((1, tt, r_w), lambda b, c: (b, c, 0)),
            pl.BlockSpec((1, RWKV_HEAD, r_w), lambda b, c: (b, 0, 0)),
        ],
        out_shape=[
            jax.ShapeDtypeStruct((bsz, seq, r_w), F32),
            jax.ShapeDtypeStruct((bsz, RWKV_HEAD, r_w), F32),
        ],
        scratch_shapes=[pltpu.VMEM((1, cols), F32), pltpu.VMEM((RWKV_HEAD, r_w), F32)],
        compiler_params=_params("parallel", "arbitrary"),
        name="rwkv",
    )(p3, shift_prev.astype(F32), lanes_hk(s0), row(lw['shift_mu']), row(lw['w0']), wup, row(lw['a0']), aup,
      lw['g_up'].astype(BF16), row(lw['k_k']), row(lw['k_a']), row(lw['r_k']), row(lw['gn_w']),
      row(lw['gn_b']))
    return out, jnp.swapaxes(s_new.reshape(bsz, RWKV_HEAD, heads, RWKV_HEAD), 1, 2)


def _s5_prep_kernel(are_ref, aim_ref, ldt_ref, btre_ref, btim_ref, cre_ref, cim_ref, d_ref,
                    kst_ref, wre_ref, wim_ref, vre_ref, vimn_ref, lre_ref, lim_ref, *, tc):
    for g in range(are_ref.shape[0]):
        a_re = are_ref[g]
        a_im = aim_ref[g]
        dt = jnp.exp(ldt_ref[g])
        mag = jnp.exp(a_re * dt)
        l_re = mag * jnp.cos(a_im * dt)
        l_im = mag * jnp.sin(a_im * dt)
        den = a_re * a_re + a_im * a_im
        x_re = l_re - 1.0
        co_re = (x_re * a_re + l_im * a_im) / den
        co_im = (l_im * a_re - x_re * a_im) / den
        pw = [(jnp.ones_like(l_re), jnp.zeros_like(l_re))]
        for _ in range(tc):
            q_re, q_im = pw[-1]
            pw.append((q_re * l_re - q_im * l_im, q_re * l_im + q_im * l_re))

        c_re = cre_ref[g]
        c_im = cim_ref[g]
        bt_re = btre_ref[g]
        bt_im = btim_ref[g]
        cc_re = c_re * co_re - c_im * co_im
        cc_im = c_re * co_im + c_im * co_re
        cl_re = jnp.concatenate([cc_re * q[0] - cc_im * q[1] for q in pw[:tc]], axis=0)
        cl_im = jnp.concatenate([cc_re * q[1] + cc_im * q[0] for q in pw[:tc]], axis=0)
        kst = _mm_f32(cl_re, bt_re, NT) - _mm_f32(cl_im, bt_im, NT)
        rr = lax.broadcasted_iota(jnp.int32, kst.shape, 0)
        cc = lax.broadcasted_iota(jnp.int32, kst.shape, 1)
        kst_ref[g] = kst + jnp.where(rr == cc, d_ref[g], 0.0)

        w_re, w_im = [], []
        for s in range(tc):
            q_re, q_im = pw[tc - 1 - s]
            f_re = q_re * co_re - q_im * co_im
            f_im = q_re * co_im + q_im * co_re
            w_re.append(bt_re * f_re - bt_im * f_im)
            w_im.append(bt_re * f_im + bt_im * f_re)
        wre_ref[g] = jnp.concatenate(w_re, axis=0)
        wim_ref[g] = jnp.concatenate(w_im, axis=0)
        vre_ref[g] = jnp.concatenate([c_re * q[0] - c_im * q[1] for q in pw[1:]], axis=0)
        vimn_ref[g] = jnp.concatenate([-(c_re * q[1] + c_im * q[0]) for q in pw[1:]], axis=0)
        lre_ref[g] = pw[tc][0]
        lim_ref[g] = pw[tc][1]


def _s5_prep(lw):
    groups, state = lw['a_re'].shape
    ch = S5_GROUP
    tc = S5_CHUNK
    g3 = lambda x, a, b: x.astype(F32).reshape(groups, a, b)
    gpb = S5_SUB if groups % S5_SUB == 0 else 1
    spec = lambda a, b: pl.BlockSpec((gpb, a, b), lambda g: (g, 0, 0))
    outs = pl.pallas_call(
        functools.partial(_s5_prep_kernel, tc=tc),
        grid=(groups // gpb,),
        in_specs=[spec(1, state), spec(1, state), spec(1, 1), spec(ch, state), spec(ch, state),
                  spec(ch, state), spec(ch, state), spec(1, ch)],
        out_specs=[spec(tc * ch, ch), spec(tc * ch, state), spec(tc * ch, state), spec(tc * ch, state),
                   spec(tc * ch, state), spec(1, state), spec(1, state)],
        out_shape=[jax.ShapeDtypeStruct((groups, tc * ch, ch), F32)]
        + [jax.ShapeDtypeStruct((groups, tc * ch, state), F32)] * 4
        + [jax.ShapeDtypeStruct((groups, 1, state), F32)] * 2,
        compiler_params=_params("parallel"),
        name="s5_prep",
    )(g3(lw['a_re'], 1, state), g3(lw['a_im'], 1, state), g3(lw['log_dt'], 1, 1),
      jnp.swapaxes(lw['b_re'], 1, 2).astype(F32), jnp.swapaxes(lw['b_im'], 1, 2).astype(F32),
      lw['c_re'].astype(F32), lw['c_im'].astype(F32), g3(lw['d'], 1, ch))
    kst, w_re, w_im, v_re, v_imn, l_re, l_im = outs
    k4 = kst.reshape(groups, tc, ch, ch)
    tt = jnp.arange(tc)
    tau = tt[:, None] - tt[None, :]
    blocks = jnp.where((tau >= 0)[None, :, :, None, None], k4[:, jnp.clip(tau, 0, tc - 1)], 0.0)
    toep = jnp.transpose(blocks, (0, 1, 3, 2, 4)).reshape(groups, tc * ch, tc * ch)
    return dict(
        toep=toep.astype(BF16),
        w=jnp.concatenate([w_re, w_im, w_im, w_re], axis=-1).astype(BF16),
        vt=jnp.concatenate([v_re, v_imn], axis=-1).astype(BF16),
        a16=jnp.concatenate([l_re, l_re], axis=-1),
        b16=jnp.concatenate([-l_im, l_im], axis=-1),
    )


def _lane_block_transpose(arrs):
    n = len(arrs)
    assert n * S5_GROUP == LANES and n & (n - 1) == 0
    blk = lax.broadcasted_iota(jnp.int32, arrs[0].shape, 1) >> (S5_GROUP.bit_length() - 1)
    cur = list(arrs)
    b = n // 2
    while b:
        upper = (blk & b) != 0
        nxt = list(cur)
        for i in range(n):
            if i & b:
                continue
            lo, hi = cur[i], cur[i + b]
            nxt[i] = jnp.where(upper, pltpu.roll(hi, S5_GROUP * b, axis=1), lo)
            nxt[i + b] = jnp.where(upper, hi, pltpu.roll(lo, LANES - S5_GROUP * b, axis=1))
        cur = nxt
        b //= 2
    return cur


def _s5_kernel(x_ref, toep_ref, w_ref, vt_ref, a_ref, b_ref, h0_ref, h0s_ref,
               y_ref, hout_ref, x_st, s_st, ex_ref, es_ref, hs_ref, *, nct, bpb):
    ci = pl.program_id(2)
    sub = S5_SUB
    halves = S5_CHUNK // sub
    width = a_ref.shape[-1]
    assert halves == 2
    a = a_ref[0]
    b = b_ref[0]
    steps = min(8, nct)
    rows_of = lambda t: pl.ds(t, nct, stride=S5_CHUNK)
    hi_bits = lambda x: lax.bitcast_convert_type(x.astype(BF16).astype(F32), jnp.int32)
    halves_of = lambda w: (lax.shift_left(w, 16), w & jnp.int32(-65536))

    def scan_step(i, carry):
        x, s = carry
        for j in range(steps):
            rows = pl.ds(pl.multiple_of((i * steps + j) * sub, sub), sub)
            hs_ref[rows, :] = x
            x, s = a * x + b * s + ex_ref[rows, :], a * s - b * x + es_ref[rows, :]
        return x, s

    for bi in range(bpb):
        @pl.when(ci == 0)
        def _():
            x_st[bi] = h0_ref[bi, 0]
            s_st[bi] = h0s_ref[bi, 0]

        words = _lane_block_transpose(
            [lax.shift_right_logical(hi_bits(x_ref[bi, rows_of(tt), :]), 16) | hi_bits(x_ref[bi, rows_of(sub + tt), :])
             for tt in range(sub)])
        us = [jnp.concatenate([lax.bitcast_convert_type(h, F32) for h in halves_of(words[g])], axis=1).astype(BF16)
              for g in range(sub)]

        for g in range(sub):
            e = _mm(us[g], w_ref[g])
            ex_ref[pl.ds(g, nct, stride=sub), :] = e[:, :width]
            es_ref[pl.ds(g, nct, stride=sub), :] = e[:, width:]

        x, s = lax.fori_loop(0, nct // steps, scan_step, (x_st[bi], s_st[bi]))
        x_st[bi] = x
        s_st[bi] = s
        hout_ref[bi, 0] = x

        ys = [_mm(us[g], toep_ref[g], NT) + _mm(hs_ref[pl.ds(g, nct, stride=sub), :], vt_ref[g], NT)
              for g in range(sub)]
        for hf in range(halves):
            outs = _lane_block_transpose([ys[g][:, hf * LANES:(hf + 1) * LANES] for g in range(sub)])
            for tt in range(sub):
                y_ref[bi, rows_of(hf * sub + tt), :] = outs[tt]


def _s5(p3, col0, h_re, h_im, ops):
    bsz, seq, n_in = p3.shape
    groups, state = h_re.shape[1:]
    ch, tc, sub = S5_GROUP, S5_CHUNK, S5_SUB
    assert seq % tc == 0 and groups % sub == 0 and col0 % LANES == 0 and tc % sub == 0
    n_chunks = seq // tc
    nct = min(S5_TILE_CHUNKS, n_chunks)
    assert n_chunks % nct == 0 and nct % min(8, nct) == 0
    gbs = groups // sub
    lane0 = col0 // LANES
    bpb = bsz if n_chunks == nct and bsz * n_chunks <= S5_SMALL_CHUNKS else 1
    pack = lambda x, y: jnp.concatenate([x, y], axis=-1).astype(F32).reshape(bsz, gbs, sub, 2 * state)
    per_g = lambda last: pl.BlockSpec((sub, tc * ch, last), lambda gb, b, ci: (gb, 0, 0))
    vec = pl.BlockSpec((1, sub, 2 * state), lambda gb, b, ci: (gb, 0, 0))
    st = pl.BlockSpec((bpb, 1, sub, 2 * state), lambda gb, b, ci: (b, gb, 0, 0))
    y, h_out = pl.pallas_call(
        functools.partial(_s5_kernel, nct=nct, bpb=bpb),
        grid=(gbs, bsz // bpb, n_chunks // nct),
        in_specs=[pl.BlockSpec((bpb, nct * tc, LANES), lambda gb, b, ci: (b, ci, lane0 + gb)),
                  per_g(tc * ch), per_g(4 * state), per_g(2 * state), vec, vec, st, st],
        out_specs=[pl.BlockSpec((bpb, nct * tc, LANES), lambda gb, b, ci: (b, ci, gb)), st],
        out_shape=[jax.ShapeDtypeStruct((bsz, seq, groups * ch), F32),
                   jax.ShapeDtypeStruct((bsz, gbs, sub, 2 * state), F32)],
        scratch_shapes=[pltpu.VMEM((bpb, sub, 2 * state), F32), pltpu.VMEM((bpb, sub, 2 * state), F32),
                        pltpu.VMEM((nct * sub, 2 * state), F32), pltpu.VMEM((nct * sub, 2 * state), F32),
                        pltpu.VMEM((nct * sub, 2 * state), F32)],
        compiler_params=_params("parallel", "parallel", "arbitrary"),
        name="s5",
    )(p3, ops['toep'], ops['w'], ops['vt'],
      ops['a16'].reshape(gbs, sub, 2 * state), ops['b16'].reshape(gbs, sub, 2 * state),
      pack(h_re, h_im), pack(h_im, h_re))
    h_out = h_out.reshape(bsz, groups, 2 * state)
    return y, h_out[..., :state], h_out[..., state:]


def _gelu_tanh(x):
    return 0.5 * x * (1.0 + jnp.tanh(math.sqrt(2.0 / math.pi) * (x + 0.044715 * (x * x * x))))


def _mix_kernel(x_ref, rw_ref, y_ref, gw_ref, gb_ref, wo1_ref, wo2_ref, g_ref, b_ref, o_ref, *, alpha, rb):
    for r in range(0, x_ref.shape[0], rb):
        rows = slice(r, r + rb)
        y = _gelu_tanh(y_ref[rows, :])
        s5o = y * _sigmoid(_mm(y, gw_ref[...]) + gb_ref[...])
        mix = _mm(rw_ref[rows, :], wo1_ref[...]) + _mm(s5o, wo2_ref[...])
        o_ref[rows, :] = _ln(alpha * x_ref[rows, :] + mix, g_ref[...], b_ref[...])


def _mix(x, rw, y, glu_w, glu_b, w_out, layer, g, b, alpha):
    n, d = x.shape
    r_w = rw.shape[1]
    s_w = y.shape[1]
    tm = _token_tile(n)
    tile = lambda w: pl.BlockSpec((tm, w), lambda i: (i, 0))
    full = lambda a, bb: pl.BlockSpec((a, bb), lambda i: (0, 0), pipeline_mode=pl.Buffered(1))
    slab = lambda a, bb, k: pl.BlockSpec((None, a, bb), lambda i: (layer, k, 0), pipeline_mode=pl.Buffered(1))
    assert r_w == s_w
    return pl.pallas_call(
        functools.partial(_mix_kernel, alpha=alpha, rb=min(tm, ROW_BLOCK)),
        grid=(n // tm,),
        in_specs=[tile(d), tile(r_w), tile(s_w), slab(s_w, s_w, 0), full(1, s_w), slab(r_w, d, 0), slab(s_w, d, 1),
                  full(1, d), full(1, d)],
        out_specs=tile(d),
        out_shape=jax.ShapeDtypeStruct((n, d), F32),
        compiler_params=_params("parallel"),
        name="mix",
    )(x, rw, y, glu_w, glu_b.reshape(1, -1).astype(F32), w_out, w_out, g, b)


def _attn_kernel(x_ref, mk_ref, mv_ref, wq_ref, wo_ref, g_ref, b_ref, o_ref, *, alpha, heads, rb):
    d = x_ref.shape[-1]
    hd = d // heads
    mk = mk_ref[0]
    mv = mv_ref[0]
    for r in range(0, x_ref.shape[0], rb):
        x = x_ref[r:r + rb, :]
        q = _mm(x, wq_ref[...]) * (hd ** -0.5)
        outs = []
        for h in range(heads):
            sl = slice(h * hd, (h + 1) * hd)
            s = _mm(q[:, sl], mk[:, sl], NT)
            e = jnp.exp(s - jnp.max(s, axis=-1, keepdims=True))
            pr = e / jnp.sum(e, axis=-1, keepdims=True)
            outs.append(_mm(pr, mv[:, sl]))
        o = jnp.concatenate(outs, axis=-1)
        o_ref[r:r + rb, :] = _ln(alpha * x + _mm(o, wo_ref[...]), g_ref[...], b_ref[...])


def _attn(x, mk, mv, wq, wo, layer, g, b, alpha, seq):
    n, d = x.shape
    n_mem = mk.shape[1]
    tm = min(_token_tile(n), _token_tile(seq))
    per_batch = seq // tm
    tile = pl.BlockSpec((tm, d), lambda i: (i, 0))
    mem = pl.BlockSpec((1, n_mem, d), lambda i: (i // per_batch, 0, 0))
    full = lambda a, bb: pl.BlockSpec((a, bb), lambda i: (0, 0), pipeline_mode=pl.Buffered(1))
    slab = pl.BlockSpec((None, d, d), lambda i: (layer, 0, 0), pipeline_mode=pl.Buffered(1))
    return pl.pallas_call(
        functools.partial(_attn_kernel, alpha=alpha, heads=XATTN_HEADS, rb=min(tm, ROW_BLOCK)),
        grid=(n // tm,),
        in_specs=[tile, mem, mem, slab, slab, full(1, d), full(1, d)],
        out_specs=tile,
        out_shape=jax.ShapeDtypeStruct((n, d), F32),
        compiler_params=_params("parallel"),
        name="attn",
    )(x, mk.astype(BF16), mv.astype(BF16), wq, wo, g, b)


def _layer(x, mk, mv, shift_prev, s_rwkv, h_re, h_im, lw, wb, layer, s5_ops, alpha):
    bsz, seq, d = x.shape
    n = bsz * seq
    ln = lambda i: (lw['ln_g'][i].reshape(1, d).astype(F32), lw['ln_b'][i].reshape(1, d).astype(F32))
    cols = shift_prev.shape[-1]
    x1 = _ffn_ln(x.reshape(n, d), wb['f1g'], wb['f1u'], wb['f1d'], layer, *ln(0), alpha)
    p = _proj(x1, wb['w_in'], layer).reshape(bsz, seq, -1)
    rw, s_new = _rwkv(p, shift_prev, s_rwkv, lw)
    y, hr, hi = _s5(p, cols, h_re, h_im, s5_ops)
    x2 = _mix(x1, rw.reshape(n, -1), y.reshape(n, -1), wb['glu_w'], lw['glu_b'], wb['w_out'], layer, *ln(1), alpha)
    x3 = _attn(x2, mk.reshape(bsz, -1, d), mv.reshape(bsz, -1, d), wb['xq'], wb['xo'], layer, *ln(2), alpha, seq)
    x4 = _ffn_ln(x3, wb['f2g'], wb['f2u'], wb['f2d'], layer, *ln(3), alpha)
    return x4.reshape(bsz, seq, d), p[:, -1:, :cols], s_new, hr, hi


def kernel(x_prompt, x_sample, mem_prompt, cache_mem_k, cache_mem_v, state_rwkv, cache_shift, state_s5_re, state_s5_im, ln_g, ln_b, ffn1_gate, ffn1_up, ffn1_down, w_in, shift_mu, rwkv_w0, rwkv_w_up, rwkv_a0, rwkv_a_up, rwkv_g_up, rwkv_k_k, rwkv_k_a, rwkv_r_k, rwkv_gn_w, rwkv_gn_b, s5_a_re, s5_a_im, s5_log_dt, s5_b_re, s5_b_im, s5_c_re, s5_c_im, s5_d, s5_glu_w, s5_glu_b, w_mix_out, xattn_q, xattn_k, xattn_v, xattn_o, ffn2_gate, ffn2_up, ffn2_down):
    depth = ln_g.shape[0]
    bp, _, d = x_prompt.shape
    n_mem = mem_prompt.shape[1]
    heads, hd = state_rwkv.shape[2], state_rwkv.shape[3]
    cols = cache_shift.shape[-1]
    groups, state = state_s5_re.shape[2:]
    alpha = (2.0 * depth) ** 0.25
    bf = _to_bf16
    wb = dict(f1g=bf(ffn1_gate), f1u=bf(ffn1_up), f1d=bf(ffn1_down), w_in=bf(w_in), glu_w=bf(s5_glu_w),
              w_out=bf(w_mix_out), xq=bf(xattn_q), xk=bf(xattn_k), xv=bf(xattn_v), xo=bf(xattn_o),
              f2g=bf(ffn2_gate), f2u=bf(ffn2_up), f2d=bf(ffn2_down))
    xp, xs = x_prompt, x_sample
    outs = [[] for _ in range(10)]
    for l in range(depth):
        lw = dict(ln_g=ln_g[l], ln_b=ln_b[l], shift_mu=shift_mu[l], w0=rwkv_w0[l], w_up=rwkv_w_up[l], a0=rwkv_a0[l],
                  a_up=rwkv_a_up[l], g_up=rwkv_g_up[l], k_k=rwkv_k_k[l], k_a=rwkv_k_a[l], r_k=rwkv_r_k[l],
                  gn_w=rwkv_gn_w[l], gn_b=rwkv_gn_b[l], a_re=s5_a_re[l], a_im=s5_a_im[l], log_dt=s5_log_dt[l],
                  b_re=s5_b_re[l], b_im=s5_b_im[l], c_re=s5_c_re[l], c_im=s5_c_im[l], d=s5_d[l],
                  glu_b=s5_glu_b[l])
        s5_ops = _s5_prep(lw)
        mem2 = mem_prompt.reshape(bp * n_mem, d)
        mk_p = _proj(mem2, wb['xk'], l).reshape(bp, n_mem, XATTN_HEADS, d // XATTN_HEADS)
        mv_p = _proj(mem2, wb['xv'], l).reshape(bp, n_mem, XATTN_HEADS, d // XATTN_HEADS)
        xp, sh_p, rw_p, hr_p, hi_p = _layer(
            xp, mk_p, mv_p, jnp.zeros((bp, 1, cols), F32), jnp.zeros((bp, heads, hd, hd), F32),
            jnp.zeros((bp, groups, state), F32), jnp.zeros((bp, groups, state), F32), lw, wb, l, s5_ops, alpha)
        xs, sh_s, rw_s, hr_s, hi_s = _layer(
            xs, cache_mem_k[l], cache_mem_v[l], cache_shift[l], state_rwkv[l],
            state_s5_re[l], state_s5_im[l], lw, wb, l, s5_ops, alpha)
        for acc, val in zip(outs, (mk_p, mv_p, rw_p, sh_p, hr_p, hi_p, rw_s, sh_s, hr_s, hi_s)):
            acc.append(val)
    return (xp, xs) + tuple(jnp.stack(o) for o in outs)
```

```python
import functools
import math

import jax
import jax.numpy as jnp
from jax import lax
from jax.experimental import pallas as pl
from jax.experimental.pallas import tpu as pltpu

F32 = jnp.float32
BF16 = jnp.bfloat16

LN_EPS = 1e-5
GN_EPS = 64e-5
RWKV_HEAD = 64
RANK_W = 64
RANK_A = 64
RANK_G = 128
S5_GROUP = 16
S5_CHUNK = 16
RWKV_CHUNK = 64
RWKV_SUBCHUNKS = 8
XATTN_HEADS = 4
LANES = 128
RWKV_GROUP_LANES = 128
S5_SUB = LANES // S5_GROUP
S5_TILE_CHUNKS = 256
S5_SMALL_CHUNKS = 64
VMEM_LIMIT = 56 * 1024 * 1024
ROW_BLOCK = 256
CAST_BLOCK_BYTES = 6 * 1024 * 1024

NT = (((1,), (1,)), ((), ()))
TN = (((0,), (0,)), ((), ()))


def _mm(a, b, dims=None):
    a = a.astype(BF16)
    b = b.astype(BF16)
    if dims is None:
        return jnp.dot(a, b, preferred_element_type=F32)
    return lax.dot_general(a, b, dims, preferred_element_type=F32)


def _mm_f32(a, b, dims=None):
    if dims is None:
        return jnp.dot(a, b, preferred_element_type=F32, precision=lax.Precision.HIGHEST)
    return lax.dot_general(a, b, dims, preferred_element_type=F32, precision=lax.Precision.HIGHEST)


def _split_bf16(x, parts):
    pieces = []
    for _ in range(parts):
        piece = x.astype(BF16)
        pieces.append(piece)
        x = x - piece.astype(F32)
    return pieces


def _ln(z, g, b):
    mu = jnp.mean(z, axis=-1, keepdims=True)
    d = z - mu
    var = jnp.mean(d * d, axis=-1, keepdims=True)
    return d * lax.rsqrt(var + LN_EPS) * g + b


def _sigmoid(x):
    return 1.0 / (1.0 + jnp.exp(-x))


def _params(*sem):
    return pltpu.CompilerParams(dimension_semantics=sem, vmem_limit_bytes=VMEM_LIMIT)


def _token_tile(n):
    for t in (512, 256, 128, 64, 32, 16, 8):
        if n % t == 0:
            return t
    raise ValueError(f"token count {n} is not a multiple of 8")


def _cast_kernel(x_ref, o_ref):
    o_ref[...] = x_ref[...].astype(BF16)


def _to_bf16(w):
    depth, a, b = w.shape
    rows = depth * a
    tr = 8
    while rows % (2 * tr) == 0 and 2 * tr * b * 4 <= CAST_BLOCK_BYTES:
        tr *= 2
    out = pl.pallas_call(
        _cast_kernel,
        grid=(rows // tr,),
        in_specs=[pl.BlockSpec((tr, b), lambda i: (i, 0))],
        out_specs=pl.BlockSpec((tr, b), lambda i: (i, 0)),
        out_shape=jax.ShapeDtypeStruct((rows, b), BF16),
        compiler_params=_params("parallel"),
        name="to_bf16",
    )(w.reshape(rows, b))
    return out.reshape(depth, a, b)


def _ffn_ln_kernel(x_ref, wg_ref, wu_ref, wd_ref, g_ref, b_ref, o_ref, xb_ref, acc_ref, *,
                   alpha, n_tiles, n_blocks):
    i = pl.program_id(0)
    j = pl.program_id(1)
    slot = i % 2
    rb = o_ref.shape[0] // n_blocks

    def norm_previous_rows():
        r0 = pl.multiple_of((j % n_blocks) * rb, rb)
        z = 0.5 * acc_ref[1 - slot, pl.ds(r0, rb), :]
        o_ref[pl.ds(r0, rb), :] = _ln(z, g_ref[...], b_ref[...])

    @pl.when((i == 0) & (j == 0))
    def _():
        acc_ref[1] = jnp.zeros(acc_ref.shape[1:], F32)

    @pl.when((i < n_tiles) & (j == 0))
    def _():
        x = x_ref[...]
        xb_ref[...] = x.astype(BF16)
        acc_ref[slot] = (2.0 * alpha) * x

    @pl.when(i < n_tiles)
    def _():
        norm_previous_rows()
        xb = xb_ref[...]
        gate = jnp.dot(xb, wg_ref[...], preferred_element_type=F32)
        up = jnp.dot(xb, wu_ref[...], preferred_element_type=F32)
        h = gate * _sigmoid(gate) * up
        acc_ref[slot] += jnp.dot(h.astype(BF16), wd_ref[...], preferred_element_type=F32)

    @pl.when(i == n_tiles)
    def _():
        norm_previous_rows()


def _ffn_ln(x, wg, wu, wd, layer, g, b, alpha):
    n, d = x.shape
    dff = wg.shape[-1]
    tm = _token_tile(n)
    tf = 512 if dff % 512 == 0 else dff
    n_tiles = n // tm
    steps = dff // tf
    n_blocks = 1
    while 2 * n_blocks <= min(steps, 8) and tm % (16 * n_blocks) == 0:
        n_blocks *= 2
    tile_in = lambda i, j: (jnp.minimum(i, n_tiles - 1), 0)
    chunk = lambda i, j: jnp.where(i < n_tiles, j, steps - 1)
    return pl.pallas_call(
        functools.partial(_ffn_ln_kernel, alpha=alpha, n_tiles=n_tiles, n_blocks=n_blocks),
        grid=(n_tiles + 1, steps),
        in_specs=[
            pl.BlockSpec((tm, d), tile_in),
            pl.BlockSpec((None, d, tf), lambda i, j: (layer, 0, chunk(i, j))),
            pl.BlockSpec((None, d, tf), lambda i, j: (layer, 0, chunk(i, j))),
            pl.BlockSpec((None, tf, d), lambda i, j: (layer, chunk(i, j), 0)),
            pl.BlockSpec((1, d), lambda i, j: (0, 0)),
            pl.BlockSpec((1, d), lambda i, j: (0, 0)),
        ],
        out_specs=pl.BlockSpec((tm, d), lambda i, j: (jnp.maximum(i - 1, 0), 0)),
        out_shape=jax.ShapeDtypeStruct((n, d), F32),
        scratch_shapes=[pltpu.VMEM((tm, d), BF16), pltpu.VMEM((2, tm, d), F32)],
        compiler_params=_params("arbitrary", "arbitrary"),
        name="ffn_ln",
    )(x, wg, wu, wd, g, b)


def _proj_kernel(x_ref, w_ref, o_ref):
    o_ref[...] = _mm(x_ref[...], w_ref[...])


def _column_tile(nout, cap=2304):
    best = None
    for t in range(LANES, min(nout, cap) + 1, LANES):
        if nout % t == 0:
            best = t
    return best if best is not None else nout


def _proj(x, w, layer):
    n, d = x.shape
    nout = w.shape[-1]
    tm = _token_tile(n)
    tn = _column_tile(nout)
    return pl.pallas_call(
        _proj_kernel,
        grid=(nout // tn, n // tm),
        in_specs=[
            pl.BlockSpec((tm, d), lambda j, i: (i, 0)),
            pl.BlockSpec((None, d, tn), lambda j, i: (layer, 0, j)),
        ],
        out_specs=pl.BlockSpec((tm, tn), lambda j, i: (i, j)),
        out_shape=jax.ShapeDtypeStruct((n, nout), F32),
        compiler_params=_params("parallel", "parallel"),
        name="proj",
    )(x, w)


def _rwkv_kernel(p_ref, shift_ref, s0_ref, mu_ref, w0_ref, wup_ref, a0_ref, aup_ref, gup_ref,
                 kk_ref, ka_ref, rk_ref, gnw_ref, gnb_ref,
                 o_ref, sout_ref, prev_ref, state_ref, *, t, n_sub, heads):
    c = pl.program_id(1)
    r_w = heads * RWKV_HEAD
    gw = min(RWKV_GROUP_LANES, r_w)
    per = gw // RWKV_HEAD
    mw = per * t
    head_shift = RWKV_HEAD.bit_length() - 1
    bf = lambda x: x.astype(BF16)

    @pl.when(c == 0)
    def _():
        prev_ref[...] = shift_ref[0]
        state_ref[...] = s0_ref[0]

    same_head = ((lax.broadcasted_iota(jnp.int32, (gw, gw), 0) >> head_shift)
                 == (lax.broadcasted_iota(jnp.int32, (gw, gw), 1) >> head_shift))
    ones_bd = jnp.where(same_head, 1.0, 0.0).astype(BF16)

    quads = range(0, r_w, gw)

    def head_sums(xs):
        parts = [piece[:, q:q + gw] for x in xs for piece in _split_bf16(x, 2) for q in quads]
        sums = jnp.dot(jnp.concatenate(parts, axis=0), ones_bd, preferred_element_type=F32)
        outs, r0 = [], 0
        for x in xs:
            rows = x.shape[0]
            hi = [sums[r0 + i * rows:r0 + (i + 1) * rows] for i in range(len(quads))]
            r0 += len(quads) * rows
            lo = [sums[r0 + i * rows:r0 + (i + 1) * rows] for i in range(len(quads))]
            r0 += len(quads) * rows
            outs.append(jnp.concatenate([h + l for h, l in zip(hi, lo)], axis=1))
        return outs

    p = p_ref[0]
    row = lax.broadcasted_iota(jnp.int32, p.shape, 0)
    prev = jnp.where(row == 0, prev_ref[...], pltpu.roll(p, 1, axis=0))
    prev_ref[...] = p[n_sub * t - 1:n_sub * t, :]
    ps = p + mu_ref[...] * (prev - p)
    r_all = ps[:, 0:r_w]
    k_all = ps[:, r_w:2 * r_w]
    v_all = ps[:, 2 * r_w:3 * r_w]
    wa_in = ps[:, 3 * r_w:3 * r_w + RANK_W + RANK_A]
    lg = ps[:, 3 * r_w + RANK_W + RANK_A:]
    log_w_all = -math.exp(-0.5) * _sigmoid(w0_ref[...] + _mm(jnp.tanh(wa_in), wup_ref[...]))
    a_all = _sigmoid(a0_ref[...] + _mm(wa_in, aup_ref[...]))
    g_all = _mm(_sigmoid(lg), gup_ref[...])
    kk_raw = k_all * kk_ref[...]
    k2_all = k_all * (1.0 + (a_all - 1.0) * ka_ref[...])
    kk_sq, rk_sum = head_sums([kk_raw * kk_raw, r_all * k2_all * rk_ref[...]])
    kk_all = kk_raw * lax.rsqrt(jnp.maximum(kk_sq, 1e-24))
    b_all = kk_all * a_all
    bonus_all = rk_sum * v_all
    in_chunk = lax.broadcasted_iota(jnp.int32, log_w_all.shape, 0) & (t - 1)
    cum_all = log_w_all
    sh = 1
    while sh < t:
        cum_all = cum_all + jnp.where(in_chunk >= sh, pltpu.roll(cum_all, sh, axis=0), 0.0)
        sh *= 2

    def decays(rows):
        kk, b, log_w, r, k2 = kk_all[rows], b_all[rows], log_w_all[rows], r_all[rows], k2_all[rows]
        cum = cum_all[rows]
        e_cum = jnp.exp(cum)
        e_neg = jnp.exp(-cum)
        e_tail = jnp.exp(cum[t - 1:t, :] - cum)
        return dict(kq=kk * jnp.exp(cum - log_w), rq=r * e_cum, bd=b * e_neg, kd=k2 * e_neg, bt=b * e_tail,
                    kt=k2 * e_tail, v=v_all[rows], p_last=e_cum[t - 1:t, :], bonus=bonus_all[rows], g=g_all[rows])

    pro = [decays(slice(j * t, (j + 1) * t)) for j in range(n_sub)]

    def block_rows(x, bw):
        xb = bf(x)
        blk = lax.broadcasted_iota(jnp.int32, xb.shape, 1) >> (bw.bit_length() - 1)
        return jnp.concatenate([jnp.where(blk == h, xb, jnp.zeros_like(xb)) for h in range(per)], axis=0)

    lane_blk = lax.broadcasted_iota(jnp.int32, (RWKV_HEAD, gw), 1) >> head_shift

    def diag_blocks(z):
        return sum(jnp.where(lane_blk == h, z[h * RWKV_HEAD:(h + 1) * RWKV_HEAD, :], 0.0) for h in range(per))

    probs = [(j, q) for j in range(n_sub) for q in quads]
    ps_ = range(len(probs))
    part = lambda name: [pro[j][name][:, q:q + gw] for j, q in probs]
    kq, rq, bd, kd, bt, kt, v = map(part, ("kq", "rq", "bd", "kd", "bt", "kt", "v"))
    row_m = lax.broadcasted_iota(jnp.int32, (t, mw), 0)
    col_m = lax.broadcasted_iota(jnp.int32, (t, mw), 1) & (t - 1)
    strict = row_m > col_m
    incl = row_m >= col_m

    vbd = [block_rows(v[i], RWKV_HEAD) for i in ps_]
    ms = [_mm(jnp.concatenate([kq[i], rq[i]], axis=0),
              jnp.concatenate([block_rows(bd[i], RWKV_HEAD), block_rows(kd[i], RWKV_HEAD)], axis=0), NT) for i in ps_]
    m_b = [jnp.where(strict, ms[i][:t, :mw], 0.0) for i in ps_]
    mkv = [_mm(jnp.where(strict, ms[i][:t, mw:], 0.0), vbd[i]) for i in ps_]
    l_b = [bf(jnp.where(incl, ms[i][t:, :mw], 0.0)) for i in ps_]
    l_k = [bf(jnp.where(incl, ms[i][t:, mw:], 0.0)) for i in ps_]
    invs = None
    k = 1
    while k < t:
        shift = (2 * k).bit_length() - 1
        join = ((row_m >> shift) == (col_m >> shift)) & ((row_m & k) != 0) & ((col_m & k) == 0)
        cs = [jnp.where(join, m, 0.0) for m in m_b]
        if k == 1:
            invs = [jnp.where(row_m == col_m, 1.0, 0.0) - c for c in cs]
        else:
            xs = [_mm(cs[i], block_rows(invs[i], t)) for i in ps_]
            invs = [invs[i] - _mm(invs[i], block_rows(xs[i], t)) for i in ps_]
        k *= 2
    invs = [bf(x) for x in invs]
    gk = [_mm(invs[i], block_rows(kq[i], RWKV_HEAD)) for i in ps_]
    u0 = [-_mm(invs[i], block_rows(mkv[i], RWKV_HEAD)) for i in ps_]
    rp = [bf(rq[i] - _mm(l_b[i], block_rows(gk[i], RWKV_HEAD))) for i in ps_]
    y0 = [_mm(jnp.concatenate([l_b[i], l_k[i]], axis=1),
              jnp.concatenate([block_rows(u0[i], RWKV_HEAD), vbd[i]], axis=0)) for i in ps_]
    phi = [block_rows(diag_blocks(_mm(gk[i], bt[i], TN)), RWKV_HEAD) for i in ps_]
    psi = [diag_blocks(_mm(jnp.concatenate([u0[i], v[i]], axis=0), jnp.concatenate([bt[i], kt[i]], axis=0), TN))
           for i in ps_]

    states = [state_ref[:, q:q + gw] for q in quads]
    n_q = len(states)
    inv_hd = 1.0 / RWKV_HEAD
    for j in range(n_sub):
        ys = []
        for qi in range(n_q):
            i = j * n_q + qi
            s = states[qi]
            ys.append(y0[i] + _mm(rp[i], block_rows(s, RWKV_HEAD), NT))
            states[qi] = s * pro[j]["p_last"][:, quads[qi]:quads[qi] + gw] - _mm(s, phi[i]) + psi[i]
        y = jnp.concatenate(ys, axis=1)
        dy = y - head_sums([y])[0] * inv_hd
        var = head_sums([dy * dy])[0] * inv_hd
        yn = dy * lax.rsqrt(var + GN_EPS) * gnw_ref[...] + gnb_ref[...]
        o_ref[0, j * t:(j + 1) * t, :] = (yn + pro[j]["bonus"]) * pro[j]["g"]
    for qi, q in enumerate(quads):
        state_ref[:, q:q + gw] = states[qi]

    @pl.when(c == pl.num_programs(1) - 1)
    def _():
        sout_ref[0] = state_ref[...]


def _rwkv(p3, shift_prev, s0, lw):
    bsz, seq, _ = p3.shape
    heads = s0.shape[1]
    r_w = heads * RWKV_HEAD
    cols = 3 * r_w + RANK_W + RANK_A + RANK_G
    t = min(RWKV_CHUNK, seq)
    assert seq % t == 0 and t & (t - 1) == 0
    n_sub = RWKV_SUBCHUNKS if seq % (RWKV_SUBCHUNKS * t) == 0 else 1
    tt = n_sub * t
    row = lambda x: x.reshape(1, -1).astype(F32)
    zeros = jnp.zeros((RANK_W, r_w), F32)
    wup = jnp.concatenate([lw['w_up'], zeros], axis=0).astype(BF16)
    aup = jnp.concatenate([zeros, lw['a_up']], axis=0).astype(BF16)
    vec = lambda n: pl.BlockSpec((1, n), lambda b, c: (0, 0))
    full = lambda a, bb: pl.BlockSpec((a, bb), lambda b, c: (0, 0))
    lanes_hk = lambda s: jnp.swapaxes(s.astype(F32), 1, 2).reshape(bsz, RWKV_HEAD, r_w)
    out, s_new = pl.pallas_call(
        functools.partial(_rwkv_kernel, t=t, n_sub=n_sub, heads=heads),
        grid=(bsz, seq // tt),
        in_specs=[
            pl.BlockSpec((1, tt, cols), lambda b, c: (b, c, 0)),
            pl.BlockSpec((1, 1, cols), lambda b, c: (b, 0, 0)),
            pl.BlockSpec((1, RWKV_HEAD, r_w), lambda b, c: (b, 0, 0)),
            vec(cols), vec(r_w), full(RANK_W + RANK_A, r_w), vec(r_w), full(RANK_W + RANK_A, r_w),
            full(RANK_G, r_w), vec(r_w), vec(r_w), vec(r_w), vec(r_w), vec(r_w),
        ],
        out_specs=[
            pl.BlockSpec((1, tt, r_w), lambda b, c: (b, c, 0)),
            pl.BlockSpec((1, RWKV_HEAD, r_w), lambda b, c: (b, 0, 0)),
        ],
        out_shape=[
            jax.ShapeDtypeStruct((bsz, seq, r_w), F32),
            jax.ShapeDtypeStruct((bsz, RWKV_HEAD, r_w), F32),
        ],
        scratch_shapes=[pltpu.VMEM((1, cols), F32), pltpu.VMEM((RWKV_HEAD, r_w), F32)],
        compiler_params=_params("parallel", "arbitrary"),
        name="rwkv",
    )(p3, shift_prev.astype(F32), lanes_hk(s0), row(lw['shift_mu']), row(lw['w0']), wup, row(lw['a0']), aup,
      lw['g_up'].astype(BF16), row(lw['k_k']), row(lw['k_a']), row(lw['r_k']), row(lw['gn_w']),
      row(lw['gn_b']))
    return out, jnp.swapaxes(s_new.reshape(bsz, RWKV_HEAD, heads, RWKV_HEAD), 1, 2)


def _s5_prep_kernel(are_ref, aim_ref, ldt_ref, btre_ref, btim_ref, cre_ref, cim_ref, d_ref,
                    kst_ref, wre_ref, wim_ref, vre_ref, vimn_ref, lre_ref, lim_ref, *, tc):
    for g in range(are_ref.shape[0]):
        a_re = are_ref[g]
        a_im = aim_ref[g]
        dt = jnp.exp(ldt_ref[g])
        mag = jnp.exp(a_re * dt)
        l_re = mag * jnp.cos(a_im * dt)
        l_im = mag * jnp.sin(a_im * dt)
        den = a_re * a_re + a_im * a_im
        x_re = l_re - 1.0
        co_re = (x_re * a_re + l_im * a_im) / den
        co_im = (l_im * a_re - x_re * a_im) / den
        pw = [(jnp.ones_like(l_re), jnp.zeros_like(l_re))]
        for _ in range(tc):
            q_re, q_im = pw[-1]
            pw.append((q_re * l_re - q_im * l_im, q_re * l_im + q_im * l_re))

        c_re = cre_ref[g]
        c_im = cim_ref[g]
        bt_re = btre_ref[g]
        bt_im = btim_ref[g]
        cc_re = c_re * co_re - c_im * co_im
        cc_im = c_re * co_im + c_im * co_re
        cl_re = jnp.concatenate([cc_re * q[0] - cc_im * q[1] for q in pw[:tc]], axis=0)
        cl_im = jnp.concatenate([cc_re * q[1] + cc_im * q[0] for q in pw[:tc]], axis=0)
        kst = _mm_f32(cl_re, bt_re, NT) - _mm_f32(cl_im, bt_im, NT)
        rr = lax.broadcasted_iota(jnp.int32, kst.shape, 0)
        cc = lax.broadcasted_iota(jnp.int32, kst.shape, 1)
        kst_ref[g] = kst + jnp.where(rr == cc, d_ref[g], 0.0)

        w_re, w_im = [], []
        for s in range(tc):
            q_re, q_im = pw[tc - 1 - s]
            f_re = q_re * co_re - q_im * co_im
            f_im = q_re * co_im + q_im * co_re
            w_re.append(bt_re * f_re - bt_im * f_im)
            w_im.append(bt_re * f_im + bt_im * f_re)
        wre_ref[g] = jnp.concatenate(w_re, axis=0)
        wim_ref[g] = jnp.concatenate(w_im, axis=0)
        vre_ref[g] = jnp.concatenate([c_re * q[0] - c_im * q[1] for q in pw[1:]], axis=0)
        vimn_ref[g] = jnp.concatenate([-(c_re * q[1] + c_im * q[0]) for q in pw[1:]], axis=0)
        lre_ref[g] = pw[tc][0]
        lim_ref[g] = pw[tc][1]


def _s5_prep(lw):
    groups, state = lw['a_re'].shape
    ch = S5_GROUP
    tc = S5_CHUNK
    g3 = lambda x, a, b: x.astype(F32).reshape(groups, a, b)
    gpb = S5_SUB if groups % S5_SUB == 0 else 1
    spec = lambda a, b: pl.BlockSpec((gpb, a, b), lambda g: (g, 0, 0))
    outs = pl.pallas_call(
        functools.partial(_s5_prep_kernel, tc=tc),
        grid=(groups // gpb,),
        in_specs=[spec(1, state), spec(1, state), spec(1, 1), spec(ch, state), spec(ch, state),
                  spec(ch, state), spec(ch, state), spec(1, ch)],
        out_specs=[spec(tc * ch, ch), spec(tc * ch, state), spec(tc * ch, state), spec(tc * ch, state),
                   spec(tc * ch, state), spec(1, state), spec(1, state)],
        out_shape=[jax.ShapeDtypeStruct((groups, tc * ch, ch), F32)]
        + [jax.ShapeDtypeStruct((groups, tc * ch, state), F32)] * 4
        + [jax.ShapeDtypeStruct((groups, 1, state), F32)] * 2,
        compiler_params=_params("parallel"),
        name="s5_prep",
    )(g3(lw['a_re'], 1, state), g3(lw['a_im'], 1, state), g3(lw['log_dt'], 1, 1),
      jnp.swapaxes(lw['b_re'], 1, 2).astype(F32), jnp.swapaxes(lw['b_im'], 1, 2).astype(F32),
      lw['c_re'].astype(F32), lw['c_im'].astype(F32), g3(lw['d'], 1, ch))
    kst, w_re, w_im, v_re, v_imn, l_re, l_im = outs
    k4 = kst.reshape(groups, tc, ch, ch)
    tt = jnp.arange(tc)
    tau = tt[:, None] - tt[None, :]
    blocks = jnp.where((tau >= 0)[None, :, :, None, None], k4[:, jnp.clip(tau, 0, tc - 1)], 0.0)
    toep = jnp.transpose(blocks, (0, 1, 3, 2, 4)).reshape(groups, tc * ch, tc * ch)
    return dict(
        toep=toep.astype(BF16),
        w=jnp.concatenate([w_re, w_im, w_im, w_re], axis=-1).astype(BF16),
        vt=jnp.concatenate([v_re, v_imn], axis=-1).astype(BF16),
        a16=jnp.concatenate([l_re, l_re], axis=-1),
        b16=jnp.concatenate([-l_im, l_im], axis=-1),
    )


def _lane_block_transpose(arrs):
    n = len(arrs)
    assert n * S5_GROUP == LANES and n & (n - 1) == 0
    blk = lax.broadcasted_iota(jnp.int32, arrs[0].shape, 1) >> (S5_GROUP.bit_length() - 1)
    cur = list(arrs)
    b = n // 2
    while b:
        upper = (blk & b) != 0
        nxt = list(cur)
        for i in range(n):
            if i & b:
                continue
            lo, hi = cur[i], cur[i + b]
            nxt[i] = jnp.where(upper, pltpu.roll(hi, S5_GROUP * b, axis=1), lo)
            nxt[i + b] = jnp.where(upper, hi, pltpu.roll(lo, LANES - S5_GROUP * b, axis=1))
        cur = nxt
        b //= 2
    return cur


def _s5_kernel(x_ref, toep_ref, w_ref, vt_ref, a_ref, b_ref, h0_ref, h0s_ref,
               y_ref, hout_ref, x_st, s_st, ex_ref, es_ref, hs_ref, *, nct, bpb):
    ci = pl.program_id(2)
    sub = S5_SUB
    halves = S5_CHUNK // sub
    width = a_ref.shape[-1]
    assert halves == 2
    a = a_ref[0]
    b = b_ref[0]
    steps = min(8, nct)
    rows_of = lambda t: pl.ds(t, nct, stride=S5_CHUNK)
    hi_bits = lambda x: lax.bitcast_convert_type(x.astype(BF16).astype(F32), jnp.int32)
    halves_of = lambda w: (lax.shift_left(w, 16), w & jnp.int32(-65536))

    def scan_step(i, carry):
        x, s = carry
        for j in range(steps):
            rows = pl.ds(pl.multiple_of((i * steps + j) * sub, sub), sub)
            hs_ref[rows, :] = x
            x, s = a * x + b * s + ex_ref[rows, :], a * s - b * x + es_ref[rows, :]
        return x, s

    for bi in range(bpb):
        @pl.when(ci == 0)
        def _():
            x_st[bi] = h0_ref[bi, 0]
            s_st[bi] = h0s_ref[bi, 0]

        words = _lane_block_transpose(
            [lax.shift_right_logical(hi_bits(x_ref[bi, rows_of(tt), :]), 16) | hi_bits(x_ref[bi, rows_of(sub + tt), :])
             for tt in range(sub)])
        us = [jnp.concatenate([lax.bitcast_convert_type(h, F32) for h in halves_of(words[g])], axis=1).astype(BF16)
              for g in range(sub)]

        for g in range(sub):
            e = _mm(us[g], w_ref[g])
            ex_ref[pl.ds(g, nct, stride=sub), :] = e[:, :width]
            es_ref[pl.ds(g, nct, stride=sub), :] = e[:, width:]

        x, s = lax.fori_loop(0, nct // steps, scan_step, (x_st[bi], s_st[bi]))
        x_st[bi] = x
        s_st[bi] = s
        hout_ref[bi, 0] = x

        ys = [_mm(us[g], toep_ref[g], NT) + _mm(hs_ref[pl.ds(g, nct, stride=sub), :], vt_ref[g], NT)
              for g in range(sub)]
        for hf in range(halves):
            outs = _lane_block_transpose([ys[g][:, hf * LANES:(hf + 1) * LANES] for g in range(sub)])
            for tt in range(sub):
                y_ref[bi, rows_of(hf * sub + tt), :] = outs[tt]


def _s5(p3, col0, h_re, h_im, ops):
    bsz, seq, n_in = p3.shape
    groups, state = h_re.shape[1:]
    ch, tc, sub = S5_GROUP, S5_CHUNK, S5_SUB
    assert seq % tc == 0 and groups % sub == 0 and col0 % LANES == 0 and tc % sub == 0
    n_chunks = seq // tc
    nct = min(S5_TILE_CHUNKS, n_chunks)
    assert n_chunks % nct == 0 and nct % min(8, nct) == 0
    gbs = groups // sub
    lane0 = col0 // LANES
    bpb = bsz if n_chunks == nct and bsz * n_chunks <= S5_SMALL_CHUNKS else 1
    pack = lambda x, y: jnp.concatenate([x, y], axis=-1).astype(F32).reshape(bsz, gbs, sub, 2 * state)
    per_g = lambda last: pl.BlockSpec((sub, tc * ch, last), lambda gb, b, ci: (gb, 0, 0))
    vec = pl.BlockSpec((1, sub, 2 * state), lambda gb, b, ci: (gb, 0, 0))
    st = pl.BlockSpec((bpb, 1, sub, 2 * state), lambda gb, b, ci: (b, gb, 0, 0))
    y, h_out = pl.pallas_call(
        functools.partial(_s5_kernel, nct=nct, bpb=bpb),
        grid=(gbs, bsz // bpb, n_chunks // nct),
        in_specs=[pl.BlockSpec((bpb, nct * tc, LANES), lambda gb, b, ci: (b, ci, lane0 + gb)),
                  per_g(tc * ch), per_g(4 * state), per_g(2 * state), vec, vec, st, st],
        out_specs=[pl.BlockSpec((bpb, nct * tc, LANES), lambda gb, b, ci: (b, ci, gb)), st],
        out_shape=[jax.ShapeDtypeStruct((bsz, seq, groups * ch), F32),
                   jax.ShapeDtypeStruct((bsz, gbs, sub, 2 * state), F32)],
        scratch_shapes=[pltpu.VMEM((bpb, sub, 2 * state), F32), pltpu.VMEM((bpb, sub, 2 * state), F32),
                        pltpu.VMEM((nct * sub, 2 * state), F32), pltpu.VMEM((nct * sub, 2 * state), F32),
                        pltpu.VMEM((nct * sub, 2 * state), F32)],
        compiler_params=_params("parallel", "parallel", "arbitrary"),
        name="s5",
    )(p3, ops['toep'], ops['w'], ops['vt'],
      ops['a16'].reshape(gbs, sub, 2 * state), ops['b16'].reshape(gbs, sub, 2 * state),
      pack(h_re, h_im), pack(h_im, h_re))
    h_out = h_out.reshape(bsz, groups, 2 * state)
    return y, h_out[..., :state], h_out[..., state:]


def _gelu_tanh(x):
    return 0.5 * x * (1.0 + jnp.tanh(math.sqrt(2.0 / math.pi) * (x + 0.044715 * (x * x * x))))


def _mix_kernel(x_ref, rw_ref, y_ref, gw_ref, gb_ref, wo1_ref, wo2_ref, g_ref, b_ref, o_ref, *, alpha, rb):
    for r in range(0, x_ref.shape[0], rb):
        rows = slice(r, r + rb)
        y = _gelu_tanh(y_ref[rows, :])
        s5o = y * _sigmoid(_mm(y, gw_ref[...]) + gb_ref[...])
        mix = _mm(rw_ref[rows, :], wo1_ref[...]) + _mm(s5o, wo2_ref[...])
        o_ref[rows, :] = _ln(alpha * x_ref[rows, :] + mix, g_ref[...], b_ref[...])


def _mix(x, rw, y, glu_w, glu_b, w_out, layer, g, b, alpha):
    n, d = x.shape
    r_w = rw.shape[1]
    s_w = y.shape[1]
    tm = _token_tile(n)
    tile = lambda w: pl.BlockSpec((tm, w), lambda i: (i, 0))
    full = lambda a, bb: pl.BlockSpec((a, bb), lambda i: (0, 0), pipeline_mode=pl.Buffered(1))
    slab = lambda a, bb, k: pl.BlockSpec((None, a, bb), lambda i: (layer, k, 0), pipeline_mode=pl.Buffered(1))
    assert r_w == s_w
    return pl.pallas_call(
        functools.partial(_mix_kernel, alpha=alpha, rb=min(tm, ROW_BLOCK)),
        grid=(n // tm,),
        in_specs=[tile(d), tile(r_w), tile(s_w), slab(s_w, s_w, 0), full(1, s_w), slab(r_w, d, 0), slab(s_w, d, 1),
                  full(1, d), full(1, d)],
        out_specs=tile(d),
        out_shape=jax.ShapeDtypeStruct((n, d), F32),
        compiler_params=_params("parallel"),
        name="mix",
    )(x, rw, y, glu_w, glu_b.reshape(1, -1).astype(F32), w_out, w_out, g, b)


def _attn_kernel(x_ref, mk_ref, mv_ref, wq_ref, wo_ref, g_ref, b_ref, o_ref, *, alpha, heads, rb):
    d = x_ref.shape[-1]
    hd = d // heads
    mk = mk_ref[0]
    mv = mv_ref[0]
    for r in range(0, x_ref.shape[0], rb):
        x = x_ref[r:r + rb, :]
        q = _mm(x, wq_ref[...]) * (hd ** -0.5)
        outs = []
        for h in range(heads):
            sl = slice(h * hd, (h + 1) * hd)
            s = _mm(q[:, sl], mk[:, sl], NT)
            e = jnp.exp(s - jnp.max(s, axis=-1, keepdims=True))
            pr = e / jnp.sum(e, axis=-1, keepdims=True)
            outs.append(_mm(pr, mv[:, sl]))
        o = jnp.concatenate(outs, axis=-1)
        o_ref[r:r + rb, :] = _ln(alpha * x + _mm(o, wo_ref[...]), g_ref[...], b_ref[...])


def _attn(x, mk, mv, wq, wo, layer, g, b, alpha, seq):
    n, d = x.shape
    n_mem = mk.shape[1]
    tm = min(_token_tile(n), _token_tile(seq))
    per_batch = seq // tm
    tile = pl.BlockSpec((tm, d), lambda i: (i, 0))
    mem = pl.BlockSpec((1, n_mem, d), lambda i: (i // per_batch, 0, 0))
    full = lambda a, bb: pl.BlockSpec((a, bb), lambda i: (0, 0), pipeline_mode=pl.Buffered(1))
    slab = pl.BlockSpec((None, d, d), lambda i: (layer, 0, 0), pipeline_mode=pl.Buffered(1))
    return pl.pallas_call(
        functools.partial(_attn_kernel, alpha=alpha, heads=XATTN_HEADS, rb=min(tm, ROW_BLOCK)),
        grid=(n // tm,),
        in_specs=[tile, mem, mem, slab, slab, full(1, d), full(1, d)],
        out_specs=tile,
        out_shape=jax.ShapeDtypeStruct((n, d), F32),
        compiler_params=_params("parallel"),
        name="attn",
    )(x, mk.astype(BF16), mv.astype(BF16), wq, wo, g, b)


def _layer(x, mk, mv, shift_prev, s_rwkv, h_re, h_im, lw, wb, layer, s5_ops, alpha):
    bsz, seq, d = x.shape
    n = bsz * seq
    ln = lambda i: (lw['ln_g'][i].reshape(1, d).astype(F32), lw['ln_b'][i].reshape(1, d).astype(F32))
    cols = shift_prev.shape[-1]
    x1 = _ffn_ln(x.reshape(n, d), wb['f1g'], wb['f1u'], wb['f1d'], layer, *ln(0), alpha)
    p = _proj(x1, wb['w_in'], layer).reshape(bsz, seq, -1)
    rw, s_new = _rwkv(p, shift_prev, s_rwkv, lw)
    y, hr, hi = _s5(p, cols, h_re, h_im, s5_ops)
    x2 = _mix(x1, rw.reshape(n, -1), y.reshape(n, -1), wb['glu_w'], lw['glu_b'], wb['w_out'], layer, *ln(1), alpha)
    x3 = _attn(x2, mk.reshape(bsz, -1, d), mv.reshape(bsz, -1, d), wb['xq'], wb['xo'], layer, *ln(2), alpha, seq)
    x4 = _ffn_ln(x3, wb['f2g'], wb['f2u'], wb['f2d'], layer, *ln(3), alpha)
    return x4.reshape(bsz, seq, d), p[:, -1:, :cols], s_new, hr, hi


def kernel(x_prompt, x_sample, mem_prompt, cache_mem_k, cache_mem_v, state_rwkv, cache_shift, state_s5_re, state_s5_im, ln_g, ln_b, ffn1_gate, ffn1_up, ffn1_down, w_in, shift_mu, rwkv_w0, rwkv_w_up, rwkv_a0, rwkv_a_up, rwkv_g_up, rwkv_k_k, rwkv_k_a, rwkv_r_k, rwkv_gn_w, rwkv_gn_b, s5_a_re, s5_a_im, s5_log_dt, s5_b_re, s5_b_im, s5_c_re, s5_c_im, s5_d, s5_glu_w, s5_glu_b, w_mix_out, xattn_q, xattn_k, xattn_v, xattn_o, ffn2_gate, ffn2_up, ffn2_down):
    depth = ln_g.shape[0]
    bp, _, d = x_prompt.shape
    n_mem = mem_prompt.shape[1]
    heads, hd = state_rwkv.shape[2], state_rwkv.shape[3]
    cols = cache_shift.shape[-1]
    groups, state = state_s5_re.shape[2:]
    alpha = (2.0 * depth) ** 0.25
    bf = _to_bf16
    wb = dict(f1g=bf(ffn1_gate), f1u=bf(ffn1_up), f1d=bf(ffn1_down), w_in=bf(w_in), glu_w=bf(s5_glu_w),
              w_out=bf(w_mix_out), xq=bf(xattn_q), xk=bf(xattn_k), xv=bf(xattn_v), xo=bf(xattn_o),
              f2g=bf(ffn2_gate), f2u=bf(ffn2_up), f2d=bf(ffn2_down))
    xp, xs = x_prompt, x_sample
    outs = [[] for _ in range(10)]
    for l in range(depth):
        lw = dict(ln_g=ln_g[l], ln_b=ln_b[l], shift_mu=shift_mu[l], w0=rwkv_w0[l], w_up=rwkv_w_up[l], a0=rwkv_a0[l],
                  a_up=rwkv_a_up[l], g_up=rwkv_g_up[l], k_k=rwkv_k_k[l], k_a=rwkv_k_a[l], r_k=rwkv_r_k[l],
                  gn_w=rwkv_gn_w[l], gn_b=rwkv_gn_b[l], a_re=s5_a_re[l], a_im=s5_a_im[l], log_dt=s5_log_dt[l],
                  b_re=s5_b_re[l], b_im=s5_b_im[l], c_re=s5_c_re[l], c_im=s5_c_im[l], d=s5_d[l],
                  glu_b=s5_glu_b[l])
        s5_ops = _s5_prep(lw)
        mem2 = mem_prompt.reshape(bp * n_mem, d)
        mk_p = _proj(mem2, wb['xk'], l).reshape(bp, n_mem, XATTN_HEADS, d // XATTN_HEADS)
        mv_p = _proj(mem2, wb['xv'], l).reshape(bp, n_mem, XATTN_HEADS, d // XATTN_HEADS)
        xp, sh_p, rw_p, hr_p, hi_p = _layer(
            xp, mk_p, mv_p, jnp.zeros((bp, 1, cols), F32), jnp.zeros((bp, heads, hd, hd), F32),
            jnp.zeros((bp, groups, state), F32), jnp.zeros((bp, groups, state), F32), lw, wb, l, s5_ops, alpha)
        xs, sh_s, rw_s, hr_s, hi_s = _layer(
            xs, cache_mem_k[l], cache_mem_v[l], cache_shift[l], state_rwkv[l],
            state_s5_re[l], state_s5_im[l], lw, wb, l, s5_ops, alpha)
        for acc, val in zip(outs, (mk_p, mv_p, rw_p, sh_p, hr_p, hi_p, rw_s, sh_s, hr_s, hi_s)):
            acc.append(val)
    return (xp, xs) + tuple(jnp.stack(o) for o in outs)
```

```python
import functools
import math

import jax
import jax.numpy as jnp
from jax import lax
from jax.experimental import pallas as pl
from jax.experimental.pallas import tpu as pltpu

F32 = jnp.float32
BF16 = jnp.bfloat16

LN_EPS = 1e-5
GN_EPS = 64e-5
RWKV_HEAD = 64
RANK_W = 64
RANK_A = 64
RANK_G = 128
S5_GROUP = 16
S5_CHUNK = 16
RWKV_CHUNK = 64
RWKV_SUBCHUNKS = 4
XATTN_HEADS = 4
LANES = 128
RWKV_GROUP_LANES = 128
S5_SUB = LANES // S5_GROUP
S5_TILE_CHUNKS = 256
S5_SMALL_CHUNKS = 64
VMEM_LIMIT = 56 * 1024 * 1024
ROW_BLOCK = 256
CAST_BLOCK_BYTES = 6 * 1024 * 1024

NT = (((1,), (1,)), ((), ()))
TN = (((0,), (0,)), ((), ()))


def _mm(a, b, dims=None):
    a = a.astype(BF16)
    b = b.astype(BF16)
    if dims is None:
        return jnp.dot(a, b, preferred_element_type=F32)
    return lax.dot_general(a, b, dims, preferred_element_type=F32)


def _mm_f32(a, b, dims=None):
    if dims is None:
        return jnp.dot(a, b, preferred_element_type=F32, precision=lax.Precision.HIGHEST)
    return lax.dot_general(a, b, dims, preferred_element_type=F32, precision=lax.Precision.HIGHEST)


def _split_bf16(x, parts):
    pieces = []
    for _ in range(parts):
        piece = x.astype(BF16)
        pieces.append(piece)
        x = x - piece.astype(F32)
    return pieces


def _ln(z, g, b):
    mu = jnp.mean(z, axis=-1, keepdims=True)
    d = z - mu
    var = jnp.mean(d * d, axis=-1, keepdims=True)
    return d * lax.rsqrt(var + LN_EPS) * g + b


def _sigmoid(x):
    return 1.0 / (1.0 + jnp.exp(-x))


def _params(*sem):
    return pltpu.CompilerParams(dimension_semantics=sem, vmem_limit_bytes=VMEM_LIMIT)


def _token_tile(n):
    for t in (512, 256, 128, 64, 32, 16, 8):
        if n % t == 0:
            return t
    raise ValueError(f"token count {n} is not a multiple of 8")


def _cast_kernel(x_ref, o_ref):
    o_ref[...] = x_ref[...].astype(BF16)


def _to_bf16(w):
    depth, a, b = w.shape
    rows = depth * a
    tr = 8
    while rows % (2 * tr) == 0 and 2 * tr * b * 4 <= CAST_BLOCK_BYTES:
        tr *= 2
    out = pl.pallas_call(
        _cast_kernel,
        grid=(rows // tr,),
        in_specs=[pl.BlockSpec((tr, b), lambda i: (i, 0))],
        out_specs=pl.BlockSpec((tr, b), lambda i: (i, 0)),
        out_shape=jax.ShapeDtypeStruct((rows, b), BF16),
        compiler_params=_params("parallel"),
        name="to_bf16",
    )(w.reshape(rows, b))
    return out.reshape(depth, a, b)


def _ffn_ln_kernel(x_ref, wg_ref, wu_ref, wd_ref, g_ref, b_ref, o_ref, xb_ref, acc_ref, *,
                   alpha, n_tiles, n_blocks):
    i = pl.program_id(0)
    j = pl.program_id(1)
    slot = i % 2
    rb = o_ref.shape[0] // n_blocks

    def norm_previous_rows():
        r0 = pl.multiple_of((j % n_blocks) * rb, rb)
        z = 0.5 * acc_ref[1 - slot, pl.ds(r0, rb), :]
        o_ref[pl.ds(r0, rb), :] = _ln(z, g_ref[...], b_ref[...])

    @pl.when((i == 0) & (j == 0))
    def _():
        acc_ref[1] = jnp.zeros(acc_ref.shape[1:], F32)

    @pl.when((i < n_tiles) & (j == 0))
    def _():
        x = x_ref[...]
        xb_ref[...] = x.astype(BF16)
        acc_ref[slot] = (2.0 * alpha) * x

    @pl.when(i < n_tiles)
    def _():
        norm_previous_rows()
        xb = xb_ref[...]
        gate = jnp.dot(xb, wg_ref[...], preferred_element_type=F32)
        up = jnp.dot(xb, wu_ref[...], preferred_element_type=F32)
        h = gate * _sigmoid(gate) * up
        acc_ref[slot] += jnp.dot(h.astype(BF16), wd_ref[...], preferred_element_type=F32)

    @pl.when(i == n_tiles)
    def _():
        norm_previous_rows()


def _ffn_ln(x, wg, wu, wd, layer, g, b, alpha):
    n, d = x.shape
    dff = wg.shape[-1]
    tm = _token_tile(n)
    tf = 512 if dff % 512 == 0 else dff
    n_tiles = n // tm
    steps = dff // tf
    n_blocks = 1
    while 2 * n_blocks <= min(steps, 8) and tm % (16 * n_blocks) == 0:
        n_blocks *= 2
    tile_in = lambda i, j: (jnp.minimum(i, n_tiles - 1), 0)
    chunk = lambda i, j: jnp.where(i < n_tiles, j, steps - 1)
    return pl.pallas_call(
        functools.partial(_ffn_ln_kernel, alpha=alpha, n_tiles=n_tiles, n_blocks=n_blocks),
        grid=(n_tiles + 1, steps),
        in_specs=[
            pl.BlockSpec((tm, d), tile_in),
            pl.BlockSpec((None, d, tf), lambda i, j: (layer, 0, chunk(i, j))),
            pl.BlockSpec((None, d, tf), lambda i, j: (layer, 0, chunk(i, j))),
            pl.BlockSpec((None, tf, d), lambda i, j: (layer, chunk(i, j), 0)),
            pl.BlockSpec((1, d), lambda i, j: (0, 0)),
            pl.BlockSpec((1, d), lambda i, j: (0, 0)),
        ],
        out_specs=pl.BlockSpec((tm, d), lambda i, j: (jnp.maximum(i - 1, 0), 0)),
        out_shape=jax.ShapeDtypeStruct((n, d), F32),
        scratch_shapes=[pltpu.VMEM((tm, d), BF16), pltpu.VMEM((2, tm, d), F32)],
        compiler_params=_params("arbitrary", "arbitrary"),
        name="ffn_ln",
    )(x, wg, wu, wd, g, b)


def _proj_kernel(x_ref, w_ref, o_ref):
    o_ref[...] = _mm(x_ref[...], w_ref[...])


def _column_tile(nout, cap=2304):
    best = None
    for t in range(LANES, min(nout, cap) + 1, LANES):
        if nout % t == 0:
            best = t
    return best if best is not None else nout


def _proj(x, w, layer):
    n, d = x.shape
    nout = w.shape[-1]
    tm = _token_tile(n)
    tn = _column_tile(nout)
    return pl.pallas_call(
        _proj_kernel,
        grid=(nout // tn, n // tm),
        in_specs=[
            pl.BlockSpec((tm, d), lambda j, i: (i, 0)),
            pl.BlockSpec((None, d, tn), lambda j, i: (layer, 0, j)),
        ],
        out_specs=pl.BlockSpec((tm, tn), lambda j, i: (i, j)),
        out_shape=jax.ShapeDtypeStruct((n, nout), F32),
        compiler_params=_params("parallel", "parallel"),
        name="proj",
    )(x, w)


def _rwkv_kernel(p_ref, shift_ref, s0_ref, mu_ref, w0_ref, wup_ref, a0_ref, aup_ref, gup_ref,
                 kk_ref, ka_ref, rk_ref, gnw_ref, gnb_ref,
                 o_ref, sout_ref, prev_ref, state_ref, *, t, n_sub, heads):
    c = pl.program_id(1)
    r_w = heads * RWKV_HEAD
    gw = min(RWKV_GROUP_LANES, r_w)
    per = gw // RWKV_HEAD
    mw = per * t
    head_shift = RWKV_HEAD.bit_length() - 1
    bf = lambda x: x.astype(BF16)

    @pl.when(c == 0)
    def _():
        prev_ref[...] = shift_ref[0]
        state_ref[...] = s0_ref[0]

    same_head = ((lax.broadcasted_iota(jnp.int32, (gw, gw), 0) >> head_shift)
                 == (lax.broadcasted_iota(jnp.int32, (gw, gw), 1) >> head_shift))
    ones_bd = jnp.where(same_head, 1.0, 0.0).astype(BF16)

    quads = range(0, r_w, gw)

    def head_sums(xs):
        parts = [piece[:, q:q + gw] for x in xs for piece in _split_bf16(x, 2) for q in quads]
        sums = jnp.dot(jnp.concatenate(parts, axis=0), ones_bd, preferred_element_type=F32)
        outs, r0 = [], 0
        for x in xs:
            rows = x.shape[0]
            hi = [sums[r0 + i * rows:r0 + (i + 1) * rows] for i in range(len(quads))]
            r0 += len(quads) * rows
            lo = [sums[r0 + i * rows:r0 + (i + 1) * rows] for i in range(len(quads))]
            r0 += len(quads) * rows
            outs.append(jnp.concatenate([h + l for h, l in zip(hi, lo)], axis=1))
        return outs

    p = p_ref[0]
    row = lax.broadcasted_iota(jnp.int32, p.shape, 0)
    prev = jnp.where(row == 0, prev_ref[...], pltpu.roll(p, 1, axis=0))
    prev_ref[...] = p[n_sub * t - 1:n_sub * t, :]
    ps = p + mu_ref[...] * (prev - p)
    r_all = ps[:, 0:r_w]
    k_all = ps[:, r_w:2 * r_w]
    v_all = ps[:, 2 * r_w:3 * r_w]
    wa_in = ps[:, 3 * r_w:3 * r_w + RANK_W + RANK_A]
    lg = ps[:, 3 * r_w + RANK_W + RANK_A:]
    log_w_all = -math.exp(-0.5) * _sigmoid(w0_ref[...] + _mm(jnp.tanh(wa_in), wup_ref[...]))
    a_all = _sigmoid(a0_ref[...] + _mm(wa_in, aup_ref[...]))
    g_all = _mm(_sigmoid(lg), gup_ref[...])
    kk_raw = k_all * kk_ref[...]
    k2_all = k_all * (1.0 + (a_all - 1.0) * ka_ref[...])
    kk_sq, rk_sum = head_sums([kk_raw * kk_raw, r_all * k2_all * rk_ref[...]])
    kk_all = kk_raw * lax.rsqrt(jnp.maximum(kk_sq, 1e-24))
    b_all = kk_all * a_all
    bonus_all = rk_sum * v_all
    in_chunk = lax.broadcasted_iota(jnp.int32, log_w_all.shape, 0) & (t - 1)
    cum_all = log_w_all
    sh = 1
    while sh < t:
        cum_all = cum_all + jnp.where(in_chunk >= sh, pltpu.roll(cum_all, sh, axis=0), 0.0)
        sh *= 2

    def decays(rows):
        kk, b, log_w, r, k2 = kk_all[rows], b_all[rows], log_w_all[rows], r_all[rows], k2_all[rows]
        cum = cum_all[rows]
        e_cum = jnp.exp(cum)
        e_neg = jnp.exp(-cum)
        e_tail = jnp.exp(cum[t - 1:t, :] - cum)
        return dict(kq=kk * jnp.exp(cum - log_w), rq=r * e_cum, bd=b * e_neg, kd=k2 * e_neg, bt=b * e_tail,
                    kt=k2 * e_tail, v=v_all[rows], p_last=e_cum[t - 1:t, :], bonus=bonus_all[rows], g=g_all[rows])

    pro = [decays(slice(j * t, (j + 1) * t)) for j in range(n_sub)]

    def block_rows(x, bw):
        xb = bf(x)
        blk = lax.broadcasted_iota(jnp.int32, xb.shape, 1) >> (bw.bit_length() - 1)
        return jnp.concatenate([jnp.where(blk == h, xb, jnp.zeros_like(xb)) for h in range(per)], axis=0)

    lane_blk = lax.broadcasted_iota(jnp.int32, (RWKV_HEAD, gw), 1) >> head_shift

    def diag_blocks(z):
        return sum(jnp.where(lane_blk == h, z[h * RWKV_HEAD:(h + 1) * RWKV_HEAD, :], 0.0) for h in range(per))

    probs = [(j, q) for j in range(n_sub) for q in quads]
    ps_ = range(len(probs))
    part = lambda name: [pro[j][name][:, q:q + gw] for j, q in probs]
    kq, rq, bd, kd, bt, kt, v = map(part, ("kq", "rq", "bd", "kd", "bt", "kt", "v"))
    row_m = lax.broadcasted_iota(jnp.int32, (t, mw), 0)
    col_m = lax.broadcasted_iota(jnp.int32, (t, mw), 1) & (t - 1)
    strict = row_m > col_m
    incl = row_m >= col_m

    vbd = [block_rows(v[i], RWKV_HEAD) for i in ps_]
    ms = [_mm(jnp.concatenate([kq[i], rq[i]], axis=0),
              jnp.concatenate([block_rows(bd[i], RWKV_HEAD), block_rows(kd[i], RWKV_HEAD)], axis=0), NT) for i in ps_]
    m_b = [jnp.where(strict, ms[i][:t, :mw], 0.0) for i in ps_]
    mkv = [_mm(jnp.where(strict, ms[i][:t, mw:], 0.0), vbd[i]) for i in ps_]
    l_b = [bf(jnp.where(incl, ms[i][t:, :mw], 0.0)) for i in ps_]
    l_k = [bf(jnp.where(incl, ms[i][t:, mw:], 0.0)) for i in ps_]
    invs = None
    k = 1
    while k < t:
        shift = (2 * k).bit_length() - 1
        join = ((row_m >> shift) == (col_m >> shift)) & ((row_m & k) != 0) & ((col_m & k) == 0)
        cs = [jnp.where(join, m, 0.0) for m in m_b]
        if k == 1:
            invs = [jnp.where(row_m == col_m, 1.0, 0.0) - c for c in cs]
        else:
            xs = [_mm(cs[i], block_rows(invs[i], t)) for i in ps_]
            invs = [invs[i] - _mm(invs[i], block_rows(xs[i], t)) for i in ps_]
        k *= 2
    invs = [bf(x) for x in invs]
    gu = [_mm(invs[i], jnp.concatenate([block_rows(kq[i], RWKV_HEAD), block_rows(mkv[i], RWKV_HEAD)], axis=1))
          for i in ps_]
    gk = [x[:, :gw] for x in gu]
    u0 = [-x[:, gw:] for x in gu]
    rp = [bf(rq[i] - _mm(l_b[i], block_rows(gk[i], RWKV_HEAD))) for i in ps_]
    y0 = [_mm(jnp.concatenate([l_b[i], l_k[i]], axis=1),
              jnp.concatenate([block_rows(u0[i], RWKV_HEAD), vbd[i]], axis=0)) for i in ps_]
    phi = [block_rows(diag_blocks(_mm(gk[i], bt[i], TN)), RWKV_HEAD) for i in ps_]
    psi = [diag_blocks(_mm(jnp.concatenate([u0[i], v[i]], axis=0), jnp.concatenate([bt[i], kt[i]], axis=0), TN))
           for i in ps_]

    states = [state_ref[:, q:q + gw] for q in quads]
    n_q = len(states)
    inv_hd = 1.0 / RWKV_HEAD
    for j in range(n_sub):
        ys = []
        for qi in range(n_q):
            i = j * n_q + qi
            s = states[qi]
            ys.append(y0[i] + _mm(rp[i], block_rows(s, RWKV_HEAD), NT))
            states[qi] = s * pro[j]["p_last"][:, quads[qi]:quads[qi] + gw] - _mm(s, phi[i]) + psi[i]
        y = jnp.concatenate(ys, axis=1)
        dy = y - head_sums([y])[0] * inv_hd
        var = head_sums([dy * dy])[0] * inv_hd
        yn = dy * lax.rsqrt(var + GN_EPS) * gnw_ref[...] + gnb_ref[...]
        o_ref[0, j * t:(j + 1) * t, :] = (yn + pro[j]["bonus"]) * pro[j]["g"]
    for qi, q in enumerate(quads):
        state_ref[:, q:q + gw] = states[qi]

    @pl.when(c == pl.num_programs(1) - 1)
    def _():
        sout_ref[0] = state_ref[...]


def _rwkv(p3, shift_prev, s0, lw):
    bsz, seq, _ = p3.shape
    heads = s0.shape[1]
    r_w = heads * RWKV_HEAD
    cols = 3 * r_w + RANK_W + RANK_A + RANK_G
    t = min(RWKV_CHUNK, seq)
    assert seq % t == 0 and t & (t - 1) == 0
    n_sub = RWKV_SUBCHUNKS if seq % (RWKV_SUBCHUNKS * t) == 0 else 1
    tt = n_sub * t
    row = lambda x: x.reshape(1, -1).astype(F32)
    zeros = jnp.zeros((RANK_W, r_w), F32)
    wup = jnp.concatenate([lw['w_up'], zeros], axis=0).astype(BF16)
    aup = jnp.concatenate([zeros, lw['a_up']], axis=0).astype(BF16)
    vec = lambda n: pl.BlockSpec((1, n), lambda b, c: (0, 0))
    full = lambda a, bb: pl.BlockSpec((a, bb), lambda b, c: (0, 0))
    lanes_hk = lambda s: jnp.swapaxes(s.astype(F32), 1, 2).reshape(bsz, RWKV_HEAD, r_w)
    out, s_new = pl.pallas_call(
        functools.partial(_rwkv_kernel, t=t, n_sub=n_sub, heads=heads),
        grid=(bsz, seq // tt),
        in_specs=[
            pl.BlockSpec((1, tt, cols), lambda b, c: (b, c, 0)),
            pl.BlockSpec((1, 1, cols), lambda b, c: (b, 0, 0)),
            pl.BlockSpec((1, RWKV_HEAD, r_w), lambda b, c: (b, 0, 0)),
            vec(cols), vec(r_w), full(RANK_W + RANK_A, r_w), vec(r_w), full(RANK_W + RANK_A, r_w),
            full(RANK_G, r_w), vec(r_w), vec(r_w), vec(r_w), vec(r_w), vec(r_w),
        ],
        out_specs=[
            pl.BlockSpec((1, tt, r_w), lambda b, c: (b, c, 0)),
            pl.BlockSpec((1, RWKV_HEAD, r_w), lambda b, c: (b, 0, 0)),
        ],
        out_shape=[
            jax.ShapeDtypeStruct((bsz, seq, r_w), F32),
            jax.ShapeDtypeStruct((bsz, RWKV_HEAD, r_w), F32),
        ],
        scratch_shapes=[pltpu.VMEM((1, cols), F32), pltpu.VMEM((RWKV_HEAD, r_w), F32)],
        compiler_params=_params("parallel", "arbitrary"),
        name="rwkv",
    )(p3, shift_prev.astype(F32), lanes_hk(s0), row(lw['shift_mu']), row(lw['w0']), wup, row(lw['a0']), aup,
      lw['g_up'].astype(BF16), row(lw['k_k']), row(lw['k_a']), row(lw['r_k']), row(lw['gn_w']),
      row(lw['gn_b']))
    return out, jnp.swapaxes(s_new.reshape(bsz, RWKV_HEAD, heads, RWKV_HEAD), 1, 2)


def _s5_prep_kernel(are_ref, aim_ref, ldt_ref, btre_ref, btim_ref, cre_ref, cim_ref, d_ref,
                    kst_ref, wre_ref, wim_ref, vre_ref, vimn_ref, lre_ref, lim_ref, *, tc):
    for g in range(are_ref.shape[0]):
        a_re = are_ref[g]
        a_im = aim_ref[g]
        dt = jnp.exp(ldt_ref[g])
        mag = jnp.exp(a_re * dt)
        l_re = mag * jnp.cos(a_im * dt)
        l_im = mag * jnp.sin(a_im * dt)
        den = a_re * a_re + a_im * a_im
        x_re = l_re - 1.0
        co_re = (x_re * a_re + l_im * a_im) / den
        co_im = (l_im * a_re - x_re * a_im) / den
        pw = [(jnp.ones_like(l_re), jnp.zeros_like(l_re))]
        for _ in range(tc):
            q_re, q_im = pw[-1]
            pw.append((q_re * l_re - q_im * l_im, q_re * l_im + q_im * l_re))

        c_re = cre_ref[g]
        c_im = cim_ref[g]
        bt_re = btre_ref[g]
        bt_im = btim_ref[g]
        cc_re = c_re * co_re - c_im * co_im
        cc_im = c_re * co_im + c_im * co_re
        cl_re = jnp.concatenate([cc_re * q[0] - cc_im * q[1] for q in pw[:tc]], axis=0)
        cl_im = jnp.concatenate([cc_re * q[1] + cc_im * q[0] for q in pw[:tc]], axis=0)
        kst = _mm_f32(cl_re, bt_re, NT) - _mm_f32(cl_im, bt_im, NT)
        rr = lax.broadcasted_iota(jnp.int32, kst.shape, 0)
        cc = lax.broadcasted_iota(jnp.int32, kst.shape, 1)
        kst_ref[g] = kst + jnp.where(rr == cc, d_ref[g], 0.0)

        w_re, w_im = [], []
        for s in range(tc):
            q_re, q_im = pw[tc - 1 - s]
            f_re = q_re * co_re - q_im * co_im
            f_im = q_re * co_im + q_im * co_re
            w_re.append(bt_re * f_re - bt_im * f_im)
            w_im.append(bt_re * f_im + bt_im * f_re)
        wre_ref[g] = jnp.concatenate(w_re, axis=0)
        wim_ref[g] = jnp.concatenate(w_im, axis=0)
        vre_ref[g] = jnp.concatenate([c_re * q[0] - c_im * q[1] for q in pw[1:]], axis=0)
        vimn_ref[g] = jnp.concatenate([-(c_re * q[1] + c_im * q[0]) for q in pw[1:]], axis=0)
        lre_ref[g] = pw[tc][0]
        lim_ref[g] = pw[tc][1]


def _s5_prep(lw):
    groups, state = lw['a_re'].shape
    ch = S5_GROUP
    tc = S5_CHUNK
    g3 = lambda x, a, b: x.astype(F32).reshape(groups, a, b)
    gpb = S5_SUB if groups % S5_SUB == 0 else 1
    spec = lambda a, b: pl.BlockSpec((gpb, a, b), lambda g: (g, 0, 0))
    outs = pl.pallas_call(
        functools.partial(_s5_prep_kernel, tc=tc),
        grid=(groups // gpb,),
        in_specs=[spec(1, state), spec(1, state), spec(1, 1), spec(ch, state), spec(ch, state),
                  spec(ch, state), spec(ch, state), spec(1, ch)],
        out_specs=[spec(tc * ch, ch), spec(tc * ch, state), spec(tc * ch, state), spec(tc * ch, state),
                   spec(tc * ch, state), spec(1, state), spec(1, state)],
        out_shape=[jax.ShapeDtypeStruct((groups, tc * ch, ch), F32)]
        + [jax.ShapeDtypeStruct((groups, tc * ch, state), F32)] * 4
        + [jax.ShapeDtypeStruct((groups, 1, state), F32)] * 2,
        compiler_params=_params("parallel"),
        name="s5_prep",
    )(g3(lw['a_re'], 1, state), g3(lw['a_im'], 1, state), g3(lw['log_dt'], 1, 1),
      jnp.swapaxes(lw['b_re'], 1, 2).astype(F32), jnp.swapaxes(lw['b_im'], 1, 2).astype(F32),
      lw['c_re'].astype(F32), lw['c_im'].astype(F32), g3(lw['d'], 1, ch))
    kst, w_re, w_im, v_re, v_imn, l_re, l_im = outs
    k4 = kst.reshape(groups, tc, ch, ch)
    tt = jnp.arange(tc)
    tau = tt[:, None] - tt[None, :]
    blocks = jnp.where((tau >= 0)[None, :, :, None, None], k4[:, jnp.clip(tau, 0, tc - 1)], 0.0)
    toep = jnp.transpose(blocks, (0, 1, 3, 2, 4)).reshape(groups, tc * ch, tc * ch)
    return dict(
        toep=toep.astype(BF16),
        w=jnp.concatenate([w_re, w_im, w_im, w_re], axis=-1).astype(BF16),
        vt=jnp.concatenate([v_re, v_imn], axis=-1).astype(BF16),
        a16=jnp.concatenate([l_re, l_re], axis=-1),
        b16=jnp.concatenate([-l_im, l_im], axis=-1),
    )


def _lane_block_transpose(arrs):
    n = len(arrs)
    assert n * S5_GROUP == LANES and n & (n - 1) == 0
    blk = lax.broadcasted_iota(jnp.int32, arrs[0].shape, 1) >> (S5_GROUP.bit_length() - 1)
    cur = list(arrs)
    b = n // 2
    while b:
        upper = (blk & b) != 0
        nxt = list(cur)
        for i in range(n):
            if i & b:
                continue
            lo, hi = cur[i], cur[i + b]
            nxt[i] = jnp.where(upper, pltpu.roll(hi, S5_GROUP * b, axis=1), lo)
            nxt[i + b] = jnp.where(upper, hi, pltpu.roll(lo, LANES - S5_GROUP * b, axis=1))
        cur = nxt
        b //= 2
    return cur


def _s5_kernel(x_ref, toep_ref, w_ref, vt_ref, a_ref, b_ref, h0_ref, h0s_ref,
               y_ref, hout_ref, x_st, s_st, ex_ref, es_ref, hs_ref, *, nct, bpb):
    ci = pl.program_id(2)
    sub = S5_SUB
    halves = S5_CHUNK // sub
    width = a_ref.shape[-1]
    assert halves == 2
    a = a_ref[0]
    b = b_ref[0]
    steps = min(8, nct)
    rows_of = lambda t: pl.ds(t, nct, stride=S5_CHUNK)
    hi_bits = lambda x: lax.bitcast_convert_type(x.astype(BF16).astype(F32), jnp.int32)
    halves_of = lambda w: (lax.shift_left(w, 16), w & jnp.int32(-65536))

    def scan_step(i, carry):
        x, s = carry
        for j in range(steps):
            rows = pl.ds(pl.multiple_of((i * steps + j) * sub, sub), sub)
            hs_ref[rows, :] = x
            x, s = a * x + b * s + ex_ref[rows, :], a * s - b * x + es_ref[rows, :]
        return x, s

    for bi in range(bpb):
        @pl.when(ci == 0)
        def _():
            x_st[bi] = h0_ref[bi, 0]
            s_st[bi] = h0s_ref[bi, 0]

        words = _lane_block_transpose(
            [lax.shift_right_logical(hi_bits(x_ref[bi, rows_of(tt), :]), 16) | hi_bits(x_ref[bi, rows_of(sub + tt), :])
             for tt in range(sub)])
        us = [jnp.concatenate([lax.bitcast_convert_type(h, F32) for h in halves_of(words[g])], axis=1).astype(BF16)
              for g in range(sub)]

        for g in range(sub):
            e = _mm(us[g], w_ref[g])
            ex_ref[pl.ds(g, nct, stride=sub), :] = e[:, :width]
            es_ref[pl.ds(g, nct, stride=sub), :] = e[:, width:]

        x, s = lax.fori_loop(0, nct // steps, scan_step, (x_st[bi], s_st[bi]))
        x_st[bi] = x
        s_st[bi] = s
        hout_ref[bi, 0] = x

        ys = [_mm(us[g], toep_ref[g], NT) + _mm(hs_ref[pl.ds(g, nct, stride=sub), :], vt_ref[g], NT)
              for g in range(sub)]
        for hf in range(halves):
            outs = _lane_block_transpose([ys[g][:, hf * LANES:(hf + 1) * LANES] for g in range(sub)])
            for tt in range(sub):
                y_ref[bi, rows_of(hf * sub + tt), :] = outs[tt]


def _s5(p3, col0, h_re, h_im, ops):
    bsz, seq, n_in = p3.shape
    groups, state = h_re.shape[1:]
    ch, tc, sub = S5_GROUP, S5_CHUNK, S5_SUB
    assert seq % tc == 0 and groups % sub == 0 and col0 % LANES == 0 and tc % sub == 0
    n_chunks = seq // tc
    nct = min(S5_TILE_CHUNKS, n_chunks)
    assert n_chunks % nct == 0 and nct % min(8, nct) == 0
    gbs = groups // sub
    lane0 = col0 // LANES
    bpb = bsz if n_chunks == nct and bsz * n_chunks <= S5_SMALL_CHUNKS else 1
    pack = lambda x, y: jnp.concatenate([x, y], axis=-1).astype(F32).reshape(bsz, gbs, sub, 2 * state)
    per_g = lambda last: pl.BlockSpec((sub, tc * ch, last), lambda gb, b, ci: (gb, 0, 0))
    vec = pl.BlockSpec((1, sub, 2 * state), lambda gb, b, ci: (gb, 0, 0))
    st = pl.BlockSpec((bpb, 1, sub, 2 * state), lambda gb, b, ci: (b, gb, 0, 0))
    y, h_out = pl.pallas_call(
        functools.partial(_s5_kernel, nct=nct, bpb=bpb),
        grid=(gbs, bsz // bpb, n_chunks // nct),
        in_specs=[pl.BlockSpec((bpb, nct * tc, LANES), lambda gb, b, ci: (b, ci, lane0 + gb)),
                  per_g(tc * ch), per_g(4 * state), per_g(2 * state), vec, vec, st, st],
        out_specs=[pl.BlockSpec((bpb, nct * tc, LANES), lambda gb, b, ci: (b, ci, gb)), st],
        out_shape=[jax.ShapeDtypeStruct((bsz, seq, groups * ch), F32),
                   jax.ShapeDtypeStruct((bsz, gbs, sub, 2 * state), F32)],
        scratch_shapes=[pltpu.VMEM((bpb, sub, 2 * state), F32), pltpu.VMEM((bpb, sub, 2 * state), F32),
                        pltpu.VMEM((nct * sub, 2 * state), F32), pltpu.VMEM((nct * sub, 2 * state), F32),
                        pltpu.VMEM((nct * sub, 2 * state), F32)],
        compiler_params=_params("parallel", "parallel", "arbitrary"),
        name="s5",
    )(p3, ops['toep'], ops['w'], ops['vt'],
      ops['a16'].reshape(gbs, sub, 2 * state), ops['b16'].reshape(gbs, sub, 2 * state),
      pack(h_re, h_im), pack(h_im, h_re))
    h_out = h_out.reshape(bsz, groups, 2 * state)
    return y, h_out[..., :state], h_out[..., state:]


def _gelu_tanh(x):
    return 0.5 * x * (1.0 + jnp.tanh(math.sqrt(2.0 / math.pi) * (x + 0.044715 * (x * x * x))))


def _mix_kernel(x_ref, rw_ref, y_ref, gw_ref, gb_ref, wo1_ref, wo2_ref, g_ref, b_ref, o_ref, *, alpha, rb):
    for r in range(0, x_ref.shape[0], rb):
        rows = slice(r, r + rb)
        y = _gelu_tanh(y_ref[rows, :])
        s5o = y * _sigmoid(_mm(y, gw_ref[...]) + gb_ref[...])
        mix = _mm(rw_ref[rows, :], wo1_ref[...]) + _mm(s5o, wo2_ref[...])
        o_ref[rows, :] = _ln(alpha * x_ref[rows, :] + mix, g_ref[...], b_ref[...])


def _mix(x, rw, y, glu_w, glu_b, w_out, layer, g, b, alpha):
    n, d = x.shape
    r_w = rw.shape[1]
    s_w = y.shape[1]
    tm = _token_tile(n)
    tile = lambda w: pl.BlockSpec((tm, w), lambda i: (i, 0))
    full = lambda a, bb: pl.BlockSpec((a, bb), lambda i: (0, 0), pipeline_mode=pl.Buffered(1))
    slab = lambda a, bb, k: pl.BlockSpec((None, a, bb), lambda i: (layer, k, 0), pipeline_mode=pl.Buffered(1))
    assert r_w == s_w
    return pl.pallas_call(
        functools.partial(_mix_kernel, alpha=alpha, rb=min(tm, ROW_BLOCK)),
        grid=(n // tm,),
        in_specs=[tile(d), tile(r_w), tile(s_w), slab(s_w, s_w, 0), full(1, s_w), slab(r_w, d, 0), slab(s_w, d, 1),
                  full(1, d), full(1, d)],
        out_specs=tile(d),
        out_shape=jax.ShapeDtypeStruct((n, d), F32),
        compiler_params=_params("parallel"),
        name="mix",
    )(x, rw, y, glu_w, glu_b.reshape(1, -1).astype(F32), w_out, w_out, g, b)


def _attn_kernel(x_ref, mk_ref, mv_ref, wq_ref, wo_ref, g_ref, b_ref, o_ref, *, alpha, heads, rb):
    d = x_ref.shape[-1]
    hd = d // heads
    mk = mk_ref[0]
    mv = mv_ref[0]
    for r in range(0, x_ref.shape[0], rb):
        x = x_ref[r:r + rb, :]
        q = _mm(x, wq_ref[...]) * (hd ** -0.5)
        outs = []
        for h in range(heads):
            sl = slice(h * hd, (h + 1) * hd)
            s = _mm(q[:, sl], mk[:, sl], NT)
            e = jnp.exp(s - jnp.max(s, axis=-1, keepdims=True))
            pr = e / jnp.sum(e, axis=-1, keepdims=True)
            outs.append(_mm(pr, mv[:, sl]))
        o = jnp.concatenate(outs, axis=-1)
        o_ref[r:r + rb, :] = _ln(alpha * x + _mm(o, wo_ref[...]), g_ref[...], b_ref[...])


def _attn(x, mk, mv, wq, wo, layer, g, b, alpha, seq):
    n, d = x.shape
    n_mem = mk.shape[1]
    tm = min(_token_tile(n), _token_tile(seq))
    per_batch = seq // tm
    tile = pl.BlockSpec((tm, d), lambda i: (i, 0))
    mem = pl.BlockSpec((1, n_mem, d), lambda i: (i // per_batch, 0, 0))
    full = lambda a, bb: pl.BlockSpec((a, bb), lambda i: (0, 0), pipeline_mode=pl.Buffered(1))
    slab = pl.BlockSpec((None, d, d), lambda i: (layer, 0, 0), pipeline_mode=pl.Buffered(1))
    return pl.pallas_call(
        functools.partial(_attn_kernel, alpha=alpha, heads=XATTN_HEADS, rb=min(tm, ROW_BLOCK)),
        grid=(n // tm,),
        in_specs=[tile, mem, mem, slab, slab, full(1, d), full(1, d)],
        out_specs=tile,
        out_shape=jax.ShapeDtypeStruct((n, d), F32),
        compiler_params=_params("parallel"),
        name="attn",
    )(x, mk.astype(BF16), mv.astype(BF16), wq, wo, g, b)


def _layer(x, mk, mv, shift_prev, s_rwkv, h_re, h_im, lw, wb, layer, s5_ops, alpha):
    bsz, seq, d = x.shape
    n = bsz * seq
    ln = lambda i: (lw['ln_g'][i].reshape(1, d).astype(F32), lw['ln_b'][i].reshape(1, d).astype(F32))
    cols = shift_prev.shape[-1]
    x1 = _ffn_ln(x.reshape(n, d), wb['f1g'], wb['f1u'], wb['f1d'], layer, *ln(0), alpha)
    p = _proj(x1, wb['w_in'], layer).reshape(bsz, seq, -1)
    rw, s_new = _rwkv(p, shift_prev, s_rwkv, lw)
    y, hr, hi = _s5(p, cols, h_re, h_im, s5_ops)
    x2 = _mix(x1, rw.reshape(n, -1), y.reshape(n, -1), wb['glu_w'], lw['glu_b'], wb['w_out'], layer, *ln(1), alpha)
    x3 = _attn(x2, mk.reshape(bsz, -1, d), mv.reshape(bsz, -1, d), wb['xq'], wb['xo'], layer, *ln(2), alpha, seq)
    x4 = _ffn_ln(x3, wb['f2g'], wb['f2u'], wb['f2d'], layer, *ln(3), alpha)
    return x4.reshape(bsz, seq, d), p[:, -1:, :cols], s_new, hr, hi


def kernel(x_prompt, x_sample, mem_prompt, cache_mem_k, cache_mem_v, state_rwkv, cache_shift, state_s5_re, state_s5_im, ln_g, ln_b, ffn1_gate, ffn1_up, ffn1_down, w_in, shift_mu, rwkv_w0, rwkv_w_up, rwkv_a0, rwkv_a_up, rwkv_g_up, rwkv_k_k, rwkv_k_a, rwkv_r_k, rwkv_gn_w, rwkv_gn_b, s5_a_re, s5_a_im, s5_log_dt, s5_b_re, s5_b_im, s5_c_re, s5_c_im, s5_d, s5_glu_w, s5_glu_b, w_mix_out, xattn_q, xattn_k, xattn_v, xattn_o, ffn2_gate, ffn2_up, ffn2_down):
    depth = ln_g.shape[0]
    bp, _, d = x_prompt.shape
    n_mem = mem_prompt.shape[1]
    heads, hd = state_rwkv.shape[2], state_rwkv.shape[3]
    cols = cache_shift.shape[-1]
    groups, state = state_s5_re.shape[2:]
    alpha = (2.0 * depth) ** 0.25
    bf = _to_bf16
    wb = dict(f1g=bf(ffn1_gate), f1u=bf(ffn1_up), f1d=bf(ffn1_down), w_in=bf(w_in), glu_w=bf(s5_glu_w),
              w_out=bf(w_mix_out), xq=bf(xattn_q), xk=bf(xattn_k), xv=bf(xattn_v), xo=bf(xattn_o),
              f2g=bf(ffn2_gate), f2u=bf(ffn2_up), f2d=bf(ffn2_down))
    xp, xs = x_prompt, x_sample
    outs = [[] for _ in range(10)]
    for l in range(depth):
        lw = dict(ln_g=ln_g[l], ln_b=ln_b[l], shift_mu=shift_mu[l], w0=rwkv_w0[l], w_up=rwkv_w_up[l], a0=rwkv_a0[l],
                  a_up=rwkv_a_up[l], g_up=rwkv_g_up[l], k_k=rwkv_k_k[l], k_a=rwkv_k_a[l], r_k=rwkv_r_k[l],
                  gn_w=rwkv_gn_w[l], gn_b=rwkv_gn_b[l], a_re=s5_a_re[l], a_im=s5_a_im[l], log_dt=s5_log_dt[l],
                  b_re=s5_b_re[l], b_im=s5_b_im[l], c_re=s5_c_re[l], c_im=s5_c_im[l], d=s5_d[l],
                  glu_b=s5_glu_b[l])
        s5_ops = _s5_prep(lw)
        mem2 = mem_prompt.reshape(bp * n_mem, d)
        mk_p = _proj(mem2, wb['xk'], l).reshape(bp, n_mem, XATTN_HEADS, d // XATTN_HEADS)
        mv_p = _proj(mem2, wb['xv'], l).reshape(bp, n_mem, XATTN_HEADS, d // XATTN_HEADS)
        xp, sh_p, rw_p, hr_p, hi_p = _layer(
            xp, mk_p, mv_p, jnp.zeros((bp, 1, cols), F32), jnp.zeros((bp, heads, hd, hd), F32),
            jnp.zeros((bp, groups, state), F32), jnp.zeros((bp, groups, state), F32), lw, wb, l, s5_ops, alpha)
        xs, sh_s, rw_s, hr_s, hi_s = _layer(
            xs, cache_mem_k[l], cache_mem_v[l], cache_shift[l], state_rwkv[l],
            state_s5_re[l], state_s5_im[l], lw, wb, l, s5_ops, alpha)
        for acc, val in zip(outs, (mk_p, mv_p, rw_p, sh_p, hr_p, hi_p, rw_s, sh_s, hr_s, hi_s)):
            acc.append(val)
    return (xp, xs) + tuple(jnp.stack(o) for o in outs)
```
